```python
import math
import jax, jax.numpy as jnp
from jax import lax
import numpy as np

D_MODEL = 2048
BATCH = 2
SEQ = 4096
DEPTH = 1

N_Q_HEADS = 32
N_KV_HEADS = 4
HEAD_DIM = 64
WINDOW = 128
ROPE_THETA = 500000.0
ROT_DIM = HEAD_DIM // 4
SGU_GROUPS = 8
SGU_CH = 128
CHUNK = 128
N_EXPERTS = 64
N_EXPERT_GROUPS = 8
TOPK_GROUPS = 4
TOP_K = 8
D_EXPERT = 512
D_SHARED = 512
ROUTED_SCALE = 2.5
EXPERT_BLOCK = 128

ATTN_W = N_Q_HEADS * HEAD_DIM
KV_W = N_KV_HEADS * HEAD_DIM
SGU_W = SGU_GROUPS * SGU_CH
IN_W = ATTN_W + 2 * KV_W + 2 * SGU_W + 2 * D_MODEL
ALPHA = (2 * DEPTH) ** 0.25
BETA = (8 * DEPTH) ** -0.25
LN_EPS = 1e-5

kernel_name = "hybrid_swa_sgu_moe_deepnorm"


def layer_norm(x, g, b):
    xf = x.astype(jnp.float32)
    mu = jnp.mean(xf, axis=-1, keepdims=True)
    var = jnp.mean(jnp.square(xf - mu), axis=-1, keepdims=True)
    y = (xf - mu) * lax.rsqrt(var + LN_EPS)
    return (y * g.astype(jnp.float32) + b.astype(jnp.float32)).astype(x.dtype)


def partial_rope(x, pos):
    half = ROT_DIM // 2
    inv_freq = ROPE_THETA ** (-jnp.arange(0, ROT_DIM, 2, dtype=jnp.float32) / ROT_DIM)
    ang = pos[:, None] * inv_freq[None, :]
    cos = jnp.cos(ang)[None, :, None, :]
    sin = jnp.sin(ang)[None, :, None, :]
    xr = x[..., :ROT_DIM].astype(jnp.float32)
    x1, x2 = xr[..., :half], xr[..., half:]
    rot = jnp.concatenate([x1 * cos - x2 * sin, x2 * cos + x1 * sin], axis=-1)
    return jnp.concatenate([rot.astype(x.dtype), x[..., ROT_DIM:]], axis=-1)


def sliding_window_attention(q, k, v, sinks):
    B, S = q.shape[0], q.shape[1]
    nb = S // WINDOW
    G = N_Q_HEADS // N_KV_HEADS
    qb = q.reshape(B, nb, WINDOW, N_KV_HEADS, G, HEAD_DIM)

    def band(t):
        padded = jnp.pad(t, ((0, 0), (WINDOW, 0), (0, 0), (0, 0)))
        prev = padded[:, :S].reshape(B, nb, WINDOW, N_KV_HEADS, HEAD_DIM)
        cur = t.reshape(B, nb, WINDOW, N_KV_HEADS, HEAD_DIM)
        return jnp.concatenate([prev, cur], axis=2)

    kb, vb = band(k), band(v)
    s = jnp.einsum('bnqhgd,bnkhd->bnhgqk', qb, kb).astype(jnp.float32) * (HEAD_DIM ** -0.5)
    qi = jnp.arange(WINDOW)[:, None]
    kj = jnp.arange(2 * WINDOW)[None, :]
    in_band = (kj > qi) & (kj <= qi + WINDOW)
    not_pad = (jnp.arange(nb)[:, None, None] > 0) | (kj[None] >= WINDOW)
    valid = in_band[None] & not_pad
    s = jnp.where(valid[None, :, None, None], s, -jnp.inf)
    sink = sinks.astype(jnp.float32).reshape(N_KV_HEADS, G)[None, None, :, :, None, None]
    m = jnp.maximum(jnp.max(s, axis=-1, keepdims=True), sink)
    p = jnp.exp(s - m)
    denom = jnp.sum(p, axis=-1, keepdims=True) + jnp.exp(sink - m)
    p = (p / denom).astype(v.dtype)
    o = jnp.einsum('bnhgqk,bnkhd->bnqhgd', p, vb)
    return o.reshape(B, S, ATTN_W)


def spatial_gating(u, vg, ln_g, ln_b, w_s, b_s):
    B, S = u.shape[0], u.shape[1]
    nc = S // CHUNK
    vn = layer_norm(vg, ln_g, ln_b)
    vc = vn.reshape(B, nc, CHUNK, SGU_GROUPS, SGU_CH)
    causal = jnp.tril(jnp.ones((CHUNK, CHUNK), dtype=bool))
    ws = jnp.where(causal[None], w_s, jnp.zeros_like(w_s))
    sv = jnp.einsum('gts,bnsgc->bntgc', ws, vc) + jnp.transpose(b_s)[None, None, :, :, None]
    return (u * sv.reshape(B, S, SGU_GROUPS, SGU_CH)).reshape(B, S, SGU_W)


def swiglu(x, w1, w3, w2):
    return (jax.nn.silu(x @ w1) * (x @ w3)) @ w2


def route(xf, w_router, router_bias):
    N = xf.shape[0]
    scores = jax.nn.sigmoid((xf @ w_router).astype(jnp.float32))
    biased = scores + router_bias.astype(jnp.float32)
    per_group = N_EXPERTS // N_EXPERT_GROUPS
    grp = biased.reshape(N, N_EXPERT_GROUPS, per_group)
    grp_score = jnp.sum(lax.top_k(grp, 2)[0], axis=-1)
    _, top_g = lax.top_k(grp_score, TOPK_GROUPS)
    gmask = jnp.sum(jax.nn.one_hot(top_g, N_EXPERT_GROUPS, dtype=jnp.float32), axis=1) > 0
    emask = jnp.repeat(gmask, per_group, axis=1)
    masked = jnp.where(emask, biased, -jnp.inf)
    _, idx = lax.top_k(masked, TOP_K)
    w = jnp.take_along_axis(scores, idx, axis=1)
    w = w / (jnp.sum(w, axis=-1, keepdims=True) + 1e-20) * ROUTED_SCALE
    return idx, w


def routed_experts(xf, idx, wts, w1, w3, w2):
    N, D = xf.shape
    A = N * TOP_K
    n_blocks = -(-(A + N_EXPERTS * (EXPERT_BLOCK - 1)) // EXPERT_BLOCK)
    P = n_blocks * EXPERT_BLOCK
    flat_e = idx.reshape(-1)
    flat_tok = jnp.repeat(jnp.arange(N, dtype=jnp.int32), TOP_K)
    flat_w = wts.reshape(-1)
    order = jnp.argsort(flat_e)
    se = flat_e[order]
    counts = jnp.zeros((N_EXPERTS,), jnp.int32).at[flat_e].add(1)
    start = jnp.cumsum(counts) - counts
    padded = (counts + EXPERT_BLOCK - 1) // EXPERT_BLOCK * EXPERT_BLOCK
    pad_end = jnp.cumsum(padded)
    pad_start = pad_end - padded
    dest = pad_start[se] + (jnp.arange(A, dtype=jnp.int32) - start[se])
    tok_buf = jnp.full((P,), N, jnp.int32).at[dest].set(flat_tok[order])
    w_buf = jnp.zeros((P,), jnp.float32).at[dest].set(flat_w[order])
    block_start = jnp.arange(n_blocks, dtype=jnp.int32) * EXPERT_BLOCK
    block_e = jnp.minimum(jnp.searchsorted(pad_end, block_start, side='right'), N_EXPERTS - 1)
    xpad = jnp.concatenate([xf, jnp.zeros((1, D), xf.dtype)], axis=0)

    def one_block(args):
        e, tok, w = args
        xb = xpad[tok]
        return (swiglu(xb, w1[e], w3[e], w2[e]) * w[:, None]).astype(xf.dtype)

    outs = lax.map(one_block, (block_e, tok_buf.reshape(n_blocks, EXPERT_BLOCK),
                               w_buf.reshape(n_blocks, EXPERT_BLOCK)))
    y = jnp.zeros((N + 1, D), xf.dtype).at[tok_buf].add(outs.reshape(P, D))
    return y[:N]


def setup_inputs(seed: int = 0) -> dict:
    key = jax.random.key(seed)
    ks = jax.random.split(key, 24)
    L, D = DEPTH, D_MODEL
    f = jnp.float32
    nrm = lambda k, shape, scale: jax.random.normal(k, shape, f) * scale
    return {
        "x": jax.random.normal(ks[0], (BATCH, SEQ, D), f),
        "w_in": nrm(ks[1], (L, D, IN_W), D ** -0.5),
        "b_in": nrm(ks[2], (L, IN_W), 0.02),
        "sinks": nrm(ks[3], (L, N_Q_HEADS), 0.5),
        "sgu_ln_g": 1.0 + nrm(ks[4], (L, SGU_GROUPS, SGU_CH), 0.05),
        "sgu_ln_b": nrm(ks[5], (L, SGU_GROUPS, SGU_CH), 0.02),
        "w_spatial": nrm(ks[6], (L, SGU_GROUPS, CHUNK, CHUNK), CHUNK ** -0.5),
        "b_spatial": 1.0 + nrm(ks[7], (L, SGU_GROUPS, CHUNK), 0.1),
        "w_branch_attn": nrm(ks[8], (L, ATTN_W, D), ATTN_W ** -0.5),
        "w_branch_sgu": nrm(ks[9], (L, SGU_W, D), SGU_W ** -0.5),
        "w_out": nrm(ks[10], (L, D, D), BETA * D ** -0.5),
        "ln1_g": 1.0 + nrm(ks[11], (L, D), 0.05),
        "ln1_b": nrm(ks[12], (L, D), 0.02),
        "w_router": nrm(ks[13], (L, D, N_EXPERTS), D ** -0.5),
        "router_bias": nrm(ks[14], (L, N_EXPERTS), 0.01),
        "w1": nrm(ks[15], (L, N_EXPERTS, D, D_EXPERT), D ** -0.5),
        "w3": nrm(ks[16], (L, N_EXPERTS, D, D_EXPERT), D ** -0.5),
        "w2": nrm(ks[17], (L, N_EXPERTS, D_EXPERT, D), BETA * D_EXPERT ** -0.5),
        "ws1": nrm(ks[18], (L, D, D_SHARED), D ** -0.5),
        "ws3": nrm(ks[19], (L, D, D_SHARED), D ** -0.5),
        "ws2": nrm(ks[20], (L, D_SHARED, D), BETA * D_SHARED ** -0.5),
        "ln2_g": 1.0 + nrm(ks[21], (L, D), 0.05),
        "ln2_b": nrm(ks[22], (L, D), 0.02),
    }


def reference(x, w_in, b_in, sinks, sgu_ln_g, sgu_ln_b, w_spatial, b_spatial,
              w_branch_attn, w_branch_sgu, w_out, ln1_g, ln1_b, w_router, router_bias,
              w1, w3, w2, ws1, ws3, ws2, ln2_g, ln2_b):
    B, S, D = x.shape
    pos = jnp.arange(S, dtype=jnp.float32)
    splits = list(np.cumsum([ATTN_W, KV_W, KV_W, SGU_W, SGU_W, D_MODEL]))
    for l in range(DEPTH):
        h = x @ w_in[l] + b_in[l]
        q, k, v, u_pre, vg_pre, gate_a, gate_b = jnp.split(h, splits, axis=-1)
        q = partial_rope(q.reshape(B, S, N_Q_HEADS, HEAD_DIM), pos)
        k = partial_rope(k.reshape(B, S, N_KV_HEADS, HEAD_DIM), pos)
        v = v.reshape(B, S, N_KV_HEADS, HEAD_DIM)
        attn = sliding_window_attention(q, k, v, sinks[l])
        u = jax.nn.gelu(u_pre).reshape(B, S, SGU_GROUPS, SGU_CH)
        vg = jax.nn.gelu(vg_pre).reshape(B, S, SGU_GROUPS, SGU_CH)
        sgu = spatial_gating(u, vg, sgu_ln_g[l], sgu_ln_b[l], w_spatial[l], b_spatial[l])
        merged = (jax.nn.sigmoid(gate_a) * (attn @ w_branch_attn[l])
                  + jax.nn.sigmoid(gate_b) * (sgu @ w_branch_sgu[l]))
        x = layer_norm(ALPHA * x + merged @ w_out[l], ln1_g[l], ln1_b[l])
        xf = x.reshape(B * S, D)
        idx, wts = route(xf, w_router[l], router_bias[l])
        ffn = routed_experts(xf, idx, wts, w1[l], w3[l], w2[l]) + swiglu(xf, ws1[l], ws3[l], ws2[l])
        x = layer_norm(ALPHA * x + ffn.reshape(B, S, D), ln2_g[l], ln2_b[l])
    return x
```

```python
import functools
import math

import numpy as np
import jax
import jax.numpy as jnp
from jax import lax
from jax.experimental import pallas as pl
from jax.experimental.pallas import tpu as pltpu

D_MODEL = 2048
BATCH = 2
SEQ = 4096
N_TOK = BATCH * SEQ
N_Q_HEADS = 32
N_KV_HEADS = 4
HEAD_DIM = 64
GQA = N_Q_HEADS // N_KV_HEADS
WINDOW = 128
ROPE_THETA = 500000.0
ROT_DIM = HEAD_DIM // 4
SGU_GROUPS = 8
SGU_CH = 128
N_EXPERTS = 64
N_EXPERT_GROUPS = 8
GROUP_SIZE = N_EXPERTS // N_EXPERT_GROUPS
TOPK_GROUPS = 4
TOP_K = 8
D_EXPERT = 512
D_SHARED = 512
ROUTED_SCALE = 2.5
ATTN_W = N_Q_HEADS * HEAD_DIM
KV_W = N_KV_HEADS * HEAD_DIM
SGU_W = SGU_GROUPS * SGU_CH
IN_W = ATTN_W + 2 * KV_W + 2 * SGU_W + 2 * D_MODEL
ALPHA = 2.0 ** 0.25
LN_EPS = 1e-5
N_ASSIGN = N_TOK * TOP_K

LANES = 128
VMEM_LIMIT_CAP = 56 * 1024 * 1024

PROJ_TM = 512
PROJ_TN = 512
MERGE_TM = 256
ROUTE_TT = 512
EXPERT_BM = 128
COMBINE_TM = 128
N_EXPERT_BLOCKS = -(-(N_ASSIGN + N_EXPERTS * (EXPERT_BM - 1)) // EXPERT_BM)
N_SLOTS = N_EXPERT_BLOCKS * EXPERT_BM

H_Q, H_GA, H_GB, H_U, H_VG, H_KV = 0, 2048, 4096, 6144, 7168, 8192
_SRC_TILE = np.array([0, 1, 2, 3, 9, 10, 11, 12, 13, 14, 15, 16, 5, 6, 7, 8, 4], np.int32)
_N_PROJ_TILES = IN_W // PROJ_TN
_Q_TILES = ATTN_W // PROJ_TN
_GATE_END = (H_U) // PROJ_TN
_SGU_END = (H_KV) // PROJ_TN


def _bf16(a):
    return a.astype(jnp.bfloat16)


def _dot(a, b):
    return jnp.dot(a, b, preferred_element_type=jnp.float32)


def _dot_nt(a, b):
    return lax.dot_general(a, b, (((1,), (1,)), ((), ())), preferred_element_type=jnp.float32)


def _rope_slab(x, c, s_next, s_prev):
    return (x * c + pltpu.roll(x, LANES - ROT_DIM // 2, axis=1) * s_next
            + pltpu.roll(x, ROT_DIM // 2, axis=1) * s_prev)


def _in_proj_kernel(src_ref, x_ref, w_ref, b_ref, c_ref, sn_ref, sp_ref, o_ref, wb_ref):
    j = pl.program_id(0)
    i = pl.program_id(1)

    @pl.when(i == 0)
    def _():
        wb_ref[...] = _bf16(w_ref[...])

    acc = _dot(x_ref[...], wb_ref[...]) + b_ref[...]
    n_slab = PROJ_TN // LANES

    @pl.when(j < _Q_TILES)
    def _():
        c, sn, sp = c_ref[...], sn_ref[...], sp_ref[...]
        scale = HEAD_DIM ** -0.5
        for t in range(n_slab):
            sl = slice(t * LANES, (t + 1) * LANES)
            o_ref[:, sl] = _bf16(_rope_slab(acc[:, sl], c, sn, sp) * scale)

    @pl.when(j == _N_PROJ_TILES - 1)
    def _():
        c, sn, sp = c_ref[...], sn_ref[...], sp_ref[...]
        for t in range(n_slab):
            sl = slice(t * LANES, (t + 1) * LANES)
            if t < KV_W // LANES:
                o_ref[:, sl] = _bf16(_rope_slab(acc[:, sl], c, sn, sp))
            else:
                o_ref[:, sl] = _bf16(acc[:, sl])

    @pl.when((j >= _Q_TILES) & (j < _GATE_END))
    def _():
        o_ref[...] = _bf16(jax.nn.sigmoid(acc))

    @pl.when((j >= _GATE_END) & (j < _SGU_END))
    def _():
        o_ref[...] = _bf16(jax.nn.gelu(acc))


def _rope_tables():
    half = ROT_DIM // 2
    inv_freq = ROPE_THETA ** (-np.arange(0, ROT_DIM, 2, dtype=np.float32) / ROT_DIM)
    pos = np.arange(SEQ, dtype=np.float32)
    ang = jnp.asarray(pos[:, None] * inv_freq[None, :].astype(np.float32), jnp.float32)
    cos, sin = jnp.cos(ang), jnp.sin(ang)
    ones = jnp.ones((SEQ, HEAD_DIM - ROT_DIM), jnp.float32)
    zeros = jnp.zeros((SEQ, HEAD_DIM - ROT_DIM), jnp.float32)
    zh = jnp.zeros((SEQ, half), jnp.float32)
    c = jnp.concatenate([cos, cos, ones], axis=1)
    s_next = jnp.concatenate([-sin, zh, zeros], axis=1)
    s_prev = jnp.concatenate([zh, sin, zeros], axis=1)
    rep = LANES // HEAD_DIM
    return tuple(jnp.tile(t, (1, rep)) for t in (c, s_next, s_prev))


def _in_proj(x_bf, w_in, b_in):
    c, sn, sp = _rope_tables()
    n_i = N_TOK // PROJ_TM
    pos_tiles = SEQ // PROJ_TM
    tbl = pl.BlockSpec((PROJ_TM, LANES), lambda j, i, src: (i % pos_tiles, 0))
    grid_spec = pltpu.PrefetchScalarGridSpec(
        num_scalar_prefetch=1,
        grid=(_N_PROJ_TILES, n_i),
        in_specs=[
            pl.BlockSpec((PROJ_TM, D_MODEL), lambda j, i, src: (i, 0)),
            pl.BlockSpec((D_MODEL, PROJ_TN), lambda j, i, src: (0, src[j])),
            pl.BlockSpec((1, PROJ_TN), lambda j, i, src: (0, src[j])),
            tbl, tbl, tbl,
        ],
        out_specs=pl.BlockSpec((PROJ_TM, PROJ_TN), lambda j, i, src: (i, j)),
        scratch_shapes=[pltpu.VMEM((D_MODEL, PROJ_TN), jnp.bfloat16)],
    )
    return pl.pallas_call(
        _in_proj_kernel,
        grid_spec=grid_spec,
        out_shape=jax.ShapeDtypeStruct((N_TOK, IN_W), jnp.bfloat16),
        compiler_params=pltpu.CompilerParams(
            dimension_semantics=("arbitrary", "arbitrary"),
            vmem_limit_bytes=40 * 1024 * 1024),
        name="in_proj",
    )(jnp.asarray(_SRC_TILE), x_bf, w_in, b_in, c, sn, sp)


def _mixers_kernel(sink_ref, q_ref, kvc_ref, kvp_ref, u_ref, vg_ref, lng_ref, lnb_ref,
                   ws_ref, bs_ref, attn_ref, sgu_ref):
    n = pl.program_id(0)
    w = WINDOW
    first_key = jnp.where((n % (SEQ // w)) == 0, w, 0)
    pairs = GQA // 2
    rows = pairs * w

    qi = lax.broadcasted_iota(jnp.int32, (w, 2 * w), 0)
    kj = lax.broadcasted_iota(jnp.int32, (w, 2 * w), 1)
    valid = (kj > qi) & (kj <= qi + w) & (kj >= first_key)
    valid = jnp.concatenate([valid] * pairs, axis=0)
    lane = lax.broadcasted_iota(jnp.int32, (2 * w, LANES), 1)
    low = lane < HEAD_DIM
    lane_r = lax.broadcasted_iota(jnp.int32, (rows, LANES), 1)
    low_r = lane_r < HEAD_DIM
    ones_low = jnp.where(low, 1.0, 0.0).astype(jnp.bfloat16)
    ones_high = jnp.where(low, 0.0, 1.0).astype(jnp.bfloat16)
    neg_inf = jnp.float32(-jnp.inf)

    kv = jnp.concatenate([kvp_ref[...], kvc_ref[...]], axis=0).astype(jnp.float32)

    def padded(group, head_is_high):
        rolled = pltpu.roll(group, HEAD_DIM, axis=1)
        if head_is_high:
            lo_half, hi_half = rolled, group
        else:
            lo_half, hi_half = group, rolled
        return (_bf16(jnp.where(low, lo_half, 0.0)), _bf16(jnp.where(low, 0.0, hi_half)))

    for h in range(N_KV_HEADS):
        g0 = (h // 2) * LANES
        k_lo, k_hi = padded(kv[:, g0:g0 + LANES], h % 2 == 1)
        v_lo, v_hi = padded(kv[:, KV_W + g0:KV_W + g0 + LANES], h % 2 == 1)
        r_even = jnp.concatenate([v_lo, ones_low], axis=1)
        r_odd = jnp.concatenate([v_hi, ones_high], axis=1)
        q4 = jnp.concatenate(
            [q_ref[:, (h * pairs + p) * LANES:(h * pairs + p + 1) * LANES] for p in range(pairs)],
            axis=0)
        sink_e = jnp.concatenate(
            [jnp.full((w, 1), sink_ref[h * GQA + 2 * p], jnp.float32) for p in range(pairs)], axis=0)
        sink_o = jnp.concatenate(
            [jnp.full((w, 1), sink_ref[h * GQA + 2 * p + 1], jnp.float32) for p in range(pairs)], axis=0)

        s_e = jnp.where(valid, _dot_nt(q4, k_lo), neg_inf)
        s_o = jnp.where(valid, _dot_nt(q4, k_hi), neg_inf)
        m_e = jnp.maximum(jnp.max(s_e, axis=1, keepdims=True), sink_e)
        m_o = jnp.maximum(jnp.max(s_o, axis=1, keepdims=True), sink_o)
        p_e = _bf16(jnp.exp(s_e - m_e))
        p_o = _bf16(jnp.exp(s_o - m_o))
        acc = _dot(p_e, r_even) + _dot(p_o, r_odd)
        sink_term = jnp.exp(jnp.where(low_r, sink_e - m_e, sink_o - m_o))
        out = acc[:, :LANES] / (acc[:, LANES:] + sink_term)
        for p in range(pairs):
            c0 = (h * pairs + p) * LANES
            attn_ref[:, c0:c0 + LANES] = _bf16(out[p * w:(p + 1) * w])

    ti = lax.broadcasted_iota(jnp.int32, (w, w), 0)
    si = lax.broadcasted_iota(jnp.int32, (w, w), 1)
    causal = si <= ti
    for g in range(SGU_GROUPS):
        sl = slice(g * SGU_CH, (g + 1) * SGU_CH)
        x = vg_ref[:, sl].astype(jnp.float32)
        mu = jnp.mean(x, axis=-1, keepdims=True)
        xc = x - mu
        var = jnp.mean(xc * xc, axis=-1, keepdims=True)
        vn = xc * lax.rsqrt(var + LN_EPS) * lng_ref[:, sl] + lnb_ref[:, sl]
        wsg = _bf16(jnp.where(causal, ws_ref[g], 0.0))
        sv = _dot(wsg, _bf16(vn)) + bs_ref[g]
        sgu_ref[:, sl] = _bf16(u_ref[:, sl].astype(jnp.float32) * sv)


def _mixers(h, sinks, ln_g, ln_b, w_s, b_s):
    w = WINDOW
    nb = N_TOK // w
    kv_blk = H_KV // (2 * KV_W)
    grid_spec = pltpu.PrefetchScalarGridSpec(
        num_scalar_prefetch=0,
        grid=(nb,),
        in_specs=[
            pl.BlockSpec(memory_space=pltpu.SMEM),
            pl.BlockSpec((w, ATTN_W), lambda n: (n, H_Q // ATTN_W)),
            pl.BlockSpec((w, 2 * KV_W), lambda n: (n, kv_blk)),
            pl.BlockSpec((w, 2 * KV_W), lambda n: (jnp.maximum(n - 1, 0), kv_blk)),
            pl.BlockSpec((w, SGU_W), lambda n: (n, H_U // SGU_W)),
            pl.BlockSpec((w, SGU_W), lambda n: (n, H_VG // SGU_W)),
            pl.BlockSpec((1, SGU_W), lambda n: (0, 0)),
            pl.BlockSpec((1, SGU_W), lambda n: (0, 0)),
            pl.BlockSpec((SGU_GROUPS, w, w), lambda n: (0, 0, 0)),
            pl.BlockSpec((SGU_GROUPS, w, 1), lambda n: (0, 0, 0)),
        ],
        out_specs=[
            pl.BlockSpec((w, ATTN_W), lambda n: (n, 0)),
            pl.BlockSpec((w, SGU_W), lambda n: (n, 0)),
        ],
    )
    return pl.pallas_call(
        _mixers_kernel,
        grid_spec=grid_spec,
        out_shape=[jax.ShapeDtypeStruct((N_TOK, ATTN_W), jnp.bfloat16),
                   jax.ShapeDtypeStruct((N_TOK, SGU_W), jnp.bfloat16)],
        compiler_params=pltpu.CompilerParams(
            dimension_semantics=("arbitrary",), vmem_limit_bytes=32 * 1024 * 1024),
        name="mixers",
    )(sinks, h, h, h, h, h, ln_g.reshape(1, SGU_W), ln_b.reshape(1, SGU_W), w_s,
      b_s.reshape(SGU_GROUPS, w, 1))


def _layer_norm(z, g, b):
    mu = jnp.mean(z, axis=-1, keepdims=True)
    zc = z - mu
    var = jnp.mean(zc * zc, axis=-1, keepdims=True)
    return zc * lax.rsqrt(var + LN_EPS) * g + b


def _merge_kernel(attn_ref, sgu_ref, ga_ref, gb_ref, x_ref, wa_ref, wb_ref, wo_ref, g_ref, b_ref,
                  wr_ref, x1_ref, lg_ref):
    mix = (ga_ref[...].astype(jnp.float32) * _dot(attn_ref[...], wa_ref[...])
           + gb_ref[...].astype(jnp.float32) * _dot(sgu_ref[...], wb_ref[...]))
    z = ALPHA * x_ref[...] + _dot(_bf16(mix), wo_ref[...])
    x1 = _layer_norm(z, g_ref[...], b_ref[...])
    x1_ref[...] = x1
    lg_ref[...] = lax.dot_general(wr_ref[...], x1, (((1,), (1,)), ((), ())),
                                  precision=lax.Precision.HIGHEST,
                                  preferred_element_type=jnp.float32)


def _merge(attn, sgu, h, x2d, wa, wb, wo, g, b, wr_t):
    tm = MERGE_TM
    resident = pl.Buffered(1)
    grid_spec = pltpu.PrefetchScalarGridSpec(
        num_scalar_prefetch=0,
        grid=(N_TOK // tm,),
        in_specs=[
            pl.BlockSpec((tm, ATTN_W), lambda i: (i, 0)),
            pl.BlockSpec((tm, SGU_W), lambda i: (i, 0)),
            pl.BlockSpec((tm, D_MODEL), lambda i: (i, H_GA // D_MODEL)),
            pl.BlockSpec((tm, D_MODEL), lambda i: (i, H_GB // D_MODEL)),
            pl.BlockSpec((tm, D_MODEL), lambda i: (i, 0)),
            pl.BlockSpec((ATTN_W, D_MODEL), lambda i: (0, 0), pipeline_mode=resident),
            pl.BlockSpec((SGU_W, D_MODEL), lambda i: (0, 0), pipeline_mode=resident),
            pl.BlockSpec((D_MODEL, D_MODEL), lambda i: (0, 0), pipeline_mode=resident),
            pl.BlockSpec((1, D_MODEL), lambda i: (0, 0)),
            pl.BlockSpec((1, D_MODEL), lambda i: (0, 0)),
            pl.BlockSpec((N_EXPERTS, D_MODEL), lambda i: (0, 0)),
        ],
        out_specs=[
            pl.BlockSpec((tm, D_MODEL), lambda i: (i, 0)),
            pl.BlockSpec((N_EXPERTS, tm), lambda i: (0, i)),
        ],
    )
    return pl.pallas_call(
        _merge_kernel,
        grid_spec=grid_spec,
        out_shape=[jax.ShapeDtypeStruct((N_TOK, D_MODEL), jnp.float32),
                   jax.ShapeDtypeStruct((N_EXPERTS, N_TOK), jnp.float32)],
        compiler_params=pltpu.CompilerParams(
            dimension_semantics=("arbitrary",), vmem_limit_bytes=VMEM_LIMIT_CAP),
        name="merge",
    )(attn, sgu, h, h, x2d, wa, wb, wo, g, b, wr_t)


def _first_argmax(v, rows):
    m = jnp.max(v, axis=0, keepdims=True)
    i = jnp.min(jnp.where(v == m, rows, float(v.shape[0])), axis=0, keepdims=True)
    return m, i


def _row_index(shape):
    return lax.broadcasted_iota(jnp.int32, shape, 0).astype(jnp.float32)


def _route_kernel(lg_ref, bias_ref, idx_ref, w_ref):
    tt = lg_ref.shape[1]
    neg_inf = jnp.float32(-jnp.inf)
    scores = jax.nn.sigmoid(lg_ref[...])
    biased = scores + bias_ref[...]
    row_g = _row_index((GROUP_SIZE, tt))
    gs = []
    for g in range(N_EXPERT_GROUPS):
        blk = biased[g * GROUP_SIZE:(g + 1) * GROUP_SIZE]
        m1, i1 = _first_argmax(blk, row_g)
        m2 = jnp.max(jnp.where(row_g == i1, neg_inf, blk), axis=0, keepdims=True)
        gs.append(m1 + m2)
    cur = jnp.concatenate(gs, axis=0)
    row_n = _row_index((N_EXPERT_GROUPS, tt))
    sel = jnp.zeros((N_EXPERT_GROUPS, tt), jnp.float32)
    for _ in range(TOPK_GROUPS):
        _, i = _first_argmax(cur, row_n)
        hit = row_n == i
        sel = jnp.where(hit, 1.0, sel)
        cur = jnp.where(hit, neg_inf, cur)
    emask = jnp.concatenate(
        [jnp.broadcast_to(sel[g:g + 1], (GROUP_SIZE, tt)) for g in range(N_EXPERT_GROUPS)], axis=0)
    masked = jnp.where(emask > 0.5, biased, neg_inf)
    row_e = _row_index((N_EXPERTS, tt))
    idx_rows, w_rows = [], []
    for _ in range(TOP_K):
        _, i = _first_argmax(masked, row_e)
        hit = row_e == i
        w_rows.append(jnp.sum(jnp.where(hit, scores, 0.0), axis=0, keepdims=True))
        idx_rows.append(i)
        masked = jnp.where(hit, neg_inf, masked)
    wsel = jnp.concatenate(w_rows, axis=0)
    idx_ref[...] = jnp.concatenate(idx_rows, axis=0).astype(jnp.int32)
    w_ref[...] = wsel / (jnp.sum(wsel, axis=0, keepdims=True) + 1e-20) * ROUTED_SCALE


def _route(logits_t, bias):
    tt = ROUTE_TT
    return pl.pallas_call(
        _route_kernel,
        grid=(N_TOK // tt,),
        in_specs=[pl.BlockSpec((N_EXPERTS, tt), lambda i: (0, i)),
                  pl.BlockSpec((N_EXPERTS, 1), lambda i: (0, 0))],
        out_specs=[pl.BlockSpec((TOP_K, tt), lambda i: (0, i)),
                   pl.BlockSpec((TOP_K, tt), lambda i: (0, i))],
        out_shape=[jax.ShapeDtypeStruct((TOP_K, N_TOK), jnp.int32),
                   jax.ShapeDtypeStruct((TOP_K, N_TOK), jnp.float32)],
        compiler_params=pltpu.CompilerParams(dimension_semantics=("arbitrary",)),
        name="route",
    )(logits_t, bias.reshape(N_EXPERTS, 1))


def _dispatch_plan(idx_t):
    bm = EXPERT_BM
    flat_e = idx_t.T.reshape(-1)
    counts = jnp.sum((flat_e[:, None] == jnp.arange(N_EXPERTS, dtype=jnp.int32)[None, :])
                     .astype(jnp.int32), axis=0)
    padded = (counts + bm - 1) // bm * bm
    pad_end = jnp.cumsum(padded)
    fill_end = jnp.cumsum(padded - counts)
    n_fill = N_SLOTS - N_ASSIGN
    fill_key = jnp.sum((jnp.arange(n_fill, dtype=jnp.int32)[:, None] >= fill_end[None, :])
                       .astype(jnp.int32), axis=1)
    keys = jnp.concatenate([flat_e, fill_key])
    src = jnp.argsort(keys, stable=True).astype(jnp.int32)
    real = src < N_ASSIGN
    slot = jnp.arange(N_SLOTS, dtype=jnp.int32)
    dst = jnp.where(real, (src % TOP_K) * N_TOK + src // TOP_K, N_ASSIGN + slot % (2 * bm))
    used = jnp.any(real.reshape(N_EXPERT_BLOCKS, bm), axis=1)
    block_start = jnp.arange(N_EXPERT_BLOCKS, dtype=jnp.int32) * bm
    block_e = jnp.sum((block_start[:, None] >= pad_end[None, :]).astype(jnp.int32), axis=1)
    n_used = jnp.sum(used.astype(jnp.int32))
    last_e = block_e[jnp.maximum(n_used - 1, 0)]
    block_e = jnp.where(used, jnp.minimum(block_e, N_EXPERTS - 1), last_e).astype(jnp.int32)
    prev_e = jnp.concatenate([jnp.full((1,), -1, jnp.int32), block_e[:-1]])
    fresh = (block_e != prev_e).astype(jnp.int32)
    return dst.reshape(N_EXPERT_BLOCKS, 1, bm), block_e, used.astype(jnp.int32), fresh


def _experts_kernel(be_ref, used_ref, fresh_ref, dst_ref, dstn_ref, x_hbm, w1_ref, w3_ref, w2_ref,
                    y_hbm, xbuf, obuf, w1b, w3b, w2b, gsem, ssem):
    b = pl.program_id(0)
    nb = pl.num_programs(0)
    bm = EXPERT_BM
    slot = b % 2

    def gather_rows(idx_ref, s):
        for r in range(bm):
            tok = idx_ref[0, 0, r] & (N_TOK - 1)
            pltpu.make_async_copy(x_hbm.at[pl.ds(tok, 1)], xbuf.at[s, pl.ds(r, 1)], gsem.at[s]).start()

    def wait_gather(s):
        pltpu.make_async_copy(x_hbm.at[pl.ds(0, bm)], xbuf.at[s], gsem.at[s]).wait()

    def wait_scatter(s):
        pltpu.make_async_copy(obuf.at[s], y_hbm.at[pl.ds(0, bm)], ssem.at[s]).wait()

    @pl.when(b == 0)
    def _():
        obuf[1] = jnp.zeros((bm, D_MODEL), jnp.float32)
        for region in range(2):
            fill = pltpu.make_async_copy(
                obuf.at[1], y_hbm.at[pl.ds(N_ASSIGN + region * bm, bm)], ssem.at[1])
            fill.start()
            fill.wait()

    @pl.when((b == 0) & (used_ref[0] == 1))
    def _():
        gather_rows(dst_ref, 0)

    @pl.when((b + 1 < nb) & (used_ref[jnp.minimum(b + 1, nb - 1)] == 1))
    def _():
        gather_rows(dstn_ref, 1 - slot)

    @pl.when((b >= 2) & (used_ref[jnp.maximum(b - 2, 0)] == 1))
    def _():
        wait_scatter(slot)

    @pl.when(used_ref[b] == 1)
    def _():
        @pl.when(fresh_ref[b] == 1)
        def _():
            w1b[...] = _bf16(w1_ref[0])
            w3b[...] = _bf16(w3_ref[0])
            w2b[...] = _bf16(w2_ref[0])

        wait_gather(slot)
        xb = _bf16(xbuf[slot])
        a = _bf16(jax.nn.silu(_dot(xb, w1b[...])) * _dot(xb, w3b[...]))
        obuf[slot] = _dot(a, w2b[...])
        for r in range(bm):
            row = dst_ref[0, 0, r]
            pltpu.make_async_copy(obuf.at[slot, pl.ds(r, 1)], y_hbm.at[pl.ds(row, 1)], ssem.at[slot]).start()

    @pl.when(b == nb - 1)
    def _():
        @pl.when((b >= 1) & (used_ref[jnp.maximum(b - 1, 0)] == 1))
        def _():
            wait_scatter(1 - slot)

        @pl.when(used_ref[b] == 1)
        def _():
            wait_scatter(slot)


def _experts(x1, w1, w3, w2, dst, block_e, used, fresh):
    bm = EXPERT_BM
    nb = N_EXPERT_BLOCKS
    smem_blk = lambda f: pl.BlockSpec((1, 1, bm), f, memory_space=pltpu.SMEM)
    grid_spec = pltpu.PrefetchScalarGridSpec(
        num_scalar_prefetch=3,
        grid=(nb,),
        in_specs=[
            smem_blk(lambda b, be, us, fr: (b, 0, 0)),
            smem_blk(lambda b, be, us, fr: (jnp.minimum(b + 1, nb - 1), 0, 0)),
            pl.BlockSpec(memory_space=pl.ANY),
            pl.BlockSpec((1, D_MODEL, D_EXPERT), lambda b, be, us, fr: (be[b], 0, 0)),
            pl.BlockSpec((1, D_MODEL, D_EXPERT), lambda b, be, us, fr: (be[b], 0, 0)),
            pl.BlockSpec((1, D_EXPERT, D_MODEL), lambda b, be, us, fr: (be[b], 0, 0)),
        ],
        out_specs=pl.BlockSpec(memory_space=pl.ANY),
        scratch_shapes=[
            pltpu.VMEM((2, bm, D_MODEL), jnp.float32),
            pltpu.VMEM((2, bm, D_MODEL), jnp.float32),
            pltpu.VMEM((D_MODEL, D_EXPERT), jnp.bfloat16),
            pltpu.VMEM((D_MODEL, D_EXPERT), jnp.bfloat16),
            pltpu.VMEM((D_EXPERT, D_MODEL), jnp.bfloat16),
            pltpu.SemaphoreType.DMA((2,)),
            pltpu.SemaphoreType.DMA((2,)),
        ],
    )
    return pl.pallas_call(
        _experts_kernel,
        grid_spec=grid_spec,
        out_shape=jax.ShapeDtypeStruct((N_ASSIGN + 2 * bm, D_MODEL), jnp.float32),
        compiler_params=pltpu.CompilerParams(
            dimension_semantics=("arbitrary",), vmem_limit_bytes=48 * 1024 * 1024),
        name="experts",
    )(block_e, used, fresh, dst, dst, x1, w1, w3, w2)


def _combine_kernel(*refs):
    y_refs = refs[:TOP_K]
    w_ref, x_ref, s1_ref, s3_ref, s2_ref, g_ref, b_ref, o_ref = refs[TOP_K:]
    x1 = x_ref[...]
    xb = _bf16(x1)
    a = _bf16(jax.nn.silu(_dot(xb, s1_ref[...])) * _dot(xb, s3_ref[...]))
    ffn = _dot(a, s2_ref[...])
    wts = w_ref[...]
    routed = wts[:, 0:1] * y_refs[0][...]
    for k in range(1, TOP_K):
        routed = routed + wts[:, k:k + 1] * y_refs[k][...]
    o_ref[...] = _layer_norm(ALPHA * x1 + (routed + ffn), g_ref[...], b_ref[...])


def _combine(y, wts, x1, s1, s3, s2, g, b):
    tm = COMBINE_TM
    nt = N_TOK // tm
    y_specs = [pl.BlockSpec((tm, D_MODEL), functools.partial(lambda i, k: (k * nt + i, 0), k=k))
               for k in range(TOP_K)]
    return pl.pallas_call(
        _combine_kernel,
        grid=(nt,),
        in_specs=y_specs + [
            pl.BlockSpec((tm, TOP_K), lambda i: (i, 0)),
            pl.BlockSpec((tm, D_MODEL), lambda i: (i, 0)),
            pl.BlockSpec((D_MODEL, D_SHARED), lambda i: (0, 0)),
            pl.BlockSpec((D_MODEL, D_SHARED), lambda i: (0, 0)),
            pl.BlockSpec((D_SHARED, D_MODEL), lambda i: (0, 0)),
            pl.BlockSpec((1, D_MODEL), lambda i: (0, 0)),
            pl.BlockSpec((1, D_MODEL), lambda i: (0, 0)),
        ],
        out_specs=pl.BlockSpec((tm, D_MODEL), lambda i: (i, 0)),
        out_shape=jax.ShapeDtypeStruct((N_TOK, D_MODEL), jnp.float32),
        compiler_params=pltpu.CompilerParams(
            dimension_semantics=("arbitrary",), vmem_limit_bytes=VMEM_LIMIT_CAP),
        name="combine",
    )(*([y] * TOP_K), wts, x1, s1, s3, s2, g, b)


def kernel(x, w_in, b_in, sinks, sgu_ln_g, sgu_ln_b, w_spatial, b_spatial, w_branch_attn,
           w_branch_sgu, w_out, ln1_g, ln1_b, w_router, router_bias, w1, w3, w2, ws1, ws3, ws2,
           ln2_g, ln2_b):
    assert x.shape == (BATCH, SEQ, D_MODEL) and w_in.shape == (1, D_MODEL, IN_W)
    x2d = x.reshape(N_TOK, D_MODEL)
    h = _in_proj(_bf16(x2d), w_in[0], b_in)
    attn, sgu = _mixers(h, sinks[0], sgu_ln_g[0], sgu_ln_b[0], w_spatial[0], b_spatial[0])
    x1, logits_t = _merge(attn, sgu, h, x2d, _bf16(w_branch_attn[0]), _bf16(w_branch_sgu[0]),
                          _bf16(w_out[0]), ln1_g, ln1_b, w_router[0].T)
    idx_t, w_t = _route(logits_t, router_bias[0])
    dst, block_e, used, fresh = _dispatch_plan(idx_t)
    y = _experts(x1, w1[0], w3[0], w2[0], dst, block_e, used, fresh)
    out = _combine(y, w_t.T, x1, _bf16(ws1[0]), _bf16(ws3[0]), _bf16(ws2[0]), ln2_g, ln2_b)
    return out.reshape(BATCH, SEQ, D_MODEL)
```

```python
import functools
import math

import numpy as np
import jax
import jax.numpy as jnp
from jax import lax
from jax.experimental import pallas as pl
from jax.experimental.pallas import tpu as pltpu

D_MODEL = 2048
BATCH = 2
SEQ = 4096
N_TOK = BATCH * SEQ
N_Q_HEADS = 32
N_KV_HEADS = 4
HEAD_DIM = 64
GQA = N_Q_HEADS // N_KV_HEADS
WINDOW = 128
ROPE_THETA = 500000.0
ROT_DIM = HEAD_DIM // 4
SGU_GROUPS = 8
SGU_CH = 128
N_EXPERTS = 64
N_EXPERT_GROUPS = 8
GROUP_SIZE = N_EXPERTS // N_EXPERT_GROUPS
TOPK_GROUPS = 4
TOP_K = 8
D_EXPERT = 512
D_SHARED = 512
ROUTED_SCALE = 2.5
ATTN_W = N_Q_HEADS * HEAD_DIM
KV_W = N_KV_HEADS * HEAD_DIM
SGU_W = SGU_GROUPS * SGU_CH
IN_W = ATTN_W + 2 * KV_W + 2 * SGU_W + 2 * D_MODEL
ALPHA = 2.0 ** 0.25
LN_EPS = 1e-5
N_ASSIGN = N_TOK * TOP_K

LANES = 128
VMEM_LIMIT_CAP = 56 * 1024 * 1024

PROJ_TM = 512
PROJ_TN = 1024
MERGE_TM = 256
ROUTE_TT = 512
EXPERT_BM = 256
COMBINE_TM = 128
N_EXPERT_BLOCKS = -(-(N_ASSIGN + N_EXPERTS * (EXPERT_BM - 1)) // EXPERT_BM)
N_SLOTS = N_EXPERT_BLOCKS * EXPERT_BM

H_Q, H_GA, H_GB, H_U, H_VG = 0, 2048, 4096, 6144, 7168
H_W = 8192
PROJ_UNIT = 512
_SRC_UNIT = np.array([0, 1, 2, 3, 9, 10, 11, 12, 13, 14, 15, 16, 5, 6, 7, 8], np.int32)
_SRC_UNIT_KV = ATTN_W // PROJ_UNIT
_N_PROJ_TILES = H_W // PROJ_TN
_Q_TILES = ATTN_W // PROJ_TN
_GATE_END = H_U // PROJ_TN


def _bf16(a):
    return a.astype(jnp.bfloat16)


def _dot(a, b):
    return jnp.dot(a, b, preferred_element_type=jnp.float32)


def _dot_nt(a, b):
    return lax.dot_general(a, b, (((1,), (1,)), ((), ())), preferred_element_type=jnp.float32)


def _rope_slab(x, c, s_next, s_prev):
    return (x * c + pltpu.roll(x, LANES - ROT_DIM // 2, axis=1) * s_next
            + pltpu.roll(x, ROT_DIM // 2, axis=1) * s_prev)


def _in_proj_kernel(src_ref, x_ref, wa_ref, wb_ref, ba_ref, bb_ref, c_ref, sn_ref, sp_ref, o_ref,
                    wbf_ref):
    j = pl.program_id(0)
    i = pl.program_id(1)

    @pl.when(i == 0)
    def _():
        wbf_ref[:, :PROJ_UNIT] = _bf16(wa_ref[...])
        wbf_ref[:, PROJ_UNIT:] = _bf16(wb_ref[...])

    def project():
        bias = jnp.concatenate([ba_ref[...], bb_ref[...]], axis=1)
        return _dot(x_ref[...], wbf_ref[...]) + bias

    @pl.when(j < _Q_TILES)
    def _():
        acc = project()
        c, sn, sp = c_ref[...], sn_ref[...], sp_ref[...]
        scale = HEAD_DIM ** -0.5
        for t in range(PROJ_TN // LANES):
            sl = slice(t * LANES, (t + 1) * LANES)
            o_ref[:, sl] = _bf16(_rope_slab(acc[:, sl], c, sn, sp) * scale)

    @pl.when((j >= _Q_TILES) & (j < _GATE_END))
    def _():
        o_ref[...] = _bf16(jax.nn.sigmoid(project()))

    @pl.when(j >= _GATE_END)
    def _():
        o_ref[...] = _bf16(jax.nn.gelu(project()))


def _kv_proj_kernel(x_ref, w_ref, b_ref, c_ref, sn_ref, sp_ref, o_ref, wbf_ref):
    @pl.when(pl.program_id(0) == 0)
    def _():
        wbf_ref[...] = _bf16(w_ref[...])

    acc = _dot(x_ref[...], wbf_ref[...]) + b_ref[...]
    c, sn, sp = c_ref[...], sn_ref[...], sp_ref[...]
    for t in range(2 * KV_W // LANES):
        sl = slice(t * LANES, (t + 1) * LANES)
        if t < KV_W // LANES:
            o_ref[:, sl] = _bf16(_rope_slab(acc[:, sl], c, sn, sp))
        else:
            o_ref[:, sl] = _bf16(acc[:, sl])


def _rope_tables():
    half = ROT_DIM // 2
    inv_freq = ROPE_THETA ** (-np.arange(0, ROT_DIM, 2, dtype=np.float32) / ROT_DIM)
    pos = np.arange(SEQ, dtype=np.float32)
    ang = jnp.asarray(pos[:, None] * inv_freq[None, :].astype(np.float32), jnp.float32)
    cos, sin = jnp.cos(ang), jnp.sin(ang)
    ones = jnp.ones((SEQ, HEAD_DIM - ROT_DIM), jnp.float32)
    zeros = jnp.zeros((SEQ, HEAD_DIM - ROT_DIM), jnp.float32)
    zh = jnp.zeros((SEQ, half), jnp.float32)
    c = jnp.concatenate([cos, cos, ones], axis=1)
    s_next = jnp.concatenate([-sin, zh, zeros], axis=1)
    s_prev = jnp.concatenate([zh, sin, zeros], axis=1)
    rep = LANES // HEAD_DIM
    return tuple(jnp.tile(t, (1, rep)) for t in (c, s_next, s_prev))


def _in_proj(x_bf, w_in, b_in):
    c, sn, sp = _rope_tables()
    n_i = N_TOK // PROJ_TM
    pos_tiles = SEQ // PROJ_TM
    tbl = pl.BlockSpec((PROJ_TM, LANES), lambda j, i, src: (i % pos_tiles, 0))
    grid_spec = pltpu.PrefetchScalarGridSpec(
        num_scalar_prefetch=1,
        grid=(_N_PROJ_TILES, n_i),
        in_specs=[
            pl.BlockSpec((PROJ_TM, D_MODEL), lambda j, i, src: (i, 0)),
            pl.BlockSpec((D_MODEL, PROJ_UNIT), lambda j, i, src: (0, src[2 * j])),
            pl.BlockSpec((D_MODEL, PROJ_UNIT), lambda j, i, src: (0, src[2 * j + 1])),
            pl.BlockSpec((1, PROJ_UNIT), lambda j, i, src: (0, src[2 * j])),
            pl.BlockSpec((1, PROJ_UNIT), lambda j, i, src: (0, src[2 * j + 1])),
            tbl, tbl, tbl,
        ],
        out_specs=pl.BlockSpec((PROJ_TM, PROJ_TN), lambda j, i, src: (i, j)),
        scratch_shapes=[pltpu.VMEM((D_MODEL, PROJ_TN), jnp.bfloat16)],
    )
    h = pl.pallas_call(
        _in_proj_kernel,
        grid_spec=grid_spec,
        out_shape=jax.ShapeDtypeStruct((N_TOK, H_W), jnp.bfloat16),
        compiler_params=pltpu.CompilerParams(
            dimension_semantics=("arbitrary", "arbitrary"),
            vmem_limit_bytes=48 * 1024 * 1024),
        name="in_proj",
    )(jnp.asarray(_SRC_UNIT), x_bf, w_in, w_in, b_in, b_in, c, sn, sp)

    tbl1 = pl.BlockSpec((PROJ_TM, LANES), lambda i: (i % pos_tiles, 0))
    hkv = pl.pallas_call(
        _kv_proj_kernel,
        grid=(n_i,),
        in_specs=[
            pl.BlockSpec((PROJ_TM, D_MODEL), lambda i: (i, 0)),
            pl.BlockSpec((D_MODEL, 2 * KV_W), lambda i: (0, _SRC_UNIT_KV)),
            pl.BlockSpec((1, 2 * KV_W), lambda i: (0, _SRC_UNIT_KV)),
            tbl1, tbl1, tbl1,
        ],
        out_specs=pl.BlockSpec((PROJ_TM, 2 * KV_W), lambda i: (i, 0)),
        out_shape=jax.ShapeDtypeStruct((N_TOK, 2 * KV_W), jnp.bfloat16),
        scratch_shapes=[pltpu.VMEM((D_MODEL, 2 * KV_W), jnp.bfloat16)],
        compiler_params=pltpu.CompilerParams(
            dimension_semantics=("arbitrary",), vmem_limit_bytes=32 * 1024 * 1024),
        name="kv_proj",
    )(x_bf, w_in, b_in, c, sn, sp)
    return h, hkv


def _mixers_kernel(sink_ref, q_ref, kvc_ref, kvp_ref, u_ref, vg_ref, lng_ref, lnb_ref,
                   ws_ref, bs_ref, attn_ref, sgu_ref):
    n = pl.program_id(0)
    w = WINDOW
    first_key = jnp.where((n % (SEQ // w)) == 0, w, 0)
    pairs = GQA // 2
    rows = pairs * w

    qi = lax.broadcasted_iota(jnp.int32, (w, 2 * w), 0)
    kj = lax.broadcasted_iota(jnp.int32, (w, 2 * w), 1)
    valid = (kj > qi) & (kj <= qi + w) & (kj >= first_key)
    valid = jnp.concatenate([valid] * pairs, axis=0)
    lane = lax.broadcasted_iota(jnp.int32, (2 * w, LANES), 1)
    low = lane < HEAD_DIM
    lane_r = lax.broadcasted_iota(jnp.int32, (rows, LANES), 1)
    low_r = lane_r < HEAD_DIM
    ones_low = jnp.where(low, 1.0, 0.0).astype(jnp.bfloat16)
    ones_high = jnp.where(low, 0.0, 1.0).astype(jnp.bfloat16)
    neg_inf = jnp.float32(-jnp.inf)

    kv = jnp.concatenate([kvp_ref[...], kvc_ref[...]], axis=0).astype(jnp.float32)

    def padded(group, head_is_high):
        rolled = pltpu.roll(group, HEAD_DIM, axis=1)
        if head_is_high:
            lo_half, hi_half = rolled, group
        else:
            lo_half, hi_half = group, rolled
        return (_bf16(jnp.where(low, lo_half, 0.0)), _bf16(jnp.where(low, 0.0, hi_half)))

    for h in range(N_KV_HEADS):
        g0 = (h // 2) * LANES
        k_lo, k_hi = padded(kv[:, g0:g0 + LANES], h % 2 == 1)
        v_lo, v_hi = padded(kv[:, KV_W + g0:KV_W + g0 + LANES], h % 2 == 1)
        r_even = jnp.concatenate([v_lo, ones_low], axis=1)
        r_odd = jnp.concatenate([v_hi, ones_high], axis=1)
        q4 = jnp.concatenate(
            [q_ref[:, (h * pairs + p) * LANES:(h * pairs + p + 1) * LANES] for p in range(pairs)],
            axis=0)
        sink_e = jnp.concatenate(
            [jnp.full((w, 1), sink_ref[h * GQA + 2 * p], jnp.float32) for p in range(pairs)], axis=0)
        sink_o = jnp.concatenate(
            [jnp.full((w, 1), sink_ref[h * GQA + 2 * p + 1], jnp.float32) for p in range(pairs)], axis=0)

        s_e = jnp.where(valid, _dot_nt(q4, k_lo), neg_inf)
        s_o = jnp.where(valid, _dot_nt(q4, k_hi), neg_inf)
        m_e = jnp.maximum(jnp.max(s_e, axis=1, keepdims=True), sink_e)
        m_o = jnp.maximum(jnp.max(s_o, axis=1, keepdims=True), sink_o)
        p_e = _bf16(jnp.exp(s_e - m_e))
        p_o = _bf16(jnp.exp(s_o - m_o))
        acc = _dot(p_e, r_even) + _dot(p_o, r_odd)
        sink_term = jnp.exp(jnp.where(low_r, sink_e - m_e, sink_o - m_o))
        out = acc[:, :LANES] / (acc[:, LANES:] + sink_term)
        for p in range(pairs):
            c0 = (h * pairs + p) * LANES
            attn_ref[:, c0:c0 + LANES] = _bf16(out[p * w:(p + 1) * w])

    ti = lax.broadcasted_iota(jnp.int32, (w, w), 0)
    si = lax.broadcasted_iota(jnp.int32, (w, w), 1)
    causal = si <= ti
    for g in range(SGU_GROUPS):
        sl = slice(g * SGU_CH, (g + 1) * SGU_CH)
        x = vg_ref[:, sl].astype(jnp.float32)
        mu = jnp.mean(x, axis=-1, keepdims=True)
        xc = x - mu
        var = jnp.mean(xc * xc, axis=-1, keepdims=True)
        vn = xc * lax.rsqrt(var + LN_EPS) * lng_ref[:, sl] + lnb_ref[:, sl]
        wsg = _bf16(jnp.where(causal, ws_ref[g], 0.0))
        sv = _dot(wsg, _bf16(vn)) + bs_ref[g]
        sgu_ref[:, sl] = _bf16(u_ref[:, sl].astype(jnp.float32) * sv)


def _mixers(h, hkv, sinks, ln_g, ln_b, w_s, b_s):
    w = WINDOW
    nb = N_TOK // w
    grid_spec = pltpu.PrefetchScalarGridSpec(
        num_scalar_prefetch=0,
        grid=(nb,),
        in_specs=[
            pl.BlockSpec(memory_space=pltpu.SMEM),
            pl.BlockSpec((w, ATTN_W), lambda n: (n, H_Q // ATTN_W)),
            pl.BlockSpec((w, 2 * KV_W), lambda n: (n, 0)),
            pl.BlockSpec((w, 2 * KV_W), lambda n: (jnp.maximum(n - 1, 0), 0)),
            pl.BlockSpec((w, SGU_W), lambda n: (n, H_U // SGU_W)),
            pl.BlockSpec((w, SGU_W), lambda n: (n, H_VG // SGU_W)),
            pl.BlockSpec((1, SGU_W), lambda n: (0, 0)),
            pl.BlockSpec((1, SGU_W), lambda n: (0, 0)),
            pl.BlockSpec((SGU_GROUPS, w, w), lambda n: (0, 0, 0)),
            pl.BlockSpec((SGU_GROUPS, w, 1), lambda n: (0, 0, 0)),
        ],
        out_specs=[
            pl.BlockSpec((w, ATTN_W), lambda n: (n, 0)),
            pl.BlockSpec((w, SGU_W), lambda n: (n, 0)),
        ],
    )
    return pl.pallas_call(
        _mixers_kernel,
        grid_spec=grid_spec,
        out_shape=[jax.ShapeDtypeStruct((N_TOK, ATTN_W), jnp.bfloat16),
                   jax.ShapeDtypeStruct((N_TOK, SGU_W), jnp.bfloat16)],
        compiler_params=pltpu.CompilerParams(
            dimension_semantics=("arbitrary",), vmem_limit_bytes=32 * 1024 * 1024),
        name="mixers",
    )(sinks, h, hkv, hkv, h, h, ln_g.reshape(1, SGU_W), ln_b.reshape(1, SGU_W), w_s,
      b_s.reshape(SGU_GROUPS, w, 1))


def _layer_norm(z, g, b):
    mu = jnp.mean(z, axis=-1, keepdims=True)
    zc = z - mu
    var = jnp.mean(zc * zc, axis=-1, keepdims=True)
    return zc * lax.rsqrt(var + LN_EPS) * g + b


def _merge_kernel(attn_ref, sgu_ref, ga_ref, gb_ref, x_ref, wa_ref, wb_ref, wo_ref, g_ref, b_ref,
                  wr_ref, x1_ref, lg_ref):
    mix = (ga_ref[...].astype(jnp.float32) * _dot(attn_ref[...], wa_ref[...])
           + gb_ref[...].astype(jnp.float32) * _dot(sgu_ref[...], wb_ref[...]))
    z = ALPHA * x_ref[...] + _dot(_bf16(mix), wo_ref[...])
    x1 = _layer_norm(z, g_ref[...], b_ref[...])
    x1_ref[...] = x1
    x_hi = _bf16(x1)
    x_lo = _bf16(x1 - x_hi.astype(jnp.float32))
    parts = _dot(x_hi, wr_ref[...]) + _dot(x_lo, wr_ref[...])
    lg_ref[...] = parts + pltpu.roll(parts, N_EXPERTS, axis=1)


def _router_parts(w_router):
    hi = _bf16(w_router)
    lo = _bf16(w_router - hi.astype(jnp.float32))
    return jnp.concatenate([hi, lo], axis=1)


def _merge(attn, sgu, h, x2d, wa, wb, wo, g, b, wr):
    tm = MERGE_TM
    resident = pl.Buffered(1)
    grid_spec = pltpu.PrefetchScalarGridSpec(
        num_scalar_prefetch=0,
        grid=(N_TOK // tm,),
        in_specs=[
            pl.BlockSpec((tm, ATTN_W), lambda i: (i, 0)),
            pl.BlockSpec((tm, SGU_W), lambda i: (i, 0)),
            pl.BlockSpec((tm, D_MODEL), lambda i: (i, H_GA // D_MODEL)),
            pl.BlockSpec((tm, D_MODEL), lambda i: (i, H_GB // D_MODEL)),
            pl.BlockSpec((tm, D_MODEL), lambda i: (i, 0)),
            pl.BlockSpec((ATTN_W, D_MODEL), lambda i: (0, 0), pipeline_mode=resident),
            pl.BlockSpec((SGU_W, D_MODEL), lambda i: (0, 0), pipeline_mode=resident),
            pl.BlockSpec((D_MODEL, D_MODEL), lambda i: (0, 0), pipeline_mode=resident),
            pl.BlockSpec((1, D_MODEL), lambda i: (0, 0)),
            pl.BlockSpec((1, D_MODEL), lambda i: (0, 0)),
            pl.BlockSpec((D_MODEL, 2 * N_EXPERTS), lambda i: (0, 0)),
        ],
        out_specs=[
            pl.BlockSpec((tm, D_MODEL), lambda i: (i, 0)),
            pl.BlockSpec((tm, 2 * N_EXPERTS), lambda i: (i, 0)),
        ],
    )
    x1, lg = pl.pallas_call(
        _merge_kernel,
        grid_spec=grid_spec,
        out_shape=[jax.ShapeDtypeStruct((N_TOK, D_MODEL), jnp.float32),
                   jax.ShapeDtypeStruct((N_TOK, 2 * N_EXPERTS), jnp.float32)],
        compiler_params=pltpu.CompilerParams(
            dimension_semantics=("arbitrary",), vmem_limit_bytes=VMEM_LIMIT_CAP),
        name="merge",
    )(attn, sgu, h, h, x2d, wa, wb, wo, g, b, wr)
    return x1, lg[:, :N_EXPERTS].T


def _first_argmax(v, rows):
    m = jnp.max(v, axis=0, keepdims=True)
    i = jnp.min(jnp.where(v == m, rows, float(v.shape[0])), axis=0, keepdims=True)
    return m, i


def _row_index(shape):
    return lax.broadcasted_iota(jnp.int32, shape, 0).astype(jnp.float32)


def _route_kernel(lg_ref, bias_ref, idx_ref, w_ref):
    tt = lg_ref.shape[1]
    neg_inf = jnp.float32(-jnp.inf)
    scores = jax.nn.sigmoid(lg_ref[...])
    biased = scores + bias_ref[...]
    row_g = _row_index((GROUP_SIZE, tt))
    gs = []
    for g in range(N_EXPERT_GROUPS):
        blk = biased[g * GROUP_SIZE:(g + 1) * GROUP_SIZE]
        m1, i1 = _first_argmax(blk, row_g)
        m2 = jnp.max(jnp.where(row_g == i1, neg_inf, blk), axis=0, keepdims=True)
        gs.append(m1 + m2)
    cur = jnp.concatenate(gs, axis=0)
    row_n = _row_index((N_EXPERT_GROUPS, tt))
    sel = jnp.zeros((N_EXPERT_GROUPS, tt), jnp.float32)
    for _ in range(TOPK_GROUPS):
        _, i = _first_argmax(cur, row_n)
        hit = row_n == i
        sel = jnp.where(hit, 1.0, sel)
        cur = jnp.where(hit, neg_inf, cur)
    emask = jnp.concatenate(
        [jnp.broadcast_to(sel[g:g + 1], (GROUP_SIZE, tt)) for g in range(N_EXPERT_GROUPS)], axis=0)
    masked = jnp.where(emask > 0.5, biased, neg_inf)
    row_e = _row_index((N_EXPERTS, tt))
    idx_rows, w_rows = [], []
    for _ in range(TOP_K):
        _, i = _first_argmax(masked, row_e)
        hit = row_e == i
        w_rows.append(jnp.sum(jnp.where(hit, scores, 0.0), axis=0, keepdims=True))
        idx_rows.append(i)
        masked = jnp.where(hit, neg_inf, masked)
    wsel = jnp.concatenate(w_rows, axis=0)
    idx_ref[...] = jnp.concatenate(idx_rows, axis=0).astype(jnp.int32)
    w_ref[...] = wsel / (jnp.sum(wsel, axis=0, keepdims=True) + 1e-20) * ROUTED_SCALE


def _route(logits_t, bias):
    tt = ROUTE_TT
    return pl.pallas_call(
        _route_kernel,
        grid=(N_TOK // tt,),
        in_specs=[pl.BlockSpec((N_EXPERTS, tt), lambda i: (0, i)),
                  pl.BlockSpec((N_EXPERTS, 1), lambda i: (0, 0))],
        out_specs=[pl.BlockSpec((TOP_K, tt), lambda i: (0, i)),
                   pl.BlockSpec((TOP_K, tt), lambda i: (0, i))],
        out_shape=[jax.ShapeDtypeStruct((TOP_K, N_TOK), jnp.int32),
                   jax.ShapeDtypeStruct((TOP_K, N_TOK), jnp.float32)],
        compiler_params=pltpu.CompilerParams(dimension_semantics=("arbitrary",)),
        name="route",
    )(logits_t, bias.reshape(N_EXPERTS, 1))


def _dispatch_plan(idx_t):
    bm = EXPERT_BM
    flat_e = idx_t.T.reshape(-1)
    counts = jnp.sum((flat_e[:, None] == jnp.arange(N_EXPERTS, dtype=jnp.int32)[None, :])
                     .astype(jnp.int32), axis=0)
    padded = (counts + bm - 1) // bm * bm
    pad_end = jnp.cumsum(padded)
    fill_end = jnp.cumsum(padded - counts)
    n_fill = N_SLOTS - N_ASSIGN
    fill_key = jnp.sum((jnp.arange(n_fill, dtype=jnp.int32)[:, None] >= fill_end[None, :])
                       .astype(jnp.int32), axis=1)
    keys = jnp.concatenate([flat_e, fill_key])
    src = jnp.argsort(keys, stable=True).astype(jnp.int32)
    real = src < N_ASSIGN
    slot = jnp.arange(N_SLOTS, dtype=jnp.int32)
    dst = jnp.where(real, (src % TOP_K) * N_TOK + src // TOP_K, N_ASSIGN + slot % (2 * bm))
    used = jnp.any(real.reshape(N_EXPERT_BLOCKS, bm), axis=1)
    block_start = jnp.arange(N_EXPERT_BLOCKS, dtype=jnp.int32) * bm
    block_e = jnp.sum((block_start[:, None] >= pad_end[None, :]).astype(jnp.int32), axis=1)
    n_used = jnp.sum(used.astype(jnp.int32))
    last_e = block_e[jnp.maximum(n_used - 1, 0)]
    block_e = jnp.where(used, jnp.minimum(block_e, N_EXPERTS - 1), last_e).astype(jnp.int32)
    prev_e = jnp.concatenate([jnp.full((1,), -1, jnp.int32), block_e[:-1]])
    fresh = (block_e != prev_e).astype(jnp.int32)
    return dst.reshape(N_EXPERT_BLOCKS, 1, bm), block_e, used.astype(jnp.int32), fresh


def _experts_kernel(be_ref, used_ref, fresh_ref, dst_ref, dstn_ref, x_hbm, w1_ref, w3_ref, w2_ref,
                    y_hbm, xbuf, obuf, w1b, w3b, w2b, gsem, ssem):
    b = pl.program_id(0)
    nb = pl.num_programs(0)
    bm = EXPERT_BM
    slot = b % 2

    def gather_rows(idx_ref, s):
        for r in range(bm):
            tok = idx_ref[0, 0, r] & (N_TOK - 1)
            pltpu.make_async_copy(x_hbm.at[pl.ds(tok, 1)], xbuf.at[s, pl.ds(r, 1)],
                                  gsem.at[s]).start(priority=r % 2)

    def wait_gather(s):
        pltpu.make_async_copy(x_hbm.at[pl.ds(0, bm)], xbuf.at[s], gsem.at[s]).wait()

    def wait_scatter(s):
        pltpu.make_async_copy(obuf.at[s], y_hbm.at[pl.ds(0, bm)], ssem.at[s]).wait()

    @pl.when(b == 0)
    def _():
        obuf[1] = jnp.zeros((bm, D_MODEL), jnp.float32)
        for region in range(2):
            fill = pltpu.make_async_copy(
                obuf.at[1], y_hbm.at[pl.ds(N_ASSIGN + region * bm, bm)], ssem.at[1])
            fill.start()
            fill.wait()

    @pl.when((b == 0) & (used_ref[0] == 1))
    def _():
        gather_rows(dst_ref, 0)

    @pl.when((b + 1 < nb) & (used_ref[jnp.minimum(b + 1, nb - 1)] == 1))
    def _():
        gather_rows(dstn_ref, 1 - slot)

    @pl.when((b >= 2) & (used_ref[jnp.maximum(b - 2, 0)] == 1))
    def _():
        wait_scatter(slot)

    @pl.when(used_ref[b] == 1)
    def _():
        @pl.when(fresh_ref[b] == 1)
        def _():
            w1b[...] = _bf16(w1_ref[0])
            w3b[...] = _bf16(w3_ref[0])
            w2b[...] = _bf16(w2_ref[0])

        wait_gather(slot)
        xb = _bf16(xbuf[slot])
        a = _bf16(jax.nn.silu(_dot(xb, w1b[...])) * _dot(xb, w3b[...]))
        obuf[slot] = _dot(a, w2b[...])
        for r in range(bm):
            row = dst_ref[0, 0, r]
            pltpu.make_async_copy(obuf.at[slot, pl.ds(r, 1)], y_hbm.at[pl.ds(row, 1)],
                                  ssem.at[slot]).start(priority=r % 2)

    @pl.when(b == nb - 1)
    def _():
        @pl.when((b >= 1) & (used_ref[jnp.maximum(b - 1, 0)] == 1))
        def _():
            wait_scatter(1 - slot)

        @pl.when(used_ref[b] == 1)
        def _():
            wait_scatter(slot)


def _experts(x1, w1, w3, w2, dst, block_e, used, fresh):
    bm = EXPERT_BM
    nb = N_EXPERT_BLOCKS
    smem_blk = lambda f: pl.BlockSpec((1, 1, bm), f, memory_space=pltpu.SMEM)
    grid_spec = pltpu.PrefetchScalarGridSpec(
        num_scalar_prefetch=3,
        grid=(nb,),
        in_specs=[
            smem_blk(lambda b, be, us, fr: (b, 0, 0)),
            smem_blk(lambda b, be, us, fr: (jnp.minimum(b + 1, nb - 1), 0, 0)),
            pl.BlockSpec(memory_space=pl.ANY),
            pl.BlockSpec((1, D_MODEL, D_EXPERT), lambda b, be, us, fr: (be[b], 0, 0)),
            pl.BlockSpec((1, D_MODEL, D_EXPERT), lambda b, be, us, fr: (be[b], 0, 0)),
            pl.BlockSpec((1, D_EXPERT, D_MODEL), lambda b, be, us, fr: (be[b], 0, 0)),
        ],
        out_specs=pl.BlockSpec(memory_space=pl.ANY),
        scratch_shapes=[
            pltpu.VMEM((2, bm, D_MODEL), jnp.float32),
            pltpu.VMEM((2, bm, D_MODEL), jnp.float32),
            pltpu.VMEM((D_MODEL, D_EXPERT), jnp.bfloat16),
            pltpu.VMEM((D_MODEL, D_EXPERT), jnp.bfloat16),
            pltpu.VMEM((D_EXPERT, D_MODEL), jnp.bfloat16),
            pltpu.SemaphoreType.DMA((2,)),
            pltpu.SemaphoreType.DMA((2,)),
        ],
    )
    return pl.pallas_call(
        _experts_kernel,
        grid_spec=grid_spec,
        out_shape=jax.ShapeDtypeStruct((N_ASSIGN + 2 * bm, D_MODEL), jnp.float32),
        compiler_params=pltpu.CompilerParams(
            dimension_semantics=("arbitrary",), vmem_limit_bytes=52 * 1024 * 1024),
        name="experts",
    )(block_e, used, fresh, dst, dst, x1, w1, w3, w2)


def _combine_kernel(*refs):
    y_refs = refs[:TOP_K]
    w_ref, x_ref, s1_ref, s3_ref, s2_ref, g_ref, b_ref, o_ref = refs[TOP_K:]
    x1 = x_ref[...]
    xb = _bf16(x1)
    a = _bf16(jax.nn.silu(_dot(xb, s1_ref[...])) * _dot(xb, s3_ref[...]))
    ffn = _dot(a, s2_ref[...])
    wts = w_ref[...]
    routed = wts[:, 0:1] * y_refs[0][...]
    for k in range(1, TOP_K):
        routed = routed + wts[:, k:k + 1] * y_refs[k][...]
    o_ref[...] = _layer_norm(ALPHA * x1 + (routed + ffn), g_ref[...], b_ref[...])


def _combine(y, wts, x1, s1, s3, s2, g, b):
    tm = COMBINE_TM
    nt = N_TOK // tm
    y_specs = [pl.BlockSpec((tm, D_MODEL), functools.partial(lambda i, k: (k * nt + i, 0), k=k))
               for k in range(TOP_K)]
    return pl.pallas_call(
        _combine_kernel,
        grid=(nt,),
        in_specs=y_specs + [
            pl.BlockSpec((tm, TOP_K), lambda i: (i, 0)),
            pl.BlockSpec((tm, D_MODEL), lambda i: (i, 0)),
            pl.BlockSpec((D_MODEL, D_SHARED), lambda i: (0, 0)),
            pl.BlockSpec((D_MODEL, D_SHARED), lambda i: (0, 0)),
            pl.BlockSpec((D_SHARED, D_MODEL), lambda i: (0, 0)),
            pl.BlockSpec((1, D_MODEL), lambda i: (0, 0)),
            pl.BlockSpec((1, D_MODEL), lambda i: (0, 0)),
        ],
        out_specs=pl.BlockSpec((tm, D_MODEL), lambda i: (i, 0)),
        out_shape=jax.ShapeDtypeStruct((N_TOK, D_MODEL), jnp.float32),
        compiler_params=pltpu.CompilerParams(
            dimension_semantics=("arbitrary",), vmem_limit_bytes=VMEM_LIMIT_CAP),
        name="combine",
    )(*([y] * TOP_K), wts, x1, s1, s3, s2, g, b)


def kernel(x, w_in, b_in, sinks, sgu_ln_g, sgu_ln_b, w_spatial, b_spatial, w_branch_attn,
           w_branch_sgu, w_out, ln1_g, ln1_b, w_router, router_bias, w1, w3, w2, ws1, ws3, ws2,
           ln2_g, ln2_b):
    assert x.shape == (BATCH, SEQ, D_MODEL) and w_in.shape == (1, D_MODEL, IN_W)
    x2d = x.reshape(N_TOK, D_MODEL)
    h, hkv = _in_proj(_bf16(x2d), w_in[0], b_in)
    attn, sgu = _mixers(h, hkv, sinks[0], sgu_ln_g[0], sgu_ln_b[0], w_spatial[0], b_spatial[0])
    x1, logits_t = _merge(attn, sgu, h, x2d, _bf16(w_branch_attn[0]), _bf16(w_branch_sgu[0]),
                          _bf16(w_out[0]), ln1_g, ln1_b, _router_parts(w_router[0]))
    idx_t, w_t = _route(logits_t, router_bias[0])
    dst, block_e, used, fresh = _dispatch_plan(idx_t)
    y = _experts(x1, w1[0], w3[0], w2[0], dst, block_e, used, fresh)
    out = _combine(y, w_t.T, x1, _bf16(ws1[0]), _bf16(ws3[0]), _bf16(ws2[0]), ln2_g, ln2_b)
    return out.reshape(BATCH, SEQ, D_MODEL)
```

```python
import functools
import math

import numpy as np
import jax
import jax.numpy as jnp
from jax import lax
from jax.experimental import pallas as pl
from jax.experimental.pallas import tpu as pltpu

D_MODEL = 2048
BATCH = 2
SEQ = 4096
N_TOK = BATCH * SEQ
N_Q_HEADS = 32
N_KV_HEADS = 4
HEAD_DIM = 64
GQA = N_Q_HEADS // N_KV_HEADS
WINDOW = 128
ROPE_THETA = 500000.0
ROT_DIM = HEAD_DIM // 4
SGU_GROUPS = 8
SGU_CH = 128
N_EXPERTS = 64
N_EXPERT_GROUPS = 8
GROUP_SIZE = N_EXPERTS // N_EXPERT_GROUPS
TOPK_GROUPS = 4
TOP_K = 8
D_EXPERT = 512
D_SHARED = 512
ROUTED_SCALE = 2.5
ATTN_W = N_Q_HEADS * HEAD_DIM
KV_W = N_KV_HEADS * HEAD_DIM
SGU_W = SGU_GROUPS * SGU_CH
IN_W = ATTN_W + 2 * KV_W + 2 * SGU_W + 2 * D_MODEL
ALPHA = 2.0 ** 0.25
LN_EPS = 1e-5
N_ASSIGN = N_TOK * TOP_K

LANES = 128
VMEM_LIMIT_CAP = 56 * 1024 * 1024

PROJ_TM = 512
PROJ_TN = 1024
MERGE_TM = 256
ROUTE_TT = 512
EXPERT_BM = 256
EXPERT_GROUP = 32
COMBINE_TM = 128
N_EXPERT_BLOCKS = -(-(N_ASSIGN + N_EXPERTS * (EXPERT_BM - 1)) // EXPERT_BM)
N_SLOTS = N_EXPERT_BLOCKS * EXPERT_BM

H_Q, H_GA, H_GB, H_U, H_VG = 0, 2048, 4096, 6144, 7168
H_W = 8192
PROJ_UNIT = 512
_SRC_UNIT = np.array([0, 1, 2, 3, 9, 10, 11, 12, 13, 14, 15, 16, 5, 6, 7, 8], np.int32)
_SRC_UNIT_KV = ATTN_W // PROJ_UNIT
_N_PROJ_TILES = H_W // PROJ_TN
_Q_TILES = ATTN_W // PROJ_TN
_GATE_END = H_U // PROJ_TN


def _bf16(a):
    return a.astype(jnp.bfloat16)


def _dot(a, b):
    return jnp.dot(a, b, preferred_element_type=jnp.float32)


def _dot_nt(a, b):
    return lax.dot_general(a, b, (((1,), (1,)), ((), ())), preferred_element_type=jnp.float32)


def _rope_slab(x, c, s_next, s_prev):
    return (x * c + pltpu.roll(x, LANES - ROT_DIM // 2, axis=1) * s_next
            + pltpu.roll(x, ROT_DIM // 2, axis=1) * s_prev)


def _in_proj_kernel(src_ref, x_ref, wa_ref, wb_ref, ba_ref, bb_ref, c_ref, sn_ref, sp_ref, o_ref,
                    wbf_ref):
    j = pl.program_id(0)
    i = pl.program_id(1)

    @pl.when(i == 0)
    def _():
        wbf_ref[:, :PROJ_UNIT] = _bf16(wa_ref[...])
        wbf_ref[:, PROJ_UNIT:] = _bf16(wb_ref[...])

    def project():
        bias = jnp.concatenate([ba_ref[...], bb_ref[...]], axis=1)
        return _dot(x_ref[...], wbf_ref[...]) + bias

    @pl.when(j < _Q_TILES)
    def _():
        acc = project()
        c, sn, sp = c_ref[...], sn_ref[...], sp_ref[...]
        scale = HEAD_DIM ** -0.5
        for t in range(PROJ_TN // LANES):
            sl = slice(t * LANES, (t + 1) * LANES)
            o_ref[:, sl] = _bf16(_rope_slab(acc[:, sl], c, sn, sp) * scale)

    @pl.when((j >= _Q_TILES) & (j < _GATE_END))
    def _():
        o_ref[...] = _bf16(jax.nn.sigmoid(project()))

    @pl.when(j >= _GATE_END)
    def _():
        o_ref[...] = _bf16(jax.nn.gelu(project()))


def _kv_proj_kernel(x_ref, w_ref, b_ref, c_ref, sn_ref, sp_ref, o_ref, wbf_ref):
    @pl.when(pl.program_id(0) == 0)
    def _():
        wbf_ref[...] = _bf16(w_ref[...])

    acc = _dot(x_ref[...], wbf_ref[...]) + b_ref[...]
    c, sn, sp = c_ref[...], sn_ref[...], sp_ref[...]
    for t in range(2 * KV_W // LANES):
        sl = slice(t * LANES, (t + 1) * LANES)
        if t < KV_W // LANES:
            o_ref[:, sl] = _bf16(_rope_slab(acc[:, sl], c, sn, sp))
        else:
            o_ref[:, sl] = _bf16(acc[:, sl])


def _rope_tables():
    half = ROT_DIM // 2
    inv_freq = ROPE_THETA ** (-np.arange(0, ROT_DIM, 2, dtype=np.float32) / ROT_DIM)
    pos = np.arange(SEQ, dtype=np.float32)
    ang = jnp.asarray(pos[:, None] * inv_freq[None, :].astype(np.float32), jnp.float32)
    cos, sin = jnp.cos(ang), jnp.sin(ang)
    ones = jnp.ones((SEQ, HEAD_DIM - ROT_DIM), jnp.float32)
    zeros = jnp.zeros((SEQ, HEAD_DIM - ROT_DIM), jnp.float32)
    zh = jnp.zeros((SEQ, half), jnp.float32)
    c = jnp.concatenate([cos, cos, ones], axis=1)
    s_next = jnp.concatenate([-sin, zh, zeros], axis=1)
    s_prev = jnp.concatenate([zh, sin, zeros], axis=1)
    rep = LANES // HEAD_DIM
    return tuple(jnp.tile(t, (1, rep)) for t in (c, s_next, s_prev))


def _in_proj(x_bf, w_in, b_in):
    c, sn, sp = _rope_tables()
    n_i = N_TOK // PROJ_TM
    pos_tiles = SEQ // PROJ_TM
    tbl = pl.BlockSpec((PROJ_TM, LANES), lambda j, i, src: (i % pos_tiles, 0))
    grid_spec = pltpu.PrefetchScalarGridSpec(
        num_scalar_prefetch=1,
        grid=(_N_PROJ_TILES, n_i),
        in_specs=[
            pl.BlockSpec((PROJ_TM, D_MODEL), lambda j, i, src: (i, 0)),
            pl.BlockSpec((D_MODEL, PROJ_UNIT), lambda j, i, src: (0, src[2 * j])),
            pl.BlockSpec((D_MODEL, PROJ_UNIT), lambda j, i, src: (0, src[2 * j + 1])),
            pl.BlockSpec((1, PROJ_UNIT), lambda j, i, src: (0, src[2 * j])),
            pl.BlockSpec((1, PROJ_UNIT), lambda j, i, src: (0, src[2 * j + 1])),
            tbl, tbl, tbl,
        ],
        out_specs=pl.BlockSpec((PROJ_TM, PROJ_TN), lambda j, i, src: (i, j)),
        scratch_shapes=[pltpu.VMEM((D_MODEL, PROJ_TN), jnp.bfloat16)],
    )
    h = pl.pallas_call(
        _in_proj_kernel,
        grid_spec=grid_spec,
        out_shape=jax.ShapeDtypeStruct((N_TOK, H_W), jnp.bfloat16),
        compiler_params=pltpu.CompilerParams(
            dimension_semantics=("arbitrary", "arbitrary"),
            vmem_limit_bytes=48 * 1024 * 1024),
        name="in_proj",
    )(jnp.asarray(_SRC_UNIT), x_bf, w_in, w_in, b_in, b_in, c, sn, sp)

    tbl1 = pl.BlockSpec((PROJ_TM, LANES), lambda i: (i % pos_tiles, 0))
    hkv = pl.pallas_call(
        _kv_proj_kernel,
        grid=(n_i,),
        in_specs=[
            pl.BlockSpec((PROJ_TM, D_MODEL), lambda i: (i, 0)),
            pl.BlockSpec((D_MODEL, 2 * KV_W), lambda i: (0, _SRC_UNIT_KV)),
            pl.BlockSpec((1, 2 * KV_W), lambda i: (0, _SRC_UNIT_KV)),
            tbl1, tbl1, tbl1,
        ],
        out_specs=pl.BlockSpec((PROJ_TM, 2 * KV_W), lambda i: (i, 0)),
        out_shape=jax.ShapeDtypeStruct((N_TOK, 2 * KV_W), jnp.bfloat16),
        scratch_shapes=[pltpu.VMEM((D_MODEL, 2 * KV_W), jnp.bfloat16)],
        compiler_params=pltpu.CompilerParams(
            dimension_semantics=("arbitrary",), vmem_limit_bytes=32 * 1024 * 1024),
        name="kv_proj",
    )(x_bf, w_in, b_in, c, sn, sp)
    return h, hkv


def _mixers_kernel(sink_ref, q_ref, kvc_ref, kvp_ref, u_ref, vg_ref, lng_ref, lnb_ref,
                   ws_ref, bs_ref, attn_ref, sgu_ref):
    n = pl.program_id(0)
    w = WINDOW
    first_key = jnp.where((n % (SEQ // w)) == 0, w, 0)
    pairs = GQA // 2
    rows = pairs * w

    qi = lax.broadcasted_iota(jnp.int32, (w, 2 * w), 0)
    kj = lax.broadcasted_iota(jnp.int32, (w, 2 * w), 1)
    valid = (kj > qi) & (kj <= qi + w) & (kj >= first_key)
    valid = jnp.concatenate([valid] * pairs, axis=0)
    lane = lax.broadcasted_iota(jnp.int32, (2 * w, LANES), 1)
    low = lane < HEAD_DIM
    lane_r = lax.broadcasted_iota(jnp.int32, (rows, LANES), 1)
    low_r = lane_r < HEAD_DIM
    ones_low = jnp.where(low, 1.0, 0.0).astype(jnp.bfloat16)
    ones_high = jnp.where(low, 0.0, 1.0).astype(jnp.bfloat16)
    neg_inf = jnp.float32(-jnp.inf)

    kv = jnp.concatenate([kvp_ref[...], kvc_ref[...]], axis=0).astype(jnp.float32)

    def padded(group, head_is_high):
        rolled = pltpu.roll(group, HEAD_DIM, axis=1)
        if head_is_high:
            lo_half, hi_half = rolled, group
        else:
            lo_half, hi_half = group, rolled
        return (_bf16(jnp.where(low, lo_half, 0.0)), _bf16(jnp.where(low, 0.0, hi_half)))

    for h in range(N_KV_HEADS):
        g0 = (h // 2) * LANES
        k_lo, k_hi = padded(kv[:, g0:g0 + LANES], h % 2 == 1)
        v_lo, v_hi = padded(kv[:, KV_W + g0:KV_W + g0 + LANES], h % 2 == 1)
        r_even = jnp.concatenate([v_lo, ones_low], axis=1)
        r_odd = jnp.concatenate([v_hi, ones_high], axis=1)
        q4 = jnp.concatenate(
            [q_ref[:, (h * pairs + p) * LANES:(h * pairs + p + 1) * LANES] for p in range(pairs)],
            axis=0)
        sink_e = jnp.concatenate(
            [jnp.full((w, 1), sink_ref[h * GQA + 2 * p], jnp.float32) for p in range(pairs)], axis=0)
        sink_o = jnp.concatenate(
            [jnp.full((w, 1), sink_ref[h * GQA + 2 * p + 1], jnp.float32) for p in range(pairs)], axis=0)

        s_e = jnp.where(valid, _dot_nt(q4, k_lo), neg_inf)
        s_o = jnp.where(valid, _dot_nt(q4, k_hi), neg_inf)
        m_e = jnp.maximum(jnp.max(s_e, axis=1, keepdims=True), sink_e)
        m_o = jnp.maximum(jnp.max(s_o, axis=1, keepdims=True), sink_o)
        p_e = _bf16(jnp.exp(s_e - m_e))
        p_o = _bf16(jnp.exp(s_o - m_o))
        acc = _dot(p_e, r_even) + _dot(p_o, r_odd)
        sink_term = jnp.exp(jnp.where(low_r, sink_e - m_e, sink_o - m_o))
        out = acc[:, :LANES] / (acc[:, LANES:] + sink_term)
        for p in range(pairs):
            c0 = (h * pairs + p) * LANES
            attn_ref[:, c0:c0 + LANES] = _bf16(out[p * w:(p + 1) * w])

    ti = lax.broadcasted_iota(jnp.int32, (w, w), 0)
    si = lax.broadcasted_iota(jnp.int32, (w, w), 1)
    causal = si <= ti
    for g in range(SGU_GROUPS):
        sl = slice(g * SGU_CH, (g + 1) * SGU_CH)
        x = vg_ref[:, sl].astype(jnp.float32)
        mu = jnp.mean(x, axis=-1, keepdims=True)
        xc = x - mu
        var = jnp.mean(xc * xc, axis=-1, keepdims=True)
        vn = xc * lax.rsqrt(var + LN_EPS) * lng_ref[:, sl] + lnb_ref[:, sl]
        wsg = _bf16(jnp.where(causal, ws_ref[g], 0.0))
        sv = _dot(wsg, _bf16(vn)) + bs_ref[g]
        sgu_ref[:, sl] = _bf16(u_ref[:, sl].astype(jnp.float32) * sv)


def _mixers(h, hkv, sinks, ln_g, ln_b, w_s, b_s):
    w = WINDOW
    nb = N_TOK // w
    grid_spec = pltpu.PrefetchScalarGridSpec(
        num_scalar_prefetch=0,
        grid=(nb,),
        in_specs=[
            pl.BlockSpec(memory_space=pltpu.SMEM),
            pl.BlockSpec((w, ATTN_W), lambda n: (n, H_Q // ATTN_W)),
            pl.BlockSpec((w, 2 * KV_W), lambda n: (n, 0)),
            pl.BlockSpec((w, 2 * KV_W), lambda n: (jnp.maximum(n - 1, 0), 0)),
            pl.BlockSpec((w, SGU_W), lambda n: (n, H_U // SGU_W)),
            pl.BlockSpec((w, SGU_W), lambda n: (n, H_VG // SGU_W)),
            pl.BlockSpec((1, SGU_W), lambda n: (0, 0)),
            pl.BlockSpec((1, SGU_W), lambda n: (0, 0)),
            pl.BlockSpec((SGU_GROUPS, w, w), lambda n: (0, 0, 0)),
            pl.BlockSpec((SGU_GROUPS, w, 1), lambda n: (0, 0, 0)),
        ],
        out_specs=[
            pl.BlockSpec((w, ATTN_W), lambda n: (n, 0)),
            pl.BlockSpec((w, SGU_W), lambda n: (n, 0)),
        ],
    )
    return pl.pallas_call(
        _mixers_kernel,
        grid_spec=grid_spec,
        out_shape=[jax.ShapeDtypeStruct((N_TOK, ATTN_W), jnp.bfloat16),
                   jax.ShapeDtypeStruct((N_TOK, SGU_W), jnp.bfloat16)],
        compiler_params=pltpu.CompilerParams(
            dimension_semantics=("arbitrary",), vmem_limit_bytes=32 * 1024 * 1024),
        name="mixers",
    )(sinks, h, hkv, hkv, h, h, ln_g.reshape(1, SGU_W), ln_b.reshape(1, SGU_W), w_s,
      b_s.reshape(SGU_GROUPS, w, 1))


def _layer_norm(z, g, b):
    mu = jnp.mean(z, axis=-1, keepdims=True)
    zc = z - mu
    var = jnp.mean(zc * zc, axis=-1, keepdims=True)
    return zc * lax.rsqrt(var + LN_EPS) * g + b


def _merge_kernel(attn_ref, sgu_ref, ga_ref, gb_ref, x_ref, wa_ref, wb_ref, wo_ref, g_ref, b_ref,
                  wr_ref, x1_ref, lg_ref):
    mix = (ga_ref[...].astype(jnp.float32) * _dot(attn_ref[...], wa_ref[...])
           + gb_ref[...].astype(jnp.float32) * _dot(sgu_ref[...], wb_ref[...]))
    z = ALPHA * x_ref[...] + _dot(_bf16(mix), wo_ref[...])
    x1 = _layer_norm(z, g_ref[...], b_ref[...])
    x1_ref[...] = x1
    x_hi = _bf16(x1)
    x_lo = _bf16(x1 - x_hi.astype(jnp.float32))
    parts = _dot(x_hi, wr_ref[...]) + _dot(x_lo, wr_ref[...])
    lg_ref[...] = parts + pltpu.roll(parts, N_EXPERTS, axis=1)


def _router_parts(w_router):
    hi = _bf16(w_router)
    lo = _bf16(w_router - hi.astype(jnp.float32))
    return jnp.concatenate([hi, lo], axis=1)


def _merge(attn, sgu, h, x2d, wa, wb, wo, g, b, wr):
    tm = MERGE_TM
    resident = pl.Buffered(1)
    grid_spec = pltpu.PrefetchScalarGridSpec(
        num_scalar_prefetch=0,
        grid=(N_TOK // tm,),
        in_specs=[
            pl.BlockSpec((tm, ATTN_W), lambda i: (i, 0)),
            pl.BlockSpec((tm, SGU_W), lambda i: (i, 0)),
            pl.BlockSpec((tm, D_MODEL), lambda i: (i, H_GA // D_MODEL)),
            pl.BlockSpec((tm, D_MODEL), lambda i: (i, H_GB // D_MODEL)),
            pl.BlockSpec((tm, D_MODEL), lambda i: (i, 0)),
            pl.BlockSpec((ATTN_W, D_MODEL), lambda i: (0, 0), pipeline_mode=resident),
            pl.BlockSpec((SGU_W, D_MODEL), lambda i: (0, 0), pipeline_mode=resident),
            pl.BlockSpec((D_MODEL, D_MODEL), lambda i: (0, 0), pipeline_mode=resident),
            pl.BlockSpec((1, D_MODEL), lambda i: (0, 0)),
            pl.BlockSpec((1, D_MODEL), lambda i: (0, 0)),
            pl.BlockSpec((D_MODEL, 2 * N_EXPERTS), lambda i: (0, 0)),
        ],
        out_specs=[
            pl.BlockSpec((tm, D_MODEL), lambda i: (i, 0)),
            pl.BlockSpec((tm, 2 * N_EXPERTS), lambda i: (i, 0)),
        ],
    )
    x1, lg = pl.pallas_call(
        _merge_kernel,
        grid_spec=grid_spec,
        out_shape=[jax.ShapeDtypeStruct((N_TOK, D_MODEL), jnp.float32),
                   jax.ShapeDtypeStruct((N_TOK, 2 * N_EXPERTS), jnp.float32)],
        compiler_params=pltpu.CompilerParams(
            dimension_semantics=("arbitrary",), vmem_limit_bytes=VMEM_LIMIT_CAP),
        name="merge",
    )(attn, sgu, h, h, x2d, wa, wb, wo, g, b, wr)
    return x1, lg[:, :N_EXPERTS].T


def _first_argmax(v, rows):
    m = jnp.max(v, axis=0, keepdims=True)
    i = jnp.min(jnp.where(v == m, rows, float(v.shape[0])), axis=0, keepdims=True)
    return m, i


def _row_index(shape):
    return lax.broadcasted_iota(jnp.int32, shape, 0).astype(jnp.float32)


def _route_kernel(lg_ref, bias_ref, idx_ref, w_ref):
    tt = lg_ref.shape[1]
    neg_inf = jnp.float32(-jnp.inf)
    scores = jax.nn.sigmoid(lg_ref[...])
    biased = scores + bias_ref[...]
    row_g = _row_index((GROUP_SIZE, tt))
    gs = []
    for g in range(N_EXPERT_GROUPS):
        blk = biased[g * GROUP_SIZE:(g + 1) * GROUP_SIZE]
        m1, i1 = _first_argmax(blk, row_g)
        m2 = jnp.max(jnp.where(row_g == i1, neg_inf, blk), axis=0, keepdims=True)
        gs.append(m1 + m2)
    cur = jnp.concatenate(gs, axis=0)
    row_n = _row_index((N_EXPERT_GROUPS, tt))
    sel = jnp.zeros((N_EXPERT_GROUPS, tt), jnp.float32)
    for _ in range(TOPK_GROUPS):
        _, i = _first_argmax(cur, row_n)
        hit = row_n == i
        sel = jnp.where(hit, 1.0, sel)
        cur = jnp.where(hit, neg_inf, cur)
    emask = jnp.concatenate(
        [jnp.broadcast_to(sel[g:g + 1], (GROUP_SIZE, tt)) for g in range(N_EXPERT_GROUPS)], axis=0)
    masked = jnp.where(emask > 0.5, biased, neg_inf)
    row_e = _row_index((N_EXPERTS, tt))
    idx_rows, w_rows = [], []
    for _ in range(TOP_K):
        _, i = _first_argmax(masked, row_e)
        hit = row_e == i
        w_rows.append(jnp.sum(jnp.where(hit, scores, 0.0), axis=0, keepdims=True))
        idx_rows.append(i)
        masked = jnp.where(hit, neg_inf, masked)
    wsel = jnp.concatenate(w_rows, axis=0)
    idx_ref[...] = jnp.concatenate(idx_rows, axis=0).astype(jnp.int32)
    w_ref[...] = wsel / (jnp.sum(wsel, axis=0, keepdims=True) + 1e-20) * ROUTED_SCALE


def _route(logits_t, bias):
    tt = ROUTE_TT
    return pl.pallas_call(
        _route_kernel,
        grid=(N_TOK // tt,),
        in_specs=[pl.BlockSpec((N_EXPERTS, tt), lambda i: (0, i)),
                  pl.BlockSpec((N_EXPERTS, 1), lambda i: (0, 0))],
        out_specs=[pl.BlockSpec((TOP_K, tt), lambda i: (0, i)),
                   pl.BlockSpec((TOP_K, tt), lambda i: (0, i))],
        out_shape=[jax.ShapeDtypeStruct((TOP_K, N_TOK), jnp.int32),
                   jax.ShapeDtypeStruct((TOP_K, N_TOK), jnp.float32)],
        compiler_params=pltpu.CompilerParams(dimension_semantics=("arbitrary",)),
        name="route",
    )(logits_t, bias.reshape(N_EXPERTS, 1))


def _dispatch_plan(idx_t):
    bm = EXPERT_BM
    flat_e = idx_t.T.reshape(-1)
    counts = jnp.sum((flat_e[:, None] == jnp.arange(N_EXPERTS, dtype=jnp.int32)[None, :])
                     .astype(jnp.int32), axis=0)
    padded = (counts + bm - 1) // bm * bm
    pad_end = jnp.cumsum(padded)
    fill_end = jnp.cumsum(padded - counts)
    n_fill = N_SLOTS - N_ASSIGN
    fill_key = jnp.sum((jnp.arange(n_fill, dtype=jnp.int32)[:, None] >= fill_end[None, :])
                       .astype(jnp.int32), axis=1)
    keys = jnp.concatenate([flat_e, fill_key])
    src = jnp.argsort(keys, stable=True).astype(jnp.int32)
    real = src < N_ASSIGN
    slot = jnp.arange(N_SLOTS, dtype=jnp.int32)
    dst = jnp.where(real, (src % TOP_K) * N_TOK + src // TOP_K, N_ASSIGN + slot % (2 * bm))
    n_valid = jnp.sum(real.reshape(N_EXPERT_BLOCKS, bm).astype(jnp.int32), axis=1)
    used = n_valid > 0
    block_start = jnp.arange(N_EXPERT_BLOCKS, dtype=jnp.int32) * bm
    block_e = jnp.sum((block_start[:, None] >= pad_end[None, :]).astype(jnp.int32), axis=1)
    n_used = jnp.sum(used.astype(jnp.int32))
    last_e = block_e[jnp.maximum(n_used - 1, 0)]
    block_e = jnp.where(used, jnp.minimum(block_e, N_EXPERTS - 1), last_e).astype(jnp.int32)
    prev_e = jnp.concatenate([jnp.full((1,), -1, jnp.int32), block_e[:-1]])
    fresh = (block_e != prev_e).astype(jnp.int32)
    e_ids = jnp.arange(N_EXPERTS, dtype=jnp.int32)
    later = (counts > 0)[None, :] & (e_ids[None, :] > e_ids[:, None])
    next_of = jnp.min(jnp.where(later, e_ids[None, :], N_EXPERTS), axis=1).astype(jnp.int32)
    return dst.reshape(N_EXPERT_BLOCKS, 1, bm), block_e, n_valid, fresh, next_of[block_e]


def _experts_kernel(be_ref, nval_ref, fresh_ref, nxt_ref, dst_ref, dstn_ref, x_hbm, w1_hbm, w3_hbm,
                    w2_hbm, y_hbm, xbuf, obuf, w1s, w3s, w2s, w1b, w3b, w2b, gsem, ssem, wsem):
    b = pl.program_id(0)
    nb = pl.num_programs(0)
    bm, grp = EXPERT_BM, EXPERT_GROUP
    slot = b % 2
    n_cur = nval_ref[b]
    n_next = jnp.where(b + 1 < nb, nval_ref[jnp.minimum(b + 1, nb - 1)], 0)
    n_prev = jnp.where(b >= 1, nval_ref[jnp.maximum(b - 1, 0)], 0)
    n_prev2 = jnp.where(b >= 2, nval_ref[jnp.maximum(b - 2, 0)], 0)

    def for_groups(n_rows, fn):
        for g in range(bm // grp):
            pl.when(g * grp < n_rows)(functools.partial(fn, g))

    def gather_copy(idx_ref, s, r):
        tok = idx_ref[0, 0, r] & (N_TOK - 1)
        return pltpu.make_async_copy(x_hbm.at[pl.ds(tok, 1)], xbuf.at[s, pl.ds(r, 1)], gsem.at[s])

    def scatter_copy(s, r):
        row = dst_ref[0, 0, r]
        return pltpu.make_async_copy(obuf.at[s, pl.ds(r, 1)], y_hbm.at[pl.ds(row, 1)], ssem.at[s])

    def start_gather(idx_ref, s, n_rows):
        def group(g):
            for r in range(g * grp, (g + 1) * grp):
                gather_copy(idx_ref, s, r).start()
        for_groups(n_rows, group)

    def wait_gather(s, n_rows):
        def group(g):
            rows = pl.ds(g * grp, grp)
            pltpu.make_async_copy(x_hbm.at[rows], xbuf.at[s, rows], gsem.at[s]).wait()
        for_groups(n_rows, group)

    def start_scatter(s, n_rows):
        def group(g):
            for r in range(g * grp, (g + 1) * grp):
                scatter_copy(s, r).start()
        for_groups(n_rows, group)

    def wait_scatter(s, n_rows):
        def group(g):
            rows = pl.ds(g * grp, grp)
            pltpu.make_async_copy(obuf.at[s, rows], y_hbm.at[rows], ssem.at[s]).wait()
        for_groups(n_rows, group)

    def weight_copies(e):
        return (pltpu.make_async_copy(w1_hbm.at[e], w1s, wsem.at[0]),
                pltpu.make_async_copy(w3_hbm.at[e], w3s, wsem.at[1]),
                pltpu.make_async_copy(w2_hbm.at[e], w2s, wsem.at[2]))

    @pl.when(b == 0)
    def _():
        for c in weight_copies(be_ref[0]):
            c.start(priority=1)
        xbuf[...] = jnp.zeros(xbuf.shape, xbuf.dtype)
        obuf[1] = jnp.zeros(obuf.shape[1:], obuf.dtype)
        for region in range(2):
            fill = pltpu.make_async_copy(
                obuf.at[1], y_hbm.at[pl.ds(N_ASSIGN + region * bm, bm)], ssem.at[1])
            fill.start()
            fill.wait()
        start_gather(dst_ref, 0, n_cur)

    start_gather(dstn_ref, 1 - slot, n_next)
    wait_scatter(slot, n_prev2)

    @pl.when(n_cur > 0)
    def _():
        @pl.when(fresh_ref[b] == 1)
        def _():
            for c in weight_copies(be_ref[b]):
                c.wait()
            w1b[...] = _bf16(w1s[...])
            w3b[...] = _bf16(w3s[...])
            w2b[...] = _bf16(w2s[...])

            @pl.when(nxt_ref[b] < N_EXPERTS)
            def _():
                for c in weight_copies(nxt_ref[b]):
                    c.start(priority=1)

        wait_gather(slot, n_cur)
        xb = _bf16(xbuf[slot])
        a = _bf16(jax.nn.silu(_dot(xb, w1b[...])) * _dot(xb, w3b[...]))
        obuf[slot] = _dot(a, w2b[...])
        start_scatter(slot, n_cur)

    @pl.when(b == nb - 1)
    def _():
        wait_scatter(1 - slot, n_prev)
        wait_scatter(slot, n_cur)


def _experts(x1, w1, w3, w2, dst, block_e, n_valid, fresh, next_e):
    bm = EXPERT_BM
    nb = N_EXPERT_BLOCKS
    smem_blk = lambda f: pl.BlockSpec((1, 1, bm), f, memory_space=pltpu.SMEM)
    hbm = pl.BlockSpec(memory_space=pl.ANY)
    grid_spec = pltpu.PrefetchScalarGridSpec(
        num_scalar_prefetch=4,
        grid=(nb,),
        in_specs=[
            smem_blk(lambda b, *_: (b, 0, 0)),
            smem_blk(lambda b, *_: (jnp.minimum(b + 1, nb - 1), 0, 0)),
            hbm, hbm, hbm, hbm,
        ],
        out_specs=hbm,
        scratch_shapes=[
            pltpu.VMEM((2, bm, D_MODEL), jnp.float32),
            pltpu.VMEM((2, bm, D_MODEL), jnp.float32),
            pltpu.VMEM((D_MODEL, D_EXPERT), jnp.float32),
            pltpu.VMEM((D_MODEL, D_EXPERT), jnp.float32),
            pltpu.VMEM((D_EXPERT, D_MODEL), jnp.float32),
            pltpu.VMEM((D_MODEL, D_EXPERT), jnp.bfloat16),
            pltpu.VMEM((D_MODEL, D_EXPERT), jnp.bfloat16),
            pltpu.VMEM((D_EXPERT, D_MODEL), jnp.bfloat16),
            pltpu.SemaphoreType.DMA((2,)),
            pltpu.SemaphoreType.DMA((2,)),
            pltpu.SemaphoreType.DMA((3,)),
        ],
    )
    return pl.pallas_call(
        _experts_kernel,
        grid_spec=grid_spec,
        out_shape=jax.ShapeDtypeStruct((N_ASSIGN + 2 * bm, D_MODEL), jnp.float32),
        compiler_params=pltpu.CompilerParams(
            dimension_semantics=("arbitrary",), vmem_limit_bytes=40 * 1024 * 1024),
        name="experts",
    )(block_e, n_valid, fresh, next_e, dst, dst, x1, w1, w3, w2)


def _combine_kernel(*refs):
    y_refs = refs[:TOP_K]
    w_ref, x_ref, s1_ref, s3_ref, s2_ref, g_ref, b_ref, o_ref = refs[TOP_K:]
    x1 = x_ref[...]
    xb = _bf16(x1)
    a = _bf16(jax.nn.silu(_dot(xb, s1_ref[...])) * _dot(xb, s3_ref[...]))
    ffn = _dot(a, s2_ref[...])
    wts = w_ref[...]
    routed = wts[:, 0:1] * y_refs[0][...]
    for k in range(1, TOP_K):
        routed = routed + wts[:, k:k + 1] * y_refs[k][...]
    o_ref[...] = _layer_norm(ALPHA * x1 + (routed + ffn), g_ref[...], b_ref[...])


def _combine(y, wts, x1, s1, s3, s2, g, b):
    tm = COMBINE_TM
    nt = N_TOK // tm
    y_specs = [pl.BlockSpec((tm, D_MODEL), functools.partial(lambda i, k: (k * nt + i, 0), k=k))
               for k in range(TOP_K)]
    return pl.pallas_call(
        _combine_kernel,
        grid=(nt,),
        in_specs=y_specs + [
            pl.BlockSpec((tm, TOP_K), lambda i: (i, 0)),
            pl.BlockSpec((tm, D_MODEL), lambda i: (i, 0)),
            pl.BlockSpec((D_MODEL, D_SHARED), lambda i: (0, 0)),
            pl.BlockSpec((D_MODEL, D_SHARED), lambda i: (0, 0)),
            pl.BlockSpec((D_SHARED, D_MODEL), lambda i: (0, 0)),
            pl.BlockSpec((1, D_MODEL), lambda i: (0, 0)),
            pl.BlockSpec((1, D_MODEL), lambda i: (0, 0)),
        ],
        out_specs=pl.BlockSpec((tm, D_MODEL), lambda i: (i, 0)),
        out_shape=jax.ShapeDtypeStruct((N_TOK, D_MODEL), jnp.float32),
        compiler_params=pltpu.CompilerParams(
            dimension_semantics=("arbitrary",), vmem_limit_bytes=VMEM_LIMIT_CAP),
        name="combine",
    )(*([y] * TOP_K), wts, x1, s1, s3, s2, g, b)


def kernel(x, w_in, b_in, sinks, sgu_ln_g, sgu_ln_b, w_spatial, b_spatial, w_branch_attn,
           w_branch_sgu, w_out, ln1_g, ln1_b, w_router, router_bias, w1, w3, w2, ws1, ws3, ws2,
           ln2_g, ln2_b):
    assert x.shape == (BATCH, SEQ, D_MODEL) and w_in.shape == (1, D_MODEL, IN_W)
    x2d = x.reshape(N_TOK, D_MODEL)
    h, hkv = _in_proj(_bf16(x2d), w_in[0], b_in)
    attn, sgu = _mixers(h, hkv, sinks[0], sgu_ln_g[0], sgu_ln_b[0], w_spatial[0], b_spatial[0])
    x1, logits_t = _merge(attn, sgu, h, x2d, _bf16(w_branch_attn[0]), _bf16(w_branch_sgu[0]),
                          _bf16(w_out[0]), ln1_g, ln1_b, _router_parts(w_router[0]))
    idx_t, w_t = _route(logits_t, router_bias[0])
    y = _experts(x1, w1[0], w3[0], w2[0], *_dispatch_plan(idx_t))
    out = _combine(y, w_t.T, x1, _bf16(ws1[0]), _bf16(ws3[0]), _bf16(ws2[0]), ln2_g, ln2_b)
    return out.reshape(BATCH, SEQ, D_MODEL)
```

```python
import functools
import math

import numpy as np
import jax
import jax.numpy as jnp
from jax import lax
from jax.experimental import pallas as pl
from jax.experimental.pallas import tpu as pltpu

D_MODEL = 2048
BATCH = 2
SEQ = 4096
N_TOK = BATCH * SEQ
N_Q_HEADS = 32
N_KV_HEADS = 4
HEAD_DIM = 64
GQA = N_Q_HEADS // N_KV_HEADS
WINDOW = 128
ROPE_THETA = 500000.0
ROT_DIM = HEAD_DIM // 4
SGU_GROUPS = 8
SGU_CH = 128
N_EXPERTS = 64
N_EXPERT_GROUPS = 8
GROUP_SIZE = N_EXPERTS // N_EXPERT_GROUPS
TOPK_GROUPS = 4
TOP_K = 8
D_EXPERT = 512
D_SHARED = 512
ROUTED_SCALE = 2.5
ATTN_W = N_Q_HEADS * HEAD_DIM
KV_W = N_KV_HEADS * HEAD_DIM
SGU_W = SGU_GROUPS * SGU_CH
IN_W = ATTN_W + 2 * KV_W + 2 * SGU_W + 2 * D_MODEL
ALPHA = 2.0 ** 0.25
LN_EPS = 1e-5
N_ASSIGN = N_TOK * TOP_K

LANES = 128
VMEM_LIMIT_CAP = 56 * 1024 * 1024

PROJ_TM = 512
PROJ_TN = 1024
MERGE_TM = 256
ROUTE_TT = 512
EXPERT_BM = 256
COMBINE_TM = 128
N_EXPERT_BLOCKS = -(-(N_ASSIGN + N_EXPERTS * (EXPERT_BM - 1)) // EXPERT_BM)
N_SLOTS = N_EXPERT_BLOCKS * EXPERT_BM

H_Q, H_GA, H_GB, H_U, H_VG = 0, 2048, 4096, 6144, 7168
H_W = 8192
PROJ_UNIT = 512
_SRC_UNIT = np.array([0, 1, 2, 3, 9, 10, 11, 12, 13, 14, 15, 16, 5, 6, 7, 8], np.int32)
_SRC_UNIT_KV = ATTN_W // PROJ_UNIT
_N_PROJ_TILES = H_W // PROJ_TN
_Q_TILES = ATTN_W // PROJ_TN
_GATE_END = H_U // PROJ_TN


def _bf16(a):
    return a.astype(jnp.bfloat16)


def _dot(a, b):
    return jnp.dot(a, b, preferred_element_type=jnp.float32)


def _dot_nt(a, b):
    return lax.dot_general(a, b, (((1,), (1,)), ((), ())), preferred_element_type=jnp.float32)


def _rope_slab(x, c, s_next, s_prev):
    return (x * c + pltpu.roll(x, LANES - ROT_DIM // 2, axis=1) * s_next
            + pltpu.roll(x, ROT_DIM // 2, axis=1) * s_prev)


def _in_proj_kernel(src_ref, x_ref, wa_ref, wb_ref, ba_ref, bb_ref, c_ref, sn_ref, sp_ref, o_ref,
                    wbf_ref):
    j = pl.program_id(0)
    i = pl.program_id(1)

    @pl.when(i == 0)
    def _():
        wbf_ref[:, :PROJ_UNIT] = _bf16(wa_ref[...])
        wbf_ref[:, PROJ_UNIT:] = _bf16(wb_ref[...])

    def project():
        bias = jnp.concatenate([ba_ref[...], bb_ref[...]], axis=1)
        return _dot(x_ref[...], wbf_ref[...]) + bias

    @pl.when(j < _Q_TILES)
    def _():
        acc = project()
        c, sn, sp = c_ref[...], sn_ref[...], sp_ref[...]
        scale = HEAD_DIM ** -0.5
        for t in range(PROJ_TN // LANES):
            sl = slice(t * LANES, (t + 1) * LANES)
            o_ref[:, sl] = _bf16(_rope_slab(acc[:, sl], c, sn, sp) * scale)

    @pl.when((j >= _Q_TILES) & (j < _GATE_END))
    def _():
        o_ref[...] = _bf16(jax.nn.sigmoid(project()))

    @pl.when(j >= _GATE_END)
    def _():
        o_ref[...] = _bf16(jax.nn.gelu(project()))


def _kv_proj_kernel(x_ref, w_ref, b_ref, c_ref, sn_ref, sp_ref, o_ref, wbf_ref):
    @pl.when(pl.program_id(0) == 0)
    def _():
        wbf_ref[...] = _bf16(w_ref[...])

    acc = _dot(x_ref[...], wbf_ref[...]) + b_ref[...]
    c, sn, sp = c_ref[...], sn_ref[...], sp_ref[...]
    for t in range(2 * KV_W // LANES):
        sl = slice(t * LANES, (t + 1) * LANES)
        if t < KV_W // LANES:
            o_ref[:, sl] = _bf16(_rope_slab(acc[:, sl], c, sn, sp))
        else:
            o_ref[:, sl] = _bf16(acc[:, sl])


def _rope_tables():
    half = ROT_DIM // 2
    inv_freq = ROPE_THETA ** (-np.arange(0, ROT_DIM, 2, dtype=np.float32) / ROT_DIM)
    pos = np.arange(SEQ, dtype=np.float32)
    ang = jnp.asarray(pos[:, None] * inv_freq[None, :].astype(np.float32), jnp.float32)
    cos, sin = jnp.cos(ang), jnp.sin(ang)
    ones = jnp.ones((SEQ, HEAD_DIM - ROT_DIM), jnp.float32)
    zeros = jnp.zeros((SEQ, HEAD_DIM - ROT_DIM), jnp.float32)
    zh = jnp.zeros((SEQ, half), jnp.float32)
    c = jnp.concatenate([cos, cos, ones], axis=1)
    s_next = jnp.concatenate([-sin, zh, zeros], axis=1)
    s_prev = jnp.concatenate([zh, sin, zeros], axis=1)
    rep = LANES // HEAD_DIM
    return tuple(jnp.tile(t, (1, rep)) for t in (c, s_next, s_prev))


def _in_proj(x_bf, w_in, b_in):
    c, sn, sp = _rope_tables()
    n_i = N_TOK // PROJ_TM
    pos_tiles = SEQ // PROJ_TM
    tbl = pl.BlockSpec((PROJ_TM, LANES), lambda j, i, src: (i % pos_tiles, 0))
    grid_spec = pltpu.PrefetchScalarGridSpec(
        num_scalar_prefetch=1,
        grid=(_N_PROJ_TILES, n_i),
        in_specs=[
            pl.BlockSpec((PROJ_TM, D_MODEL), lambda j, i, src: (i, 0)),
            pl.BlockSpec((D_MODEL, PROJ_UNIT), lambda j, i, src: (0, src[2 * j])),
            pl.BlockSpec((D_MODEL, PROJ_UNIT), lambda j, i, src: (0, src[2 * j + 1])),
            pl.BlockSpec((1, PROJ_UNIT), lambda j, i, src: (0, src[2 * j])),
            pl.BlockSpec((1, PROJ_UNIT), lambda j, i, src: (0, src[2 * j + 1])),
            tbl, tbl, tbl,
        ],
        out_specs=pl.BlockSpec((PROJ_TM, PROJ_TN), lambda j, i, src: (i, j)),
        scratch_shapes=[pltpu.VMEM((D_MODEL, PROJ_TN), jnp.bfloat16)],
    )
    h = pl.pallas_call(
        _in_proj_kernel,
        grid_spec=grid_spec,
        out_shape=jax.ShapeDtypeStruct((N_TOK, H_W), jnp.bfloat16),
        compiler_params=pltpu.CompilerParams(
            dimension_semantics=("arbitrary", "arbitrary"),
            vmem_limit_bytes=48 * 1024 * 1024),
        name="in_proj",
    )(jnp.asarray(_SRC_UNIT), x_bf, w_in, w_in, b_in, b_in, c, sn, sp)

    tbl1 = pl.BlockSpec((PROJ_TM, LANES), lambda i: (i % pos_tiles, 0))
    hkv = pl.pallas_call(
        _kv_proj_kernel,
        grid=(n_i,),
        in_specs=[
            pl.BlockSpec((PROJ_TM, D_MODEL), lambda i: (i, 0)),
            pl.BlockSpec((D_MODEL, 2 * KV_W), lambda i: (0, _SRC_UNIT_KV)),
            pl.BlockSpec((1, 2 * KV_W), lambda i: (0, _SRC_UNIT_KV)),
            tbl1, tbl1, tbl1,
        ],
        out_specs=pl.BlockSpec((PROJ_TM, 2 * KV_W), lambda i: (i, 0)),
        out_shape=jax.ShapeDtypeStruct((N_TOK, 2 * KV_W), jnp.bfloat16),
        scratch_shapes=[pltpu.VMEM((D_MODEL, 2 * KV_W), jnp.bfloat16)],
        compiler_params=pltpu.CompilerParams(
            dimension_semantics=("arbitrary",), vmem_limit_bytes=32 * 1024 * 1024),
        name="kv_proj",
    )(x_bf, w_in, b_in, c, sn, sp)
    return h, hkv


def _mixers_kernel(sink_ref, q_ref, kvc_ref, kvp_ref, u_ref, vg_ref, lng_ref, lnb_ref,
                   ws_ref, bs_ref, attn_ref, sgu_ref):
    n = pl.program_id(0)
    w = WINDOW
    first_key = jnp.where((n % (SEQ // w)) == 0, w, 0)
    pairs = GQA // 2
    rows = pairs * w

    qi = lax.broadcasted_iota(jnp.int32, (w, 2 * w), 0)
    kj = lax.broadcasted_iota(jnp.int32, (w, 2 * w), 1)
    valid = (kj > qi) & (kj <= qi + w) & (kj >= first_key)
    valid = jnp.concatenate([valid] * pairs, axis=0)
    lane = lax.broadcasted_iota(jnp.int32, (2 * w, LANES), 1)
    low = lane < HEAD_DIM
    lane_r = lax.broadcasted_iota(jnp.int32, (rows, LANES), 1)
    low_r = lane_r < HEAD_DIM
    ones_low = jnp.where(low, 1.0, 0.0).astype(jnp.bfloat16)
    ones_high = jnp.where(low, 0.0, 1.0).astype(jnp.bfloat16)
    neg_inf = jnp.float32(-jnp.inf)

    kv = jnp.concatenate([kvp_ref[...], kvc_ref[...]], axis=0).astype(jnp.float32)

    def padded(group, head_is_high):
        rolled = pltpu.roll(group, HEAD_DIM, axis=1)
        if head_is_high:
            lo_half, hi_half = rolled, group
        else:
            lo_half, hi_half = group, rolled
        return (_bf16(jnp.where(low, lo_half, 0.0)), _bf16(jnp.where(low, 0.0, hi_half)))

    for h in range(N_KV_HEADS):
        g0 = (h // 2) * LANES
        k_lo, k_hi = padded(kv[:, g0:g0 + LANES], h % 2 == 1)
        v_lo, v_hi = padded(kv[:, KV_W + g0:KV_W + g0 + LANES], h % 2 == 1)
        r_even = jnp.concatenate([v_lo, ones_low], axis=1)
        r_odd = jnp.concatenate([v_hi, ones_high], axis=1)
        q4 = jnp.concatenate(
            [q_ref[:, (h * pairs + p) * LANES:(h * pairs + p + 1) * LANES] for p in range(pairs)],
            axis=0)
        sink_e = jnp.concatenate(
            [jnp.full((w, 1), sink_ref[h * GQA + 2 * p], jnp.float32) for p in range(pairs)], axis=0)
        sink_o = jnp.concatenate(
            [jnp.full((w, 1), sink_ref[h * GQA + 2 * p + 1], jnp.float32) for p in range(pairs)], axis=0)

        s_e = jnp.where(valid, _dot_nt(q4, k_lo), neg_inf)
        s_o = jnp.where(valid, _dot_nt(q4, k_hi), neg_inf)
        m_e = jnp.maximum(jnp.max(s_e, axis=1, keepdims=True), sink_e)
        m_o = jnp.maximum(jnp.max(s_o, axis=1, keepdims=True), sink_o)
        p_e = _bf16(jnp.exp(s_e - m_e))
        p_o = _bf16(jnp.exp(s_o - m_o))
        acc = _dot(p_e, r_even) + _dot(p_o, r_odd)
        sink_term = jnp.exp(jnp.where(low_r, sink_e - m_e, sink_o - m_o))
        out = acc[:, :LANES] / (acc[:, LANES:] + sink_term)
        for p in range(pairs):
            c0 = (h * pairs + p) * LANES
            attn_ref[:, c0:c0 + LANES] = _bf16(out[p * w:(p + 1) * w])

    ti = lax.broadcasted_iota(jnp.int32, (w, w), 0)
    si = lax.broadcasted_iota(jnp.int32, (w, w), 1)
    causal = si <= ti
    for g in range(SGU_GROUPS):
        sl = slice(g * SGU_CH, (g + 1) * SGU_CH)
        x = vg_ref[:, sl].astype(jnp.float32)
        mu = jnp.mean(x, axis=-1, keepdims=True)
        xc = x - mu
        var = jnp.mean(xc * xc, axis=-1, keepdims=True)
        vn = xc * lax.rsqrt(var + LN_EPS) * lng_ref[:, sl] + lnb_ref[:, sl]
        wsg = _bf16(jnp.where(causal, ws_ref[g], 0.0))
        sv = _dot(wsg, _bf16(vn)) + bs_ref[g]
        sgu_ref[:, sl] = _bf16(u_ref[:, sl].astype(jnp.float32) * sv)


def _mixers(h, hkv, sinks, ln_g, ln_b, w_s, b_s):
    w = WINDOW
    nb = N_TOK // w
    grid_spec = pltpu.PrefetchScalarGridSpec(
        num_scalar_prefetch=0,
        grid=(nb,),
        in_specs=[
            pl.BlockSpec(memory_space=pltpu.SMEM),
            pl.BlockSpec((w, ATTN_W), lambda n: (n, H_Q // ATTN_W)),
            pl.BlockSpec((w, 2 * KV_W), lambda n: (n, 0)),
            pl.BlockSpec((w, 2 * KV_W), lambda n: (jnp.maximum(n - 1, 0), 0)),
            pl.BlockSpec((w, SGU_W), lambda n: (n, H_U // SGU_W)),
            pl.BlockSpec((w, SGU_W), lambda n: (n, H_VG // SGU_W)),
            pl.BlockSpec((1, SGU_W), lambda n: (0, 0)),
            pl.BlockSpec((1, SGU_W), lambda n: (0, 0)),
            pl.BlockSpec((SGU_GROUPS, w, w), lambda n: (0, 0, 0)),
            pl.BlockSpec((SGU_GROUPS, w, 1), lambda n: (0, 0, 0)),
        ],
        out_specs=[
            pl.BlockSpec((w, ATTN_W), lambda n: (n, 0)),
            pl.BlockSpec((w, SGU_W), lambda n: (n, 0)),
        ],
    )
    return pl.pallas_call(
        _mixers_kernel,
        grid_spec=grid_spec,
        out_shape=[jax.ShapeDtypeStruct((N_TOK, ATTN_W), jnp.bfloat16),
                   jax.ShapeDtypeStruct((N_TOK, SGU_W), jnp.bfloat16)],
        compiler_params=pltpu.CompilerParams(
            dimension_semantics=("arbitrary",), vmem_limit_bytes=32 * 1024 * 1024),
        name="mixers",
    )(sinks, h, hkv, hkv, h, h, ln_g.reshape(1, SGU_W), ln_b.reshape(1, SGU_W), w_s,
      b_s.reshape(SGU_GROUPS, w, 1))


def _layer_norm(z, g, b):
    mu = jnp.mean(z, axis=-1, keepdims=True)
    zc = z - mu
    var = jnp.mean(zc * zc, axis=-1, keepdims=True)
    return zc * lax.rsqrt(var + LN_EPS) * g + b


def _merge_kernel(attn_ref, sgu_ref, ga_ref, gb_ref, x_ref, wa_ref, wb_ref, wo_ref, g_ref, b_ref,
                  wr_ref, x1_ref, lg_ref):
    mix = (ga_ref[...].astype(jnp.float32) * _dot(attn_ref[...], wa_ref[...])
           + gb_ref[...].astype(jnp.float32) * _dot(sgu_ref[...], wb_ref[...]))
    z = ALPHA * x_ref[...] + _dot(_bf16(mix), wo_ref[...])
    x1 = _layer_norm(z, g_ref[...], b_ref[...])
    x1_ref[...] = x1
    x_hi = _bf16(x1)
    x_lo = _bf16(x1 - x_hi.astype(jnp.float32))
    parts = _dot(x_hi, wr_ref[...]) + _dot(x_lo, wr_ref[...])
    lg_ref[...] = parts + pltpu.roll(parts, N_EXPERTS, axis=1)


def _router_parts(w_router):
    hi = _bf16(w_router)
    lo = _bf16(w_router - hi.astype(jnp.float32))
    return jnp.concatenate([hi, lo], axis=1)


def _merge(attn, sgu, h, x2d, wa, wb, wo, g, b, wr):
    tm = MERGE_TM
    resident = pl.Buffered(1)
    grid_spec = pltpu.PrefetchScalarGridSpec(
        num_scalar_prefetch=0,
        grid=(N_TOK // tm,),
        in_specs=[
            pl.BlockSpec((tm, ATTN_W), lambda i: (i, 0)),
            pl.BlockSpec((tm, SGU_W), lambda i: (i, 0)),
            pl.BlockSpec((tm, D_MODEL), lambda i: (i, H_GA // D_MODEL)),
            pl.BlockSpec((tm, D_MODEL), lambda i: (i, H_GB // D_MODEL)),
            pl.BlockSpec((tm, D_MODEL), lambda i: (i, 0)),
            pl.BlockSpec((ATTN_W, D_MODEL), lambda i: (0, 0), pipeline_mode=resident),
            pl.BlockSpec((SGU_W, D_MODEL), lambda i: (0, 0), pipeline_mode=resident),
            pl.BlockSpec((D_MODEL, D_MODEL), lambda i: (0, 0), pipeline_mode=resident),
            pl.BlockSpec((1, D_MODEL), lambda i: (0, 0)),
            pl.BlockSpec((1, D_MODEL), lambda i: (0, 0)),
            pl.BlockSpec((D_MODEL, 2 * N_EXPERTS), lambda i: (0, 0)),
        ],
        out_specs=[
            pl.BlockSpec((tm, D_MODEL), lambda i: (i, 0)),
            pl.BlockSpec((tm, 2 * N_EXPERTS), lambda i: (i, 0)),
        ],
    )
    x1, lg = pl.pallas_call(
        _merge_kernel,
        grid_spec=grid_spec,
        out_shape=[jax.ShapeDtypeStruct((N_TOK, D_MODEL), jnp.float32),
                   jax.ShapeDtypeStruct((N_TOK, 2 * N_EXPERTS), jnp.float32)],
        compiler_params=pltpu.CompilerParams(
            dimension_semantics=("arbitrary",), vmem_limit_bytes=VMEM_LIMIT_CAP),
        name="merge",
    )(attn, sgu, h, h, x2d, wa, wb, wo, g, b, wr)
    return x1, lg[:, :N_EXPERTS].T


def _first_argmax(v, rows):
    m = jnp.max(v, axis=0, keepdims=True)
    i = jnp.min(jnp.where(v == m, rows, float(v.shape[0])), axis=0, keepdims=True)
    return m, i


def _row_index(shape):
    return lax.broadcasted_iota(jnp.int32, shape, 0).astype(jnp.float32)


def _route_kernel(lg_ref, bias_ref, idx_ref, w_ref):
    tt = lg_ref.shape[1]
    neg_inf = jnp.float32(-jnp.inf)
    scores = jax.nn.sigmoid(lg_ref[...])
    biased = scores + bias_ref[...]
    row_g = _row_index((GROUP_SIZE, tt))
    gs = []
    for g in range(N_EXPERT_GROUPS):
        blk = biased[g * GROUP_SIZE:(g + 1) * GROUP_SIZE]
        m1, i1 = _first_argmax(blk, row_g)
        m2 = jnp.max(jnp.where(row_g == i1, neg_inf, blk), axis=0, keepdims=True)
        gs.append(m1 + m2)
    cur = jnp.concatenate(gs, axis=0)
    row_n = _row_index((N_EXPERT_GROUPS, tt))
    sel = jnp.zeros((N_EXPERT_GROUPS, tt), jnp.float32)
    for _ in range(TOPK_GROUPS):
        _, i = _first_argmax(cur, row_n)
        hit = row_n == i
        sel = jnp.where(hit, 1.0, sel)
        cur = jnp.where(hit, neg_inf, cur)
    emask = jnp.concatenate(
        [jnp.broadcast_to(sel[g:g + 1], (GROUP_SIZE, tt)) for g in range(N_EXPERT_GROUPS)], axis=0)
    masked = jnp.where(emask > 0.5, biased, neg_inf)
    row_e = _row_index((N_EXPERTS, tt))
    idx_rows, w_rows = [], []
    for _ in range(TOP_K):
        _, i = _first_argmax(masked, row_e)
        hit = row_e == i
        w_rows.append(jnp.sum(jnp.where(hit, scores, 0.0), axis=0, keepdims=True))
        idx_rows.append(i)
        masked = jnp.where(hit, neg_inf, masked)
    wsel = jnp.concatenate(w_rows, axis=0)
    idx_ref[...] = jnp.concatenate(idx_rows, axis=0).astype(jnp.int32)
    w_ref[...] = wsel / (jnp.sum(wsel, axis=0, keepdims=True) + 1e-20) * ROUTED_SCALE


def _route(logits_t, bias):
    tt = ROUTE_TT
    return pl.pallas_call(
        _route_kernel,
        grid=(N_TOK // tt,),
        in_specs=[pl.BlockSpec((N_EXPERTS, tt), lambda i: (0, i)),
                  pl.BlockSpec((N_EXPERTS, 1), lambda i: (0, 0))],
        out_specs=[pl.BlockSpec((TOP_K, tt), lambda i: (0, i)),
                   pl.BlockSpec((TOP_K, tt), lambda i: (0, i))],
        out_shape=[jax.ShapeDtypeStruct((TOP_K, N_TOK), jnp.int32),
                   jax.ShapeDtypeStruct((TOP_K, N_TOK), jnp.float32)],
        compiler_params=pltpu.CompilerParams(dimension_semantics=("arbitrary",)),
        name="route",
    )(logits_t, bias.reshape(N_EXPERTS, 1))


def _dispatch_plan(idx_t):
    bm = EXPERT_BM
    nb = N_EXPERT_BLOCKS
    flat_e = idx_t.T.reshape(-1)
    counts = jnp.sum((flat_e[:, None] == jnp.arange(N_EXPERTS, dtype=jnp.int32)[None, :])
                     .astype(jnp.int32), axis=0)
    padded = (counts + bm - 1) // bm * bm
    pad_end = jnp.cumsum(padded)
    fill_end = jnp.cumsum(padded - counts)
    n_fill = N_SLOTS - N_ASSIGN
    fill_key = jnp.sum((jnp.arange(n_fill, dtype=jnp.int32)[:, None] >= fill_end[None, :])
                       .astype(jnp.int32), axis=1)
    keys = jnp.concatenate([flat_e, fill_key])
    pos_bits = (N_SLOTS - 1).bit_length()
    slot = jnp.arange(N_SLOTS, dtype=jnp.int32)
    src = jnp.sort((keys << pos_bits) | slot) & ((1 << pos_bits) - 1)
    real = src < N_ASSIGN
    dst = jnp.where(real, (src & (TOP_K - 1)) * N_TOK + (src >> (TOP_K.bit_length() - 1)),
                    N_ASSIGN + (slot & (2 * bm - 1)))
    dump_block = N_ASSIGN + jnp.arange(bm, dtype=jnp.int32)
    dst = jnp.concatenate([dump_block, dst, dst[-bm:]]).reshape(nb + 2, 1, bm)
    n_used = jnp.sum(jnp.any(real.reshape(nb, bm), axis=1).astype(jnp.int32))
    block_start = jnp.arange(nb + 1, dtype=jnp.int32) * bm
    block_e = jnp.sum((block_start[:, None] >= pad_end[None, :]).astype(jnp.int32), axis=1)
    block_e = jnp.minimum(block_e, N_EXPERTS - 1).astype(jnp.int32)
    prev_e = jnp.concatenate([jnp.full((1,), -1, jnp.int32), block_e[:-1]])
    fresh = (block_e != prev_e).astype(jnp.int32)
    e_ids = jnp.arange(N_EXPERTS, dtype=jnp.int32)
    later = (counts > 0)[None, :] & (e_ids[None, :] > e_ids[:, None])
    next_of = jnp.min(jnp.where(later, e_ids[None, :], N_EXPERTS), axis=1).astype(jnp.int32)
    return dst, n_used.reshape(1), block_e, fresh, next_of[block_e]


def _experts_kernel(nused_ref, be_ref, fresh_ref, nxt_ref, dstp_ref, dstc_ref, dstn_ref, x_hbm, w1_hbm,
                    w3_hbm, w2_hbm, y_hbm, xbuf, obuf, w1s, w3s, w2s, w1b, w3b, w2b, gsem, ssem, wsem):
    b = pl.program_id(0)
    bm = EXPERT_BM
    slot = b % 2
    n_used = nused_ref[0]

    def gather_copy(idx_ref, s, r):
        tok = idx_ref[0, 0, r] & (N_TOK - 1)
        return pltpu.make_async_copy(x_hbm.at[pl.ds(tok, 1)], xbuf.at[s, pl.ds(r, 1)], gsem.at[s])

    def start_gather(idx_ref, s):
        for r in range(bm):
            gather_copy(idx_ref, s, r).start()

    def wait_gather(s):
        pltpu.make_async_copy(x_hbm.at[pl.ds(0, bm)], xbuf.at[s], gsem.at[s]).wait()

    def start_scatter_prev(s):
        for r in range(bm):
            row = dstp_ref[0, 0, r]
            pltpu.make_async_copy(obuf.at[s, pl.ds(r, 1)], y_hbm.at[pl.ds(row, 1)],
                                  ssem.at[s]).start()

    def wait_scatter(s):
        pltpu.make_async_copy(obuf.at[s], y_hbm.at[pl.ds(0, bm)], ssem.at[s]).wait()

    def weight_copies(e):
        return (pltpu.make_async_copy(w1_hbm.at[e], w1s, wsem.at[0]),
                pltpu.make_async_copy(w3_hbm.at[e], w3s, wsem.at[1]),
                pltpu.make_async_copy(w2_hbm.at[e], w2s, wsem.at[2]))

    @pl.when(b == 0)
    def _():
        for c in weight_copies(be_ref[0]):
            c.start(priority=1)
        obuf[1] = jnp.zeros(obuf.shape[1:], obuf.dtype)
        fill = pltpu.make_async_copy(obuf.at[1], y_hbm.at[pl.ds(N_ASSIGN + bm, bm)], ssem.at[0])
        fill.start()
        fill.wait()
        start_gather(dstc_ref, 0)

    @pl.when((b >= 1) & (b <= n_used))
    def _():
        wait_scatter(slot)

    @pl.when((b < n_used) & (fresh_ref[b] == 1))
    def _():
        for c in weight_copies(be_ref[b]):
            c.wait()
        w1b[...] = _bf16(w1s[...])
        w3b[...] = _bf16(w3s[...])
        w2b[...] = _bf16(w2s[...])

        @pl.when(nxt_ref[b] < N_EXPERTS)
        def _():
            for c in weight_copies(nxt_ref[b]):
                c.start(priority=1)

    @pl.when(b < n_used)
    def _():
        wait_gather(slot)
        start_scatter_prev(1 - slot)
        xb = _bf16(xbuf[slot])
        a = _bf16(jax.nn.silu(_dot(xb, w1b[...])) * _dot(xb, w3b[...]))
        start_gather(dstn_ref, 1 - slot)
        obuf[slot] = _dot(a, w2b[...])

    @pl.when(b == n_used)
    def _():
        wait_gather(slot)
        start_scatter_prev(1 - slot)
        wait_scatter(1 - slot)


def _experts(x1, w1, w3, w2, dst, n_used, block_e, fresh, next_e):
    bm = EXPERT_BM
    nb = N_EXPERT_BLOCKS
    smem_blk = lambda f: pl.BlockSpec((1, 1, bm), f, memory_space=pltpu.SMEM)
    hbm = pl.BlockSpec(memory_space=pl.ANY)
    grid_spec = pltpu.PrefetchScalarGridSpec(
        num_scalar_prefetch=4,
        grid=(nb + 1,),
        in_specs=[
            smem_blk(lambda b, *_: (b, 0, 0)),
            smem_blk(lambda b, *_: (b + 1, 0, 0)),
            smem_blk(lambda b, *_: (jnp.minimum(b + 2, nb + 1), 0, 0)),
            hbm, hbm, hbm, hbm,
        ],
        out_specs=hbm,
        scratch_shapes=[
            pltpu.VMEM((2, bm, D_MODEL), jnp.float32),
            pltpu.VMEM((2, bm, D_MODEL), jnp.float32),
            pltpu.VMEM((D_MODEL, D_EXPERT), jnp.float32),
            pltpu.VMEM((D_MODEL, D_EXPERT), jnp.float32),
            pltpu.VMEM((D_EXPERT, D_MODEL), jnp.float32),
            pltpu.VMEM((D_MODEL, D_EXPERT), jnp.bfloat16),
            pltpu.VMEM((D_MODEL, D_EXPERT), jnp.bfloat16),
            pltpu.VMEM((D_EXPERT, D_MODEL), jnp.bfloat16),
            pltpu.SemaphoreType.DMA((2,)),
            pltpu.SemaphoreType.DMA((2,)),
            pltpu.SemaphoreType.DMA((3,)),
        ],
    )
    return pl.pallas_call(
        _experts_kernel,
        grid_spec=grid_spec,
        out_shape=jax.ShapeDtypeStruct((N_ASSIGN + 2 * bm, D_MODEL), jnp.float32),
        compiler_params=pltpu.CompilerParams(
            dimension_semantics=("arbitrary",), vmem_limit_bytes=40 * 1024 * 1024),
        name="experts",
    )(n_used, block_e, fresh, next_e, dst, dst, dst, x1, w1, w3, w2)


def _combine_kernel(*refs):
    y_refs = refs[:TOP_K]
    w_ref, x_ref, s1_ref, s3_ref, s2_ref, g_ref, b_ref, o_ref = refs[TOP_K:]
    x1 = x_ref[...]
    xb = _bf16(x1)
    a = _bf16(jax.nn.silu(_dot(xb, s1_ref[...])) * _dot(xb, s3_ref[...]))
    ffn = _dot(a, s2_ref[...])
    wts = w_ref[...]
    routed = wts[:, 0:1] * y_refs[0][...]
    for k in range(1, TOP_K):
        routed = routed + wts[:, k:k + 1] * y_refs[k][...]
    o_ref[...] = _layer_norm(ALPHA * x1 + (routed + ffn), g_ref[...], b_ref[...])


def _combine(y, wts, x1, s1, s3, s2, g, b):
    tm = COMBINE_TM
    nt = N_TOK // tm
    y_specs = [pl.BlockSpec((tm, D_MODEL), functools.partial(lambda i, k: (k * nt + i, 0), k=k))
               for k in range(TOP_K)]
    return pl.pallas_call(
        _combine_kernel,
        grid=(nt,),
        in_specs=y_specs + [
            pl.BlockSpec((tm, TOP_K), lambda i: (i, 0)),
            pl.BlockSpec((tm, D_MODEL), lambda i: (i, 0)),
            pl.BlockSpec((D_MODEL, D_SHARED), lambda i: (0, 0)),
            pl.BlockSpec((D_MODEL, D_SHARED), lambda i: (0, 0)),
            pl.BlockSpec((D_SHARED, D_MODEL), lambda i: (0, 0)),
            pl.BlockSpec((1, D_MODEL), lambda i: (0, 0)),
            pl.BlockSpec((1, D_MODEL), lambda i: (0, 0)),
        ],
        out_specs=pl.BlockSpec((tm, D_MODEL), lambda i: (i, 0)),
        out_shape=jax.ShapeDtypeStruct((N_TOK, D_MODEL), jnp.float32),
        compiler_params=pltpu.CompilerParams(
            dimension_semantics=("arbitrary",), vmem_limit_bytes=VMEM_LIMIT_CAP),
        name="combine",
    )(*([y] * TOP_K), wts, x1, s1, s3, s2, g, b)


def kernel(x, w_in, b_in, sinks, sgu_ln_g, sgu_ln_b, w_spatial, b_spatial, w_branch_attn,
           w_branch_sgu, w_out, ln1_g, ln1_b, w_router, router_bias, w1, w3, w2, ws1, ws3, ws2,
           ln2_g, ln2_b):
    assert x.shape == (BATCH, SEQ, D_MODEL) and w_in.shape == (1, D_MODEL, IN_W)
    x2d = x.reshape(N_TOK, D_MODEL)
    h, hkv = _in_proj(_bf16(x2d), w_in[0], b_in)
    attn, sgu = _mixers(h, hkv, sinks[0], sgu_ln_g[0], sgu_ln_b[0], w_spatial[0], b_spatial[0])
    x1, logits_t = _merge(attn, sgu, h, x2d, _bf16(w_branch_attn[0]), _bf16(w_branch_sgu[0]),
                          _bf16(w_out[0]), ln1_g, ln1_b, _router_parts(w_router[0]))
    idx_t, w_t = _route(logits_t, router_bias[0])
    y = _experts(x1, w1[0], w3[0], w2[0], *_dispatch_plan(idx_t))
    out = _combine(y, w_t.T, x1, _bf16(ws1[0]), _bf16(ws3[0]), _bf16(ws2[0]), ln2_g, ln2_b)
    return out.reshape(BATCH, SEQ, D_MODEL)
```

```python
import functools
import math

import numpy as np
import jax
import jax.numpy as jnp
from jax import lax
from jax.experimental import pallas as pl
from jax.experimental.pallas import tpu as pltpu

D_MODEL = 2048
BATCH = 2
SEQ = 4096
N_TOK = BATCH * SEQ
N_Q_HEADS = 32
N_KV_HEADS = 4
HEAD_DIM = 64
GQA = N_Q_HEADS // N_KV_HEADS
WINDOW = 128
ROPE_THETA = 500000.0
ROT_DIM = HEAD_DIM // 4
SGU_GROUPS = 8
SGU_CH = 128
N_EXPERTS = 64
N_EXPERT_GROUPS = 8
GROUP_SIZE = N_EXPERTS // N_EXPERT_GROUPS
TOPK_GROUPS = 4
TOP_K = 8
D_EXPERT = 512
D_SHARED = 512
ROUTED_SCALE = 2.5
ATTN_W = N_Q_HEADS * HEAD_DIM
KV_W = N_KV_HEADS * HEAD_DIM
SGU_W = SGU_GROUPS * SGU_CH
IN_W = ATTN_W + 2 * KV_W + 2 * SGU_W + 2 * D_MODEL
ALPHA = 2.0 ** 0.25
LN_EPS = 1e-5
N_ASSIGN = N_TOK * TOP_K

LANES = 128
VMEM_LIMIT_CAP = 56 * 1024 * 1024

PROJ_TM = 1024
PROJ_TN = 1024
MERGE_TM = 256
ROUTE_TT = 512
EXPERT_BM = 256
EXPERT_GROUP = 32
COMBINE_TM = 128
N_EXPERT_BLOCKS = -(-(N_ASSIGN + N_EXPERTS * (EXPERT_BM - 1)) // EXPERT_BM)
N_SLOTS = N_EXPERT_BLOCKS * EXPERT_BM

H_Q, H_GA, H_GB, H_U, H_VG = 0, 2048, 4096, 6144, 7168
H_W = 8192
PROJ_UNIT = 512
_SRC_UNIT = np.array([0, 1, 2, 3, 9, 10, 11, 12, 13, 14, 15, 16, 5, 6, 7, 8], np.int32)
_SRC_UNIT_KV = ATTN_W // PROJ_UNIT
_N_PROJ_TILES = H_W // PROJ_TN
_Q_TILES = ATTN_W // PROJ_TN
_GATE_END = H_U // PROJ_TN


def _bf16(a):
    return a.astype(jnp.bfloat16)


def _dot(a, b):
    return jnp.dot(a, b, preferred_element_type=jnp.float32)


def _dot_nt(a, b):
    return lax.dot_general(a, b, (((1,), (1,)), ((), ())), preferred_element_type=jnp.float32)


def _rope_slab(x, c, s_next, s_prev):
    return (x * c + pltpu.roll(x, LANES - ROT_DIM // 2, axis=1) * s_next
            + pltpu.roll(x, ROT_DIM // 2, axis=1) * s_prev)


def _in_proj_kernel(src_ref, x_ref, wa_ref, wb_ref, ba_ref, bb_ref, c_ref, sn_ref, sp_ref, o_ref,
                    wbf_ref):
    j = pl.program_id(0)
    i = pl.program_id(1)

    @pl.when(i == 0)
    def _():
        wbf_ref[:, :PROJ_UNIT] = _bf16(wa_ref[...])
        wbf_ref[:, PROJ_UNIT:] = _bf16(wb_ref[...])

    def project():
        bias = jnp.concatenate([ba_ref[...], bb_ref[...]], axis=1)
        return _dot(x_ref[...], wbf_ref[...]) + bias

    @pl.when(j < _Q_TILES)
    def _():
        acc = project()
        c, sn, sp = c_ref[...], sn_ref[...], sp_ref[...]
        scale = HEAD_DIM ** -0.5
        for t in range(PROJ_TN // LANES):
            sl = slice(t * LANES, (t + 1) * LANES)
            o_ref[:, sl] = _bf16(_rope_slab(acc[:, sl], c, sn, sp) * scale)

    @pl.when((j >= _Q_TILES) & (j < _GATE_END))
    def _():
        o_ref[...] = _bf16(jax.nn.sigmoid(project()))

    @pl.when(j >= _GATE_END)
    def _():
        o_ref[...] = _bf16(jax.nn.gelu(project()))


def _kv_proj_kernel(x_ref, w_ref, b_ref, c_ref, sn_ref, sp_ref, o_ref, wbf_ref):
    @pl.when(pl.program_id(0) == 0)
    def _():
        wbf_ref[...] = _bf16(w_ref[...])

    acc = _dot(x_ref[...], wbf_ref[...]) + b_ref[...]
    c, sn, sp = c_ref[...], sn_ref[...], sp_ref[...]
    for t in range(2 * KV_W // LANES):
        sl = slice(t * LANES, (t + 1) * LANES)
        if t < KV_W // LANES:
            o_ref[:, sl] = _bf16(_rope_slab(acc[:, sl], c, sn, sp))
        else:
            o_ref[:, sl] = _bf16(acc[:, sl])


def _rope_tables():
    half = ROT_DIM // 2
    inv_freq = ROPE_THETA ** (-np.arange(0, ROT_DIM, 2, dtype=np.float32) / ROT_DIM)
    pos = np.arange(SEQ, dtype=np.float32)
    ang = jnp.asarray(pos[:, None] * inv_freq[None, :].astype(np.float32), jnp.float32)
    cos, sin = jnp.cos(ang), jnp.sin(ang)
    ones = jnp.ones((SEQ, HEAD_DIM - ROT_DIM), jnp.float32)
    zeros = jnp.zeros((SEQ, HEAD_DIM - ROT_DIM), jnp.float32)
    zh = jnp.zeros((SEQ, half), jnp.float32)
    c = jnp.concatenate([cos, cos, ones], axis=1)
    s_next = jnp.concatenate([-sin, zh, zeros], axis=1)
    s_prev = jnp.concatenate([zh, sin, zeros], axis=1)
    rep = LANES // HEAD_DIM
    return tuple(jnp.tile(t, (1, rep)) for t in (c, s_next, s_prev))


def _in_proj(x_bf, w_in, b_in):
    c, sn, sp = _rope_tables()
    n_i = N_TOK // PROJ_TM
    pos_tiles = SEQ // PROJ_TM
    tbl = pl.BlockSpec((PROJ_TM, LANES), lambda j, i, src: (i % pos_tiles, 0))
    grid_spec = pltpu.PrefetchScalarGridSpec(
        num_scalar_prefetch=1,
        grid=(_N_PROJ_TILES, n_i),
        in_specs=[
            pl.BlockSpec((PROJ_TM, D_MODEL), lambda j, i, src: (i, 0)),
            pl.BlockSpec((D_MODEL, PROJ_UNIT), lambda j, i, src: (0, src[2 * j])),
            pl.BlockSpec((D_MODEL, PROJ_UNIT), lambda j, i, src: (0, src[2 * j + 1])),
            pl.BlockSpec((1, PROJ_UNIT), lambda j, i, src: (0, src[2 * j])),
            pl.BlockSpec((1, PROJ_UNIT), lambda j, i, src: (0, src[2 * j + 1])),
            tbl, tbl, tbl,
        ],
        out_specs=pl.BlockSpec((PROJ_TM, PROJ_TN), lambda j, i, src: (i, j)),
        scratch_shapes=[pltpu.VMEM((D_MODEL, PROJ_TN), jnp.bfloat16)],
    )
    h = pl.pallas_call(
        _in_proj_kernel,
        grid_spec=grid_spec,
        out_shape=jax.ShapeDtypeStruct((N_TOK, H_W), jnp.bfloat16),
        compiler_params=pltpu.CompilerParams(
            dimension_semantics=("arbitrary", "arbitrary"),
            vmem_limit_bytes=48 * 1024 * 1024),
        name="in_proj",
    )(jnp.asarray(_SRC_UNIT), x_bf, w_in, w_in, b_in, b_in, c, sn, sp)

    tbl1 = pl.BlockSpec((PROJ_TM, LANES), lambda i: (i % pos_tiles, 0))
    hkv = pl.pallas_call(
        _kv_proj_kernel,
        grid=(n_i,),
        in_specs=[
            pl.BlockSpec((PROJ_TM, D_MODEL), lambda i: (i, 0)),
            pl.BlockSpec((D_MODEL, 2 * KV_W), lambda i: (0, _SRC_UNIT_KV)),
            pl.BlockSpec((1, 2 * KV_W), lambda i: (0, _SRC_UNIT_KV)),
            tbl1, tbl1, tbl1,
        ],
        out_specs=pl.BlockSpec((PROJ_TM, 2 * KV_W), lambda i: (i, 0)),
        out_shape=jax.ShapeDtypeStruct((N_TOK, 2 * KV_W), jnp.bfloat16),
        scratch_shapes=[pltpu.VMEM((D_MODEL, 2 * KV_W), jnp.bfloat16)],
        compiler_params=pltpu.CompilerParams(
            dimension_semantics=("arbitrary",), vmem_limit_bytes=32 * 1024 * 1024),
        name="kv_proj",
    )(x_bf, w_in, b_in, c, sn, sp)
    return h, hkv


def _mixers_kernel(sink_ref, q_ref, kvc_ref, kvp_ref, u_ref, vg_ref, lng_ref, lnb_ref,
                   ws_ref, bs_ref, attn_ref, sgu_ref):
    n = pl.program_id(0)
    w = WINDOW
    first_key = jnp.where((n % (SEQ // w)) == 0, w, 0)
    pairs = GQA // 2
    rows = pairs * w

    qi = lax.broadcasted_iota(jnp.int32, (w, 2 * w), 0)
    kj = lax.broadcasted_iota(jnp.int32, (w, 2 * w), 1)
    valid = (kj > qi) & (kj <= qi + w) & (kj >= first_key)
    valid = jnp.concatenate([valid] * pairs, axis=0)
    lane = lax.broadcasted_iota(jnp.int32, (2 * w, LANES), 1)
    low = lane < HEAD_DIM
    lane_r = lax.broadcasted_iota(jnp.int32, (rows, LANES), 1)
    low_r = lane_r < HEAD_DIM
    ones_low = jnp.where(low, 1.0, 0.0).astype(jnp.bfloat16)
    ones_high = jnp.where(low, 0.0, 1.0).astype(jnp.bfloat16)
    neg_inf = jnp.float32(-jnp.inf)

    kv = jnp.concatenate([kvp_ref[...], kvc_ref[...]], axis=0).astype(jnp.float32)

    def padded(group, head_is_high):
        rolled = pltpu.roll(group, HEAD_DIM, axis=1)
        if head_is_high:
            lo_half, hi_half = rolled, group
        else:
            lo_half, hi_half = group, rolled
        return (_bf16(jnp.where(low, lo_half, 0.0)), _bf16(jnp.where(low, 0.0, hi_half)))

    for h in range(N_KV_HEADS):
        g0 = (h // 2) * LANES
        k_lo, k_hi = padded(kv[:, g0:g0 + LANES], h % 2 == 1)
        v_lo, v_hi = padded(kv[:, KV_W + g0:KV_W + g0 + LANES], h % 2 == 1)
        r_even = jnp.concatenate([v_lo, ones_low], axis=1)
        r_odd = jnp.concatenate([v_hi, ones_high], axis=1)
        q4 = jnp.concatenate(
            [q_ref[:, (h * pairs + p) * LANES:(h * pairs + p + 1) * LANES] for p in range(pairs)],
            axis=0)
        sink_e = jnp.concatenate(
            [jnp.full((w, 1), sink_ref[h * GQA + 2 * p], jnp.float32) for p in range(pairs)], axis=0)
        sink_o = jnp.concatenate(
            [jnp.full((w, 1), sink_ref[h * GQA + 2 * p + 1], jnp.float32) for p in range(pairs)], axis=0)

        s_e = jnp.where(valid, _dot_nt(q4, k_lo), neg_inf)
        s_o = jnp.where(valid, _dot_nt(q4, k_hi), neg_inf)
        m_e = jnp.maximum(jnp.max(s_e, axis=1, keepdims=True), sink_e)
        m_o = jnp.maximum(jnp.max(s_o, axis=1, keepdims=True), sink_o)
        p_e = _bf16(jnp.exp(s_e - m_e))
        p_o = _bf16(jnp.exp(s_o - m_o))
        acc = _dot(p_e, r_even) + _dot(p_o, r_odd)
        sink_term = jnp.exp(jnp.where(low_r, sink_e - m_e, sink_o - m_o))
        out = acc[:, :LANES] / (acc[:, LANES:] + sink_term)
        for p in range(pairs):
            c0 = (h * pairs + p) * LANES
            attn_ref[:, c0:c0 + LANES] = _bf16(out[p * w:(p + 1) * w])

    ti = lax.broadcasted_iota(jnp.int32, (w, w), 0)
    si = lax.broadcasted_iota(jnp.int32, (w, w), 1)
    causal = si <= ti
    for g in range(SGU_GROUPS):
        sl = slice(g * SGU_CH, (g + 1) * SGU_CH)
        x = vg_ref[:, sl].astype(jnp.float32)
        mu = jnp.mean(x, axis=-1, keepdims=True)
        xc = x - mu
        var = jnp.mean(xc * xc, axis=-1, keepdims=True)
        vn = xc * lax.rsqrt(var + LN_EPS) * lng_ref[:, sl] + lnb_ref[:, sl]
        wsg = _bf16(jnp.where(causal, ws_ref[g], 0.0))
        sv = _dot(wsg, _bf16(vn)) + bs_ref[g]
        sgu_ref[:, sl] = _bf16(u_ref[:, sl].astype(jnp.float32) * sv)


def _mixers(h, hkv, sinks, ln_g, ln_b, w_s, b_s):
    w = WINDOW
    nb = N_TOK // w
    grid_spec = pltpu.PrefetchScalarGridSpec(
        num_scalar_prefetch=0,
        grid=(nb,),
        in_specs=[
            pl.BlockSpec(memory_space=pltpu.SMEM),
            pl.BlockSpec((w, ATTN_W), lambda n: (n, H_Q // ATTN_W)),
            pl.BlockSpec((w, 2 * KV_W), lambda n: (n, 0)),
            pl.BlockSpec((w, 2 * KV_W), lambda n: (jnp.maximum(n - 1, 0), 0)),
            pl.BlockSpec((w, SGU_W), lambda n: (n, H_U // SGU_W)),
            pl.BlockSpec((w, SGU_W), lambda n: (n, H_VG // SGU_W)),
            pl.BlockSpec((1, SGU_W), lambda n: (0, 0)),
            pl.BlockSpec((1, SGU_W), lambda n: (0, 0)),
            pl.BlockSpec((SGU_GROUPS, w, w), lambda n: (0, 0, 0)),
            pl.BlockSpec((SGU_GROUPS, w, 1), lambda n: (0, 0, 0)),
        ],
        out_specs=[
            pl.BlockSpec((w, ATTN_W), lambda n: (n, 0)),
            pl.BlockSpec((w, SGU_W), lambda n: (n, 0)),
        ],
    )
    return pl.pallas_call(
        _mixers_kernel,
        grid_spec=grid_spec,
        out_shape=[jax.ShapeDtypeStruct((N_TOK, ATTN_W), jnp.bfloat16),
                   jax.ShapeDtypeStruct((N_TOK, SGU_W), jnp.bfloat16)],
        compiler_params=pltpu.CompilerParams(
            dimension_semantics=("arbitrary",), vmem_limit_bytes=32 * 1024 * 1024),
        name="mixers",
    )(sinks, h, hkv, hkv, h, h, ln_g.reshape(1, SGU_W), ln_b.reshape(1, SGU_W), w_s,
      b_s.reshape(SGU_GROUPS, w, 1))


def _layer_norm(z, g, b):
    mu = jnp.mean(z, axis=-1, keepdims=True)
    zc = z - mu
    var = jnp.mean(zc * zc, axis=-1, keepdims=True)
    return zc * lax.rsqrt(var + LN_EPS) * g + b


def _merge_kernel(attn_ref, sgu_ref, ga_ref, gb_ref, x_ref, wa_ref, wb_ref, wo_ref, g_ref, b_ref,
                  wr_ref, x1_ref, lg_ref):
    mix = (ga_ref[...].astype(jnp.float32) * _dot(attn_ref[...], wa_ref[...])
           + gb_ref[...].astype(jnp.float32) * _dot(sgu_ref[...], wb_ref[...]))
    z = ALPHA * x_ref[...] + _dot(_bf16(mix), wo_ref[...])
    x1 = _layer_norm(z, g_ref[...], b_ref[...])
    x1_ref[...] = x1
    x_hi = _bf16(x1)
    x_lo = _bf16(x1 - x_hi.astype(jnp.float32))
    parts = _dot(x_hi, wr_ref[...]) + _dot(x_lo, wr_ref[...])
    lg_ref[...] = parts + pltpu.roll(parts, N_EXPERTS, axis=1)


def _router_parts(w_router):
    hi = _bf16(w_router)
    lo = _bf16(w_router - hi.astype(jnp.float32))
    return jnp.concatenate([hi, lo], axis=1)


def _merge(attn, sgu, h, x2d, wa, wb, wo, g, b, wr):
    tm = MERGE_TM
    resident = pl.Buffered(1)
    grid_spec = pltpu.PrefetchScalarGridSpec(
        num_scalar_prefetch=0,
        grid=(N_TOK // tm,),
        in_specs=[
            pl.BlockSpec((tm, ATTN_W), lambda i: (i, 0)),
            pl.BlockSpec((tm, SGU_W), lambda i: (i, 0)),
            pl.BlockSpec((tm, D_MODEL), lambda i: (i, H_GA // D_MODEL)),
            pl.BlockSpec((tm, D_MODEL), lambda i: (i, H_GB // D_MODEL)),
            pl.BlockSpec((tm, D_MODEL), lambda i: (i, 0)),
            pl.BlockSpec((ATTN_W, D_MODEL), lambda i: (0, 0), pipeline_mode=resident),
            pl.BlockSpec((SGU_W, D_MODEL), lambda i: (0, 0), pipeline_mode=resident),
            pl.BlockSpec((D_MODEL, D_MODEL), lambda i: (0, 0), pipeline_mode=resident),
            pl.BlockSpec((1, D_MODEL), lambda i: (0, 0)),
            pl.BlockSpec((1, D_MODEL), lambda i: (0, 0)),
            pl.BlockSpec((D_MODEL, 2 * N_EXPERTS), lambda i: (0, 0)),
        ],
        out_specs=[
            pl.BlockSpec((tm, D_MODEL), lambda i: (i, 0)),
            pl.BlockSpec((tm, 2 * N_EXPERTS), lambda i: (i, 0)),
        ],
    )
    x1, lg = pl.pallas_call(
        _merge_kernel,
        grid_spec=grid_spec,
        out_shape=[jax.ShapeDtypeStruct((N_TOK, D_MODEL), jnp.float32),
                   jax.ShapeDtypeStruct((N_TOK, 2 * N_EXPERTS), jnp.float32)],
        compiler_params=pltpu.CompilerParams(
            dimension_semantics=("arbitrary",), vmem_limit_bytes=VMEM_LIMIT_CAP),
        name="merge",
    )(attn, sgu, h, h, x2d, wa, wb, wo, g, b, wr)
    return x1, lg[:, :N_EXPERTS].T


def _first_argmax(v, rows):
    m = jnp.max(v, axis=0, keepdims=True)
    i = jnp.min(jnp.where(v == m, rows, float(v.shape[0])), axis=0, keepdims=True)
    return m, i


def _row_index(shape):
    return lax.broadcasted_iota(jnp.int32, shape, 0).astype(jnp.float32)


def _route_kernel(lg_ref, bias_ref, idx_ref, w_ref):
    tt = lg_ref.shape[1]
    neg_inf = jnp.float32(-jnp.inf)
    scores = jax.nn.sigmoid(lg_ref[...])
    biased = scores + bias_ref[...]
    row_g = _row_index((GROUP_SIZE, tt))
    gs = []
    for g in range(N_EXPERT_GROUPS):
        blk = biased[g * GROUP_SIZE:(g + 1) * GROUP_SIZE]
        m1, i1 = _first_argmax(blk, row_g)
        m2 = jnp.max(jnp.where(row_g == i1, neg_inf, blk), axis=0, keepdims=True)
        gs.append(m1 + m2)
    cur = jnp.concatenate(gs, axis=0)
    row_n = _row_index((N_EXPERT_GROUPS, tt))
    sel = jnp.zeros((N_EXPERT_GROUPS, tt), jnp.float32)
    for _ in range(TOPK_GROUPS):
        _, i = _first_argmax(cur, row_n)
        hit = row_n == i
        sel = jnp.where(hit, 1.0, sel)
        cur = jnp.where(hit, neg_inf, cur)
    emask = jnp.concatenate(
        [jnp.broadcast_to(sel[g:g + 1], (GROUP_SIZE, tt)) for g in range(N_EXPERT_GROUPS)], axis=0)
    masked = jnp.where(emask > 0.5, biased, neg_inf)
    row_e = _row_index((N_EXPERTS, tt))
    idx_rows, w_rows = [], []
    for _ in range(TOP_K):
        _, i = _first_argmax(masked, row_e)
        hit = row_e == i
        w_rows.append(jnp.sum(jnp.where(hit, scores, 0.0), axis=0, keepdims=True))
        idx_rows.append(i)
        masked = jnp.where(hit, neg_inf, masked)
    wsel = jnp.concatenate(w_rows, axis=0)
    idx_ref[...] = jnp.concatenate(idx_rows, axis=0).astype(jnp.int32)
    w_ref[...] = wsel / (jnp.sum(wsel, axis=0, keepdims=True) + 1e-20) * ROUTED_SCALE


def _route(logits_t, bias):
    tt = ROUTE_TT
    return pl.pallas_call(
        _route_kernel,
        grid=(N_TOK // tt,),
        in_specs=[pl.BlockSpec((N_EXPERTS, tt), lambda i: (0, i)),
                  pl.BlockSpec((N_EXPERTS, 1), lambda i: (0, 0))],
        out_specs=[pl.BlockSpec((TOP_K, tt), lambda i: (0, i)),
                   pl.BlockSpec((TOP_K, tt), lambda i: (0, i))],
        out_shape=[jax.ShapeDtypeStruct((TOP_K, N_TOK), jnp.int32),
                   jax.ShapeDtypeStruct((TOP_K, N_TOK), jnp.float32)],
        compiler_params=pltpu.CompilerParams(dimension_semantics=("arbitrary",)),
        name="route",
    )(logits_t, bias.reshape(N_EXPERTS, 1))


def _dispatch_plan(idx_t):
    bm = EXPERT_BM
    nb = N_EXPERT_BLOCKS
    flat_e = idx_t.reshape(-1)
    counts = jnp.sum((flat_e[:, None] == jnp.arange(N_EXPERTS, dtype=jnp.int32)[None, :])
                     .astype(jnp.int32), axis=0)
    padded = (counts + bm - 1) // bm * bm
    pad_end = jnp.cumsum(padded)
    fill_end = jnp.cumsum(padded - counts)
    n_fill = N_SLOTS - N_ASSIGN
    fill_key = jnp.sum((jnp.arange(n_fill, dtype=jnp.int32)[:, None] >= fill_end[None, :])
                       .astype(jnp.int32), axis=1)
    keys = jnp.concatenate([flat_e, fill_key])
    pos_bits = (N_SLOTS - 1).bit_length()
    slot = jnp.arange(N_SLOTS, dtype=jnp.int32)
    src = jnp.sort((keys << pos_bits) | slot) & ((1 << pos_bits) - 1)
    real = src < N_ASSIGN
    dst = jnp.where(real, src, N_ASSIGN + (slot & (2 * bm - 1)))
    n_valid = jnp.sum(real.reshape(nb, bm).astype(jnp.int32), axis=1)
    used = n_valid > 0
    block_start = jnp.arange(nb, dtype=jnp.int32) * bm
    block_e = jnp.sum((block_start[:, None] >= pad_end[None, :]).astype(jnp.int32), axis=1)
    n_used = jnp.sum(used.astype(jnp.int32))
    last_e = block_e[jnp.maximum(n_used - 1, 0)]
    block_e = jnp.where(used, jnp.minimum(block_e, N_EXPERTS - 1), last_e).astype(jnp.int32)
    prev_e = jnp.concatenate([jnp.full((1,), -1, jnp.int32), block_e[:-1]])
    fresh = (block_e != prev_e).astype(jnp.int32)
    e_ids = jnp.arange(N_EXPERTS, dtype=jnp.int32)
    later = (counts > 0)[None, :] & (e_ids[None, :] > e_ids[:, None])
    next_of = jnp.min(jnp.where(later, e_ids[None, :], N_EXPERTS), axis=1).astype(jnp.int32)
    return dst.reshape(nb, 1, bm), block_e, n_valid, fresh, next_of[block_e]


def _experts_kernel(be_ref, nval_ref, fresh_ref, nxt_ref, dst_ref, dstn_ref, x_hbm, w1_hbm, w3_hbm,
                    w2_hbm, y_hbm, xbuf, obuf, w1s, w3s, w2s, w1b, w3b, w2b, gsem, ssem, wsem):
    b = pl.program_id(0)
    nb = pl.num_programs(0)
    bm, grp = EXPERT_BM, EXPERT_GROUP
    slot = b % 2
    n_cur = nval_ref[b]
    n_next = jnp.where(b + 1 < nb, nval_ref[jnp.minimum(b + 1, nb - 1)], 0)
    n_prev = jnp.where(b >= 1, nval_ref[jnp.maximum(b - 1, 0)], 0)
    n_prev2 = jnp.where(b >= 2, nval_ref[jnp.maximum(b - 2, 0)], 0)

    def for_groups(n_rows, fn):
        for g in range(bm // grp):
            pl.when(g * grp < n_rows)(functools.partial(fn, g))

    def gather_copy(idx_ref, s, r):
        tok = idx_ref[0, 0, r] & (N_TOK - 1)
        return pltpu.make_async_copy(x_hbm.at[pl.ds(tok, 1)], xbuf.at[s, pl.ds(r, 1)], gsem.at[s])

    def scatter_copy(s, r):
        row = dst_ref[0, 0, r]
        return pltpu.make_async_copy(obuf.at[s, pl.ds(r, 1)], y_hbm.at[pl.ds(row, 1)], ssem.at[s])

    def start_gather(idx_ref, s, n_rows):
        def group(g):
            for r in range(g * grp, (g + 1) * grp):
                gather_copy(idx_ref, s, r).start()
        for_groups(n_rows, group)

    def wait_gather(s, n_rows):
        def group(g):
            rows = pl.ds(g * grp, grp)
            pltpu.make_async_copy(x_hbm.at[rows], xbuf.at[s, rows], gsem.at[s]).wait()
        for_groups(n_rows, group)

    def start_scatter(s, n_rows):
        def group(g):
            for r in range(g * grp, (g + 1) * grp):
                scatter_copy(s, r).start()
        for_groups(n_rows, group)

    def wait_scatter(s, n_rows):
        def group(g):
            rows = pl.ds(g * grp, grp)
            pltpu.make_async_copy(obuf.at[s, rows], y_hbm.at[rows], ssem.at[s]).wait()
        for_groups(n_rows, group)

    def weight_copies(e):
        return (pltpu.make_async_copy(w1_hbm.at[e], w1s, wsem.at[0]),
                pltpu.make_async_copy(w3_hbm.at[e], w3s, wsem.at[1]),
                pltpu.make_async_copy(w2_hbm.at[e], w2s, wsem.at[2]))

    def swiglu_block(w1v, w3v, w2v):
        xb = _bf16(xbuf[slot])
        a = _bf16(jax.nn.silu(_dot(xb, w1v)) * _dot(xb, w3v))
        obuf[slot] = _dot(a, w2v)

    @pl.when(b == 0)
    def _():
        for c in weight_copies(be_ref[0]):
            c.start(priority=1)
        xbuf[...] = jnp.zeros(xbuf.shape, xbuf.dtype)
        obuf[1] = jnp.zeros(obuf.shape[1:], obuf.dtype)
        for region in range(2):
            fill = pltpu.make_async_copy(
                obuf.at[1], y_hbm.at[pl.ds(N_ASSIGN + region * bm, bm)], ssem.at[1])
            fill.start()
            fill.wait()
        start_gather(dst_ref, 0, n_cur)

    start_gather(dstn_ref, 1 - slot, n_next)
    wait_scatter(slot, n_prev2)

    @pl.when(n_cur > 0)
    def _():
        wait_gather(slot, n_cur)

        @pl.when(fresh_ref[b] == 1)
        def _():
            for c in weight_copies(be_ref[b]):
                c.wait()
            w1v, w3v, w2v = _bf16(w1s[...]), _bf16(w3s[...]), _bf16(w2s[...])
            w1b[...] = w1v
            w3b[...] = w3v
            w2b[...] = w2v
            swiglu_block(w1v, w3v, w2v)

            @pl.when(nxt_ref[b] < N_EXPERTS)
            def _():
                for c in weight_copies(nxt_ref[b]):
                    c.start(priority=1)

        @pl.when(fresh_ref[b] == 0)
        def _():
            swiglu_block(w1b[...], w3b[...], w2b[...])

        start_scatter(slot, n_cur)

    @pl.when(b == nb - 1)
    def _():
        wait_scatter(1 - slot, n_prev)
        wait_scatter(slot, n_cur)


def _experts(x1, w1, w3, w2, dst, block_e, n_valid, fresh, next_e):
    bm = EXPERT_BM
    nb = N_EXPERT_BLOCKS
    smem_blk = lambda f: pl.BlockSpec((1, 1, bm), f, memory_space=pltpu.SMEM)
    hbm = pl.BlockSpec(memory_space=pl.ANY)
    grid_spec = pltpu.PrefetchScalarGridSpec(
        num_scalar_prefetch=4,
        grid=(nb,),
        in_specs=[
            smem_blk(lambda b, *_: (b, 0, 0)),
            smem_blk(lambda b, *_: (jnp.minimum(b + 1, nb - 1), 0, 0)),
            hbm, hbm, hbm, hbm,
        ],
        out_specs=hbm,
        scratch_shapes=[
            pltpu.VMEM((2, bm, D_MODEL), jnp.float32),
            pltpu.VMEM((2, bm, D_MODEL), jnp.float32),
            pltpu.VMEM((D_MODEL, D_EXPERT), jnp.float32),
            pltpu.VMEM((D_MODEL, D_EXPERT), jnp.float32),
            pltpu.VMEM((D_EXPERT, D_MODEL), jnp.float32),
            pltpu.VMEM((D_MODEL, D_EXPERT), jnp.bfloat16),
            pltpu.VMEM((D_MODEL, D_EXPERT), jnp.bfloat16),
            pltpu.VMEM((D_EXPERT, D_MODEL), jnp.bfloat16),
            pltpu.SemaphoreType.DMA((2,)),
            pltpu.SemaphoreType.DMA((2,)),
            pltpu.SemaphoreType.DMA((3,)),
        ],
    )
    return pl.pallas_call(
        _experts_kernel,
        grid_spec=grid_spec,
        out_shape=jax.ShapeDtypeStruct((N_ASSIGN + 2 * bm, D_MODEL), jnp.float32),
        compiler_params=pltpu.CompilerParams(
            dimension_semantics=("arbitrary",), vmem_limit_bytes=40 * 1024 * 1024),
        name="experts",
    )(block_e, n_valid, fresh, next_e, dst, dst, x1, w1, w3, w2)


def _combine_kernel(*refs):
    y_refs = refs[:TOP_K]
    w_ref, x_ref, s1_ref, s3_ref, s2_ref, g_ref, b_ref, o_ref = refs[TOP_K:]
    x1 = x_ref[...]
    xb = _bf16(x1)
    a = _bf16(jax.nn.silu(_dot(xb, s1_ref[...])) * _dot(xb, s3_ref[...]))
    ffn = _dot(a, s2_ref[...])
    wts = w_ref[...]
    routed = wts[:, 0:1] * y_refs[0][...]
    for k in range(1, TOP_K):
        routed = routed + wts[:, k:k + 1] * y_refs[k][...]
    o_ref[...] = _layer_norm(ALPHA * x1 + (routed + ffn), g_ref[...], b_ref[...])


def _combine(y, wts, x1, s1, s3, s2, g, b):
    tm = COMBINE_TM
    nt = N_TOK // tm
    y_specs = [pl.BlockSpec((tm, D_MODEL), functools.partial(lambda i, k: (k * nt + i, 0), k=k))
               for k in range(TOP_K)]
    return pl.pallas_call(
        _combine_kernel,
        grid=(nt,),
        in_specs=y_specs + [
            pl.BlockSpec((tm, TOP_K), lambda i: (i, 0)),
            pl.BlockSpec((tm, D_MODEL), lambda i: (i, 0)),
            pl.BlockSpec((D_MODEL, D_SHARED), lambda i: (0, 0)),
            pl.BlockSpec((D_MODEL, D_SHARED), lambda i: (0, 0)),
            pl.BlockSpec((D_SHARED, D_MODEL), lambda i: (0, 0)),
            pl.BlockSpec((1, D_MODEL), lambda i: (0, 0)),
            pl.BlockSpec((1, D_MODEL), lambda i: (0, 0)),
        ],
        out_specs=pl.BlockSpec((tm, D_MODEL), lambda i: (i, 0)),
        out_shape=jax.ShapeDtypeStruct((N_TOK, D_MODEL), jnp.float32),
        compiler_params=pltpu.CompilerParams(
            dimension_semantics=("arbitrary",), vmem_limit_bytes=VMEM_LIMIT_CAP),
        name="combine",
    )(*([y] * TOP_K), wts, x1, s1, s3, s2, g, b)


def kernel(x, w_in, b_in, sinks, sgu_ln_g, sgu_ln_b, w_spatial, b_spatial, w_branch_attn,
           w_branch_sgu, w_out, ln1_g, ln1_b, w_router, router_bias, w1, w3, w2, ws1, ws3, ws2,
           ln2_g, ln2_b):
    assert x.shape == (BATCH, SEQ, D_MODEL) and w_in.shape == (1, D_MODEL, IN_W)
    x2d = x.reshape(N_TOK, D_MODEL)
    h, hkv = _in_proj(_bf16(x2d), w_in[0], b_in)
    attn, sgu = _mixers(h, hkv, sinks[0], sgu_ln_g[0], sgu_ln_b[0], w_spatial[0], b_spatial[0])
    x1, logits_t = _merge(attn, sgu, h, x2d, _bf16(w_branch_attn[0]), _bf16(w_branch_sgu[0]),
                          _bf16(w_out[0]), ln1_g, ln1_b, _router_parts(w_router[0]))
    idx_t, w_t = _route(logits_t, router_bias[0])
    y = _experts(x1, w1[0], w3[0], w2[0], *_dispatch_plan(idx_t))
    out = _combine(y, w_t.T, x1, _bf16(ws1[0]), _bf16(ws3[0]), _bf16(ws2[0]), ln2_g, ln2_b)
    return out.reshape(BATCH, SEQ, D_MODEL)
```

```python
import functools
import math

import numpy as np
import jax
import jax.numpy as jnp
from jax import lax
from jax.experimental import pallas as pl
from jax.experimental.pallas import tpu as pltpu

D_MODEL = 2048
BATCH = 2
SEQ = 4096
N_TOK = BATCH * SEQ
N_Q_HEADS = 32
N_KV_HEADS = 4
HEAD_DIM = 64
GQA = N_Q_HEADS // N_KV_HEADS
WINDOW = 128
ROPE_THETA = 500000.0
ROT_DIM = HEAD_DIM // 4
SGU_GROUPS = 8
SGU_CH = 128
N_EXPERTS = 64
N_EXPERT_GROUPS = 8
GROUP_SIZE = N_EXPERTS // N_EXPERT_GROUPS
TOPK_GROUPS = 4
TOP_K = 8
D_EXPERT = 512
D_SHARED = 512
ROUTED_SCALE = 2.5
ATTN_W = N_Q_HEADS * HEAD_DIM
KV_W = N_KV_HEADS * HEAD_DIM
SGU_W = SGU_GROUPS * SGU_CH
IN_W = ATTN_W + 2 * KV_W + 2 * SGU_W + 2 * D_MODEL
ALPHA = 2.0 ** 0.25
LN_EPS = 1e-5
N_ASSIGN = N_TOK * TOP_K

LANES = 128
VMEM_LIMIT_CAP = 56 * 1024 * 1024

PROJ_TM = 1024
PROJ_TN = 1024
MERGE_TM = 256
ROUTE_TT = 512
EXPERT_BM = 256
EXPERT_GROUP = 32
COMBINE_TM = 128
N_EXPERT_BLOCKS = -(-(N_ASSIGN + N_EXPERTS * (EXPERT_BM - 1)) // EXPERT_BM)
N_SLOTS = N_EXPERT_BLOCKS * EXPERT_BM

H_Q, H_GA, H_GB, H_U, H_VG = 0, 2048, 4096, 6144, 7168
H_W = 8192
PROJ_UNIT = 512
_SRC_UNIT = np.array([0, 1, 2, 3, 9, 10, 11, 12, 13, 14, 15, 16, 5, 6, 7, 8], np.int32)
_SRC_UNIT_KV = ATTN_W // PROJ_UNIT
_N_PROJ_TILES = H_W // PROJ_TN
_Q_TILES = ATTN_W // PROJ_TN
_GATE_END = H_U // PROJ_TN


def _bf16(a):
    return a.astype(jnp.bfloat16)


def _dot(a, b):
    return jnp.dot(a, b, preferred_element_type=jnp.float32)


def _dot_nt(a, b):
    return lax.dot_general(a, b, (((1,), (1,)), ((), ())), preferred_element_type=jnp.float32)


def _rope_slab(x, c, s_next, s_prev):
    return (x * c + pltpu.roll(x, LANES - ROT_DIM // 2, axis=1) * s_next
            + pltpu.roll(x, ROT_DIM // 2, axis=1) * s_prev)


def _in_proj_kernel(src_ref, x_ref, wa_ref, wb_ref, ba_ref, bb_ref, c_ref, sn_ref, sp_ref, o_ref,
                    wbf_ref):
    j = pl.program_id(0)
    i = pl.program_id(1)

    @pl.when(i == 0)
    def _():
        wbf_ref[:, :PROJ_UNIT] = _bf16(wa_ref[...])
        wbf_ref[:, PROJ_UNIT:] = _bf16(wb_ref[...])

    def project():
        bias = jnp.concatenate([ba_ref[...], bb_ref[...]], axis=1)
        return _dot(x_ref[...], wbf_ref[...]) + bias

    @pl.when(j < _Q_TILES)
    def _():
        acc = project()
        c, sn, sp = c_ref[...], sn_ref[...], sp_ref[...]
        scale = HEAD_DIM ** -0.5
        for t in range(PROJ_TN // LANES):
            sl = slice(t * LANES, (t + 1) * LANES)
            o_ref[:, sl] = _bf16(_rope_slab(acc[:, sl], c, sn, sp) * scale)

    @pl.when((j >= _Q_TILES) & (j < _GATE_END))
    def _():
        o_ref[...] = _bf16(jax.nn.sigmoid(project()))

    @pl.when(j >= _GATE_END)
    def _():
        o_ref[...] = _bf16(jax.nn.gelu(project()))


def _kv_proj_kernel(x_ref, w_ref, b_ref, c_ref, sn_ref, sp_ref, o_ref, xb_ref, wbf_ref):
    @pl.when(pl.program_id(0) == 0)
    def _():
        wbf_ref[...] = _bf16(w_ref[...])

    xb = _bf16(x_ref[...])
    xb_ref[...] = xb
    acc = _dot(xb, wbf_ref[...]) + b_ref[...]
    c, sn, sp = c_ref[...], sn_ref[...], sp_ref[...]
    for t in range(2 * KV_W // LANES):
        sl = slice(t * LANES, (t + 1) * LANES)
        if t < KV_W // LANES:
            o_ref[:, sl] = _bf16(_rope_slab(acc[:, sl], c, sn, sp))
        else:
            o_ref[:, sl] = _bf16(acc[:, sl])


def _rope_tables():
    half = ROT_DIM // 2
    inv_freq = ROPE_THETA ** (-np.arange(0, ROT_DIM, 2, dtype=np.float32) / ROT_DIM)
    pos = np.arange(SEQ, dtype=np.float32)
    ang = jnp.asarray(pos[:, None] * inv_freq[None, :].astype(np.float32), jnp.float32)
    cos, sin = jnp.cos(ang), jnp.sin(ang)
    ones = jnp.ones((SEQ, HEAD_DIM - ROT_DIM), jnp.float32)
    zeros = jnp.zeros((SEQ, HEAD_DIM - ROT_DIM), jnp.float32)
    zh = jnp.zeros((SEQ, half), jnp.float32)
    c = jnp.concatenate([cos, cos, ones], axis=1)
    s_next = jnp.concatenate([-sin, zh, zeros], axis=1)
    s_prev = jnp.concatenate([zh, sin, zeros], axis=1)
    rep = LANES // HEAD_DIM
    return tuple(jnp.tile(t, (1, rep)) for t in (c, s_next, s_prev))


def _in_proj(x_bf, w_in, b_in):
    c, sn, sp = _rope_tables()
    n_i = N_TOK // PROJ_TM
    pos_tiles = SEQ // PROJ_TM
    tbl = pl.BlockSpec((PROJ_TM, LANES), lambda j, i, src: (i % pos_tiles, 0))
    grid_spec = pltpu.PrefetchScalarGridSpec(
        num_scalar_prefetch=1,
        grid=(_N_PROJ_TILES, n_i),
        in_specs=[
            pl.BlockSpec((PROJ_TM, D_MODEL), lambda j, i, src: (i, 0)),
            pl.BlockSpec((D_MODEL, PROJ_UNIT), lambda j, i, src: (0, src[2 * j])),
            pl.BlockSpec((D_MODEL, PROJ_UNIT), lambda j, i, src: (0, src[2 * j + 1])),
            pl.BlockSpec((1, PROJ_UNIT), lambda j, i, src: (0, src[2 * j])),
            pl.BlockSpec((1, PROJ_UNIT), lambda j, i, src: (0, src[2 * j + 1])),
            tbl, tbl, tbl,
        ],
        out_specs=pl.BlockSpec((PROJ_TM, PROJ_TN), lambda j, i, src: (i, j)),
        scratch_shapes=[pltpu.VMEM((D_MODEL, PROJ_TN), jnp.bfloat16)],
    )
    return pl.pallas_call(
        _in_proj_kernel,
        grid_spec=grid_spec,
        out_shape=jax.ShapeDtypeStruct((N_TOK, H_W), jnp.bfloat16),
        compiler_params=pltpu.CompilerParams(
            dimension_semantics=("arbitrary", "arbitrary"),
            vmem_limit_bytes=48 * 1024 * 1024),
        name="in_proj",
    )(jnp.asarray(_SRC_UNIT), x_bf, w_in, w_in, b_in, b_in, c, sn, sp)


def _kv_proj(x2d, w_in, b_in):
    c, sn, sp = _rope_tables()
    tm = PROJ_TM // 2
    pos_tiles = SEQ // tm
    tbl = pl.BlockSpec((tm, LANES), lambda i: (i % pos_tiles, 0))
    return pl.pallas_call(
        _kv_proj_kernel,
        grid=(N_TOK // tm,),
        in_specs=[
            pl.BlockSpec((tm, D_MODEL), lambda i: (i, 0)),
            pl.BlockSpec((D_MODEL, 2 * KV_W), lambda i: (0, _SRC_UNIT_KV)),
            pl.BlockSpec((1, 2 * KV_W), lambda i: (0, _SRC_UNIT_KV)),
            tbl, tbl, tbl,
        ],
        out_specs=[pl.BlockSpec((tm, 2 * KV_W), lambda i: (i, 0)),
                   pl.BlockSpec((tm, D_MODEL), lambda i: (i, 0))],
        out_shape=[jax.ShapeDtypeStruct((N_TOK, 2 * KV_W), jnp.bfloat16),
                   jax.ShapeDtypeStruct((N_TOK, D_MODEL), jnp.bfloat16)],
        scratch_shapes=[pltpu.VMEM((D_MODEL, 2 * KV_W), jnp.bfloat16)],
        compiler_params=pltpu.CompilerParams(
            dimension_semantics=("arbitrary",), vmem_limit_bytes=32 * 1024 * 1024),
        name="kv_proj",
    )(x2d, w_in, b_in, c, sn, sp)


def _mixers_kernel(sink_ref, q_ref, kvc_ref, kvp_ref, u_ref, vg_ref, lng_ref, lnb_ref,
                   ws_ref, bs_ref, attn_ref, sgu_ref):
    n = pl.program_id(0)
    w = WINDOW
    first_key = jnp.where((n % (SEQ // w)) == 0, w, 0)
    pairs = GQA // 2
    rows = pairs * w

    qi = lax.broadcasted_iota(jnp.int32, (w, 2 * w), 0)
    kj = lax.broadcasted_iota(jnp.int32, (w, 2 * w), 1)
    valid = (kj > qi) & (kj <= qi + w) & (kj >= first_key)
    valid = jnp.concatenate([valid] * pairs, axis=0)
    lane = lax.broadcasted_iota(jnp.int32, (2 * w, LANES), 1)
    low = lane < HEAD_DIM
    lane_r = lax.broadcasted_iota(jnp.int32, (rows, LANES), 1)
    low_r = lane_r < HEAD_DIM
    ones_low = jnp.where(low, 1.0, 0.0).astype(jnp.bfloat16)
    ones_high = jnp.where(low, 0.0, 1.0).astype(jnp.bfloat16)
    neg_inf = jnp.float32(-jnp.inf)

    kv = jnp.concatenate([kvp_ref[...], kvc_ref[...]], axis=0).astype(jnp.float32)

    def padded(group, head_is_high):
        rolled = pltpu.roll(group, HEAD_DIM, axis=1)
        if head_is_high:
            lo_half, hi_half = rolled, group
        else:
            lo_half, hi_half = group, rolled
        return (_bf16(jnp.where(low, lo_half, 0.0)), _bf16(jnp.where(low, 0.0, hi_half)))

    for h in range(N_KV_HEADS):
        g0 = (h // 2) * LANES
        k_lo, k_hi = padded(kv[:, g0:g0 + LANES], h % 2 == 1)
        v_lo, v_hi = padded(kv[:, KV_W + g0:KV_W + g0 + LANES], h % 2 == 1)
        r_even = jnp.concatenate([v_lo, ones_low], axis=1)
        r_odd = jnp.concatenate([v_hi, ones_high], axis=1)
        q4 = jnp.concatenate(
            [q_ref[:, (h * pairs + p) * LANES:(h * pairs + p + 1) * LANES] for p in range(pairs)],
            axis=0)
        sink_e = jnp.concatenate(
            [jnp.full((w, 1), sink_ref[h * GQA + 2 * p], jnp.float32) for p in range(pairs)], axis=0)
        sink_o = jnp.concatenate(
            [jnp.full((w, 1), sink_ref[h * GQA + 2 * p + 1], jnp.float32) for p in range(pairs)], axis=0)

        s_e = jnp.where(valid, _dot_nt(q4, k_lo), neg_inf)
        s_o = jnp.where(valid, _dot_nt(q4, k_hi), neg_inf)
        m_e = jnp.maximum(jnp.max(s_e, axis=1, keepdims=True), sink_e)
        m_o = jnp.maximum(jnp.max(s_o, axis=1, keepdims=True), sink_o)
        p_e = _bf16(jnp.exp(s_e - m_e))
        p_o = _bf16(jnp.exp(s_o - m_o))
        acc = _dot(p_e, r_even) + _dot(p_o, r_odd)
        sink_term = jnp.exp(jnp.where(low_r, sink_e - m_e, sink_o - m_o))
        out = acc[:, :LANES] / (acc[:, LANES:] + sink_term)
        for p in range(pairs):
            c0 = (h * pairs + p) * LANES
            attn_ref[:, c0:c0 + LANES] = _bf16(out[p * w:(p + 1) * w])

    ti = lax.broadcasted_iota(jnp.int32, (w, w), 0)
    si = lax.broadcasted_iota(jnp.int32, (w, w), 1)
    causal = si <= ti
    for g in range(SGU_GROUPS):
        sl = slice(g * SGU_CH, (g + 1) * SGU_CH)
        x = vg_ref[:, sl].astype(jnp.float32)
        mu = jnp.mean(x, axis=-1, keepdims=True)
        xc = x - mu
        var = jnp.mean(xc * xc, axis=-1, keepdims=True)
        vn = xc * lax.rsqrt(var + LN_EPS) * lng_ref[:, sl] + lnb_ref[:, sl]
        wsg = _bf16(jnp.where(causal, ws_ref[g], 0.0))
        sv = _dot(wsg, _bf16(vn)) + bs_ref[g]
        sgu_ref[:, sl] = _bf16(u_ref[:, sl].astype(jnp.float32) * sv)


def _mixers(h, hkv, sinks, ln_g, ln_b, w_s, b_s):
    w = WINDOW
    nb = N_TOK // w
    grid_spec = pltpu.PrefetchScalarGridSpec(
        num_scalar_prefetch=0,
        grid=(nb,),
        in_specs=[
            pl.BlockSpec(memory_space=pltpu.SMEM),
            pl.BlockSpec((w, ATTN_W), lambda n: (n, H_Q // ATTN_W)),
            pl.BlockSpec((w, 2 * KV_W), lambda n: (n, 0)),
            pl.BlockSpec((w, 2 * KV_W), lambda n: (jnp.maximum(n - 1, 0), 0)),
            pl.BlockSpec((w, SGU_W), lambda n: (n, H_U // SGU_W)),
            pl.BlockSpec((w, SGU_W), lambda n: (n, H_VG // SGU_W)),
            pl.BlockSpec((1, SGU_W), lambda n: (0, 0)),
            pl.BlockSpec((1, SGU_W), lambda n: (0, 0)),
            pl.BlockSpec((SGU_GROUPS, w, w), lambda n: (0, 0, 0)),
            pl.BlockSpec((SGU_GROUPS, w, 1), lambda n: (0, 0, 0)),
        ],
        out_specs=[
            pl.BlockSpec((w, ATTN_W), lambda n: (n, 0)),
            pl.BlockSpec((w, SGU_W), lambda n: (n, 0)),
        ],
    )
    return pl.pallas_call(
        _mixers_kernel,
        grid_spec=grid_spec,
        out_shape=[jax.ShapeDtypeStruct((N_TOK, ATTN_W), jnp.bfloat16),
                   jax.ShapeDtypeStruct((N_TOK, SGU_W), jnp.bfloat16)],
        compiler_params=pltpu.CompilerParams(
            dimension_semantics=("arbitrary",), vmem_limit_bytes=32 * 1024 * 1024),
        name="mixers",
    )(sinks, h, hkv, hkv, h, h, ln_g.reshape(1, SGU_W), ln_b.reshape(1, SGU_W), w_s,
      b_s.reshape(SGU_GROUPS, w, 1))


def _layer_norm(z, g, b):
    mu = jnp.mean(z, axis=-1, keepdims=True)
    zc = z - mu
    var = jnp.mean(zc * zc, axis=-1, keepdims=True)
    return zc * lax.rsqrt(var + LN_EPS) * g + b


def _merge_kernel(attn_ref, sgu_ref, ga_ref, gb_ref, x_ref, wa_ref, wb_ref, wo_ref, g_ref, b_ref,
                  wr_ref, x1_ref, lg_ref):
    mix = (ga_ref[...].astype(jnp.float32) * _dot(attn_ref[...], wa_ref[...])
           + gb_ref[...].astype(jnp.float32) * _dot(sgu_ref[...], wb_ref[...]))
    z = ALPHA * x_ref[...] + _dot(_bf16(mix), wo_ref[...])
    x1 = _layer_norm(z, g_ref[...], b_ref[...])
    x1_ref[...] = x1
    x_hi = _bf16(x1)
    x_lo = _bf16(x1 - x_hi.astype(jnp.float32))
    parts = _dot(x_hi, wr_ref[...]) + _dot(x_lo, wr_ref[...])
    lg_ref[...] = parts + pltpu.roll(parts, N_EXPERTS, axis=1)


def _router_parts(w_router):
    hi = _bf16(w_router)
    lo = _bf16(w_router - hi.astype(jnp.float32))
    return jnp.concatenate([hi, lo], axis=1)


def _merge(attn, sgu, h, x2d, wa, wb, wo, g, b, wr):
    tm = MERGE_TM
    resident = pl.Buffered(1)
    grid_spec = pltpu.PrefetchScalarGridSpec(
        num_scalar_prefetch=0,
        grid=(N_TOK // tm,),
        in_specs=[
            pl.BlockSpec((tm, ATTN_W), lambda i: (i, 0)),
            pl.BlockSpec((tm, SGU_W), lambda i: (i, 0)),
            pl.BlockSpec((tm, D_MODEL), lambda i: (i, H_GA // D_MODEL)),
            pl.BlockSpec((tm, D_MODEL), lambda i: (i, H_GB // D_MODEL)),
            pl.BlockSpec((tm, D_MODEL), lambda i: (i, 0)),
            pl.BlockSpec((ATTN_W, D_MODEL), lambda i: (0, 0), pipeline_mode=resident),
            pl.BlockSpec((SGU_W, D_MODEL), lambda i: (0, 0), pipeline_mode=resident),
            pl.BlockSpec((D_MODEL, D_MODEL), lambda i: (0, 0), pipeline_mode=resident),
            pl.BlockSpec((1, D_MODEL), lambda i: (0, 0)),
            pl.BlockSpec((1, D_MODEL), lambda i: (0, 0)),
            pl.BlockSpec((D_MODEL, 2 * N_EXPERTS), lambda i: (0, 0)),
        ],
        out_specs=[
            pl.BlockSpec((tm, D_MODEL), lambda i: (i, 0)),
            pl.BlockSpec((tm, 2 * N_EXPERTS), lambda i: (i, 0)),
        ],
    )
    x1, lg = pl.pallas_call(
        _merge_kernel,
        grid_spec=grid_spec,
        out_shape=[jax.ShapeDtypeStruct((N_TOK, D_MODEL), jnp.float32),
                   jax.ShapeDtypeStruct((N_TOK, 2 * N_EXPERTS), jnp.float32)],
        compiler_params=pltpu.CompilerParams(
            dimension_semantics=("arbitrary",), vmem_limit_bytes=VMEM_LIMIT_CAP),
        name="merge",
    )(attn, sgu, h, h, x2d, wa, wb, wo, g, b, wr)
    return x1, lg[:, :N_EXPERTS].T


def _first_argmax(v, rows):
    m = jnp.max(v, axis=0, keepdims=True)
    i = jnp.min(jnp.where(v == m, rows, float(v.shape[0])), axis=0, keepdims=True)
    return m, i


def _row_index(shape):
    return lax.broadcasted_iota(jnp.int32, shape, 0).astype(jnp.float32)


def _route_kernel(lg_ref, bias_ref, idx_ref, w_ref):
    tt = lg_ref.shape[1]
    neg_inf = jnp.float32(-jnp.inf)
    scores = jax.nn.sigmoid(lg_ref[...])
    biased = scores + bias_ref[...]
    row_g = _row_index((GROUP_SIZE, tt))
    gs = []
    for g in range(N_EXPERT_GROUPS):
        blk = biased[g * GROUP_SIZE:(g + 1) * GROUP_SIZE]
        m1, i1 = _first_argmax(blk, row_g)
        m2 = jnp.max(jnp.where(row_g == i1, neg_inf, blk), axis=0, keepdims=True)
        gs.append(m1 + m2)
    cur = jnp.concatenate(gs, axis=0)
    row_n = _row_index((N_EXPERT_GROUPS, tt))
    sel = jnp.zeros((N_EXPERT_GROUPS, tt), jnp.float32)
    for _ in range(TOPK_GROUPS):
        _, i = _first_argmax(cur, row_n)
        hit = row_n == i
        sel = jnp.where(hit, 1.0, sel)
        cur = jnp.where(hit, neg_inf, cur)
    emask = jnp.concatenate(
        [jnp.broadcast_to(sel[g:g + 1], (GROUP_SIZE, tt)) for g in range(N_EXPERT_GROUPS)], axis=0)
    masked = jnp.where(emask > 0.5, biased, neg_inf)
    row_e = _row_index((N_EXPERTS, tt))
    idx_rows, w_rows = [], []
    for _ in range(TOP_K):
        _, i = _first_argmax(masked, row_e)
        hit = row_e == i
        w_rows.append(jnp.sum(jnp.where(hit, scores, 0.0), axis=0, keepdims=True))
        idx_rows.append(i)
        masked = jnp.where(hit, neg_inf, masked)
    wsel = jnp.concatenate(w_rows, axis=0)
    idx_ref[...] = jnp.concatenate(idx_rows, axis=0).astype(jnp.int32)
    w_ref[...] = wsel / (jnp.sum(wsel, axis=0, keepdims=True) + 1e-20) * ROUTED_SCALE


def _route(logits_t, bias):
    tt = ROUTE_TT
    return pl.pallas_call(
        _route_kernel,
        grid=(N_TOK // tt,),
        in_specs=[pl.BlockSpec((N_EXPERTS, tt), lambda i: (0, i)),
                  pl.BlockSpec((N_EXPERTS, 1), lambda i: (0, 0))],
        out_specs=[pl.BlockSpec((TOP_K, tt), lambda i: (0, i)),
                   pl.BlockSpec((TOP_K, tt), lambda i: (0, i))],
        out_shape=[jax.ShapeDtypeStruct((TOP_K, N_TOK), jnp.int32),
                   jax.ShapeDtypeStruct((TOP_K, N_TOK), jnp.float32)],
        compiler_params=pltpu.CompilerParams(dimension_semantics=("arbitrary",)),
        name="route",
    )(logits_t, bias.reshape(N_EXPERTS, 1))


def _dispatch_plan(idx_t):
    bm = EXPERT_BM
    nb = N_EXPERT_BLOCKS
    flat_e = idx_t.reshape(-1)
    counts = jnp.sum((flat_e[:, None] == jnp.arange(N_EXPERTS, dtype=jnp.int32)[None, :])
                     .astype(jnp.int32), axis=0)
    e_ids = jnp.arange(N_EXPERTS, dtype=jnp.int32)
    upto = (e_ids[:, None] <= e_ids[None, :]).astype(jnp.int32)
    padded = (counts + bm - 1) // bm * bm
    pad_end = jnp.sum(padded[:, None] * upto, axis=0)
    fill_end = jnp.sum((padded - counts)[:, None] * upto, axis=0)
    n_fill = N_SLOTS - N_ASSIGN
    fill_key = jnp.sum((jnp.arange(n_fill, dtype=jnp.int32)[:, None] >= fill_end[None, :])
                       .astype(jnp.int32), axis=1)
    keys = jnp.concatenate([flat_e, fill_key])
    pos_bits = (N_SLOTS - 1).bit_length()
    slot = jnp.arange(N_SLOTS, dtype=jnp.int32)
    src = jnp.sort((keys << pos_bits) | slot) & ((1 << pos_bits) - 1)
    real = src < N_ASSIGN
    dst = jnp.where(real, src, N_ASSIGN + (slot & (2 * bm - 1)))
    n_valid = jnp.sum(real.reshape(nb, bm).astype(jnp.int32), axis=1)
    used = n_valid > 0
    block_start = jnp.arange(nb, dtype=jnp.int32) * bm
    block_e = jnp.sum((block_start[:, None] >= pad_end[None, :]).astype(jnp.int32), axis=1)
    last_e = jnp.max(jnp.where(used, block_e, 0))
    block_e = jnp.where(used, jnp.minimum(block_e, N_EXPERTS - 1), last_e).astype(jnp.int32)
    prev_e = jnp.concatenate([jnp.full((1,), -1, jnp.int32), block_e[:-1]])
    fresh = (block_e != prev_e).astype(jnp.int32)
    later = (counts > 0)[None, :] & (e_ids[None, :] > block_e[:, None])
    next_e = jnp.min(jnp.where(later, e_ids[None, :], N_EXPERTS), axis=1).astype(jnp.int32)
    return dst.reshape(nb, 1, bm), block_e, n_valid, fresh, next_e


def _experts_kernel(be_ref, nval_ref, fresh_ref, nxt_ref, dst_ref, dstn_ref, x_hbm, w1_hbm, w3_hbm,
                    w2_hbm, y_hbm, xbuf, obuf, w1s, w3s, w2s, w1b, w3b, w2b, gsem, ssem, wsem):
    b = pl.program_id(0)
    nb = pl.num_programs(0)
    bm, grp = EXPERT_BM, EXPERT_GROUP
    slot = b % 2
    n_cur = nval_ref[b]
    n_next = jnp.where(b + 1 < nb, nval_ref[jnp.minimum(b + 1, nb - 1)], 0)
    n_prev = jnp.where(b >= 1, nval_ref[jnp.maximum(b - 1, 0)], 0)
    n_prev2 = jnp.where(b >= 2, nval_ref[jnp.maximum(b - 2, 0)], 0)

    def for_groups(n_rows, fn):
        for g in range(bm // grp):
            pl.when(g * grp < n_rows)(functools.partial(fn, g))

    def gather_copy(idx_ref, s, r):
        tok = idx_ref[0, 0, r] & (N_TOK - 1)
        return pltpu.make_async_copy(x_hbm.at[pl.ds(tok, 1)], xbuf.at[s, pl.ds(r, 1)], gsem.at[s])

    def scatter_copy(s, r):
        row = dst_ref[0, 0, r]
        return pltpu.make_async_copy(obuf.at[s, pl.ds(r, 1)], y_hbm.at[pl.ds(row, 1)], ssem.at[s])

    def start_gather(idx_ref, s, n_rows):
        def group(g):
            for r in range(g * grp, (g + 1) * grp):
                gather_copy(idx_ref, s, r).start()
        for_groups(n_rows, group)

    def wait_gather(s, n_rows):
        def group(g):
            rows = pl.ds(g * grp, grp)
            pltpu.make_async_copy(x_hbm.at[rows], xbuf.at[s, rows], gsem.at[s]).wait()
        for_groups(n_rows, group)

    def start_scatter(s, n_rows):
        def group(g):
            for r in range(g * grp, (g + 1) * grp):
                scatter_copy(s, r).start()
        for_groups(n_rows, group)

    def wait_scatter(s, n_rows):
        def group(g):
            rows = pl.ds(g * grp, grp)
            pltpu.make_async_copy(obuf.at[s, rows], y_hbm.at[rows], ssem.at[s]).wait()
        for_groups(n_rows, group)

    def weight_copies(e):
        return (pltpu.make_async_copy(w1_hbm.at[e], w1s, wsem.at[0]),
                pltpu.make_async_copy(w3_hbm.at[e], w3s, wsem.at[1]),
                pltpu.make_async_copy(w2_hbm.at[e], w2s, wsem.at[2]))

    def swiglu_block(w1v, w3v, w2v):
        xb = _bf16(xbuf[slot])
        a = _bf16(jax.nn.silu(_dot(xb, w1v)) * _dot(xb, w3v))
        obuf[slot] = _dot(a, w2v)

    @pl.when(b == 0)
    def _():
        for c in weight_copies(be_ref[0]):
            c.start(priority=1)
        xbuf[...] = jnp.zeros(xbuf.shape, xbuf.dtype)
        obuf[1] = jnp.zeros(obuf.shape[1:], obuf.dtype)
        for region in range(2):
            fill = pltpu.make_async_copy(
                obuf.at[1], y_hbm.at[pl.ds(N_ASSIGN + region * bm, bm)], ssem.at[1])
            fill.start()
            fill.wait()
        start_gather(dst_ref, 0, n_cur)

    start_gather(dstn_ref, 1 - slot, n_next)
    wait_scatter(slot, n_prev2)

    @pl.when(n_cur > 0)
    def _():
        wait_gather(slot, n_cur)

        @pl.when(fresh_ref[b] == 1)
        def _():
            for c in weight_copies(be_ref[b]):
                c.wait()
            w1v, w3v, w2v = _bf16(w1s[...]), _bf16(w3s[...]), _bf16(w2s[...])
            w1b[...] = w1v
            w3b[...] = w3v
            w2b[...] = w2v
            swiglu_block(w1v, w3v, w2v)

            @pl.when(nxt_ref[b] < N_EXPERTS)
            def _():
                for c in weight_copies(nxt_ref[b]):
                    c.start(priority=1)

        @pl.when(fresh_ref[b] == 0)
        def _():
            swiglu_block(w1b[...], w3b[...], w2b[...])

        start_scatter(slot, n_cur)

    @pl.when(b == nb - 1)
    def _():
        wait_scatter(1 - slot, n_prev)
        wait_scatter(slot, n_cur)


def _experts(x1, w1, w3, w2, dst, block_e, n_valid, fresh, next_e):
    bm = EXPERT_BM
    nb = N_EXPERT_BLOCKS
    smem_blk = lambda f: pl.BlockSpec((1, 1, bm), f, memory_space=pltpu.SMEM)
    hbm = pl.BlockSpec(memory_space=pl.ANY)
    grid_spec = pltpu.PrefetchScalarGridSpec(
        num_scalar_prefetch=4,
        grid=(nb,),
        in_specs=[
            smem_blk(lambda b, *_: (b, 0, 0)),
            smem_blk(lambda b, *_: (jnp.minimum(b + 1, nb - 1), 0, 0)),
            hbm, hbm, hbm, hbm,
        ],
        out_specs=hbm,
        scratch_shapes=[
            pltpu.VMEM((2, bm, D_MODEL), jnp.float32),
            pltpu.VMEM((2, bm, D_MODEL), jnp.float32),
            pltpu.VMEM((D_MODEL, D_EXPERT), jnp.float32),
            pltpu.VMEM((D_MODEL, D_EXPERT), jnp.float32),
            pltpu.VMEM((D_EXPERT, D_MODEL), jnp.float32),
            pltpu.VMEM((D_MODEL, D_EXPERT), jnp.bfloat16),
            pltpu.VMEM((D_MODEL, D_EXPERT), jnp.bfloat16),
            pltpu.VMEM((D_EXPERT, D_MODEL), jnp.bfloat16),
            pltpu.SemaphoreType.DMA((2,)),
            pltpu.SemaphoreType.DMA((2,)),
            pltpu.SemaphoreType.DMA((3,)),
        ],
    )
    return pl.pallas_call(
        _experts_kernel,
        grid_spec=grid_spec,
        out_shape=jax.ShapeDtypeStruct((N_ASSIGN + 2 * bm, D_MODEL), jnp.float32),
        compiler_params=pltpu.CompilerParams(
            dimension_semantics=("arbitrary",), vmem_limit_bytes=40 * 1024 * 1024),
        name="experts",
    )(block_e, n_valid, fresh, next_e, dst, dst, x1, w1, w3, w2)


def _combine_kernel(*refs):
    y_refs = refs[:TOP_K]
    w_ref, x_ref, s1_ref, s3_ref, s2_ref, g_ref, b_ref, o_ref = refs[TOP_K:]
    x1 = x_ref[...]
    xb = _bf16(x1)
    a = _bf16(jax.nn.silu(_dot(xb, s1_ref[...])) * _dot(xb, s3_ref[...]))
    ffn = _dot(a, s2_ref[...])
    wts = w_ref[...]
    routed = wts[:, 0:1] * y_refs[0][...]
    for k in range(1, TOP_K):
        routed = routed + wts[:, k:k + 1] * y_refs[k][...]
    o_ref[...] = _layer_norm(ALPHA * x1 + (routed + ffn), g_ref[...], b_ref[...])


def _combine(y, wts, x1, s1, s3, s2, g, b):
    tm = COMBINE_TM
    nt = N_TOK // tm
    y_specs = [pl.BlockSpec((tm, D_MODEL), functools.partial(lambda i, k: (k * nt + i, 0), k=k))
               for k in range(TOP_K)]
    return pl.pallas_call(
        _combine_kernel,
        grid=(nt,),
        in_specs=y_specs + [
            pl.BlockSpec((tm, TOP_K), lambda i: (i, 0)),
            pl.BlockSpec((tm, D_MODEL), lambda i: (i, 0)),
            pl.BlockSpec((D_MODEL, D_SHARED), lambda i: (0, 0)),
            pl.BlockSpec((D_MODEL, D_SHARED), lambda i: (0, 0)),
            pl.BlockSpec((D_SHARED, D_MODEL), lambda i: (0, 0)),
            pl.BlockSpec((1, D_MODEL), lambda i: (0, 0)),
            pl.BlockSpec((1, D_MODEL), lambda i: (0, 0)),
        ],
        out_specs=pl.BlockSpec((tm, D_MODEL), lambda i: (i, 0)),
        out_shape=jax.ShapeDtypeStruct((N_TOK, D_MODEL), jnp.float32),
        compiler_params=pltpu.CompilerParams(
            dimension_semantics=("arbitrary",), vmem_limit_bytes=VMEM_LIMIT_CAP),
        name="combine",
    )(*([y] * TOP_K), wts, x1, s1, s3, s2, g, b)


def kernel(x, w_in, b_in, sinks, sgu_ln_g, sgu_ln_b, w_spatial, b_spatial, w_branch_attn,
           w_branch_sgu, w_out, ln1_g, ln1_b, w_router, router_bias, w1, w3, w2, ws1, ws3, ws2,
           ln2_g, ln2_b):
    assert x.shape == (BATCH, SEQ, D_MODEL) and w_in.shape == (1, D_MODEL, IN_W)
    x2d = x.reshape(N_TOK, D_MODEL)
    hkv, x_bf = _kv_proj(x2d, w_in[0], b_in)
    h = _in_proj(x_bf, w_in[0], b_in)
    attn, sgu = _mixers(h, hkv, sinks[0], sgu_ln_g[0], sgu_ln_b[0], w_spatial[0], b_spatial[0])
    x1, logits_t = _merge(attn, sgu, h, x2d, _bf16(w_branch_attn[0]), _bf16(w_branch_sgu[0]),
                          _bf16(w_out[0]), ln1_g, ln1_b, _router_parts(w_router[0]))
    idx_t, w_t = _route(logits_t, router_bias[0])
    y = _experts(x1, w1[0], w3[0], w2[0], *_dispatch_plan(idx_t))
    out = _combine(y, w_t.T, x1, _bf16(ws1[0]), _bf16(ws3[0]), _bf16(ws2[0]), ln2_g, ln2_b)
    return out.reshape(BATCH, SEQ, D_MODEL)
```

```python
import functools
import math

import numpy as np
import jax
import jax.numpy as jnp
from jax import lax
from jax.experimental import pallas as pl
from jax.experimental.pallas import tpu as pltpu

D_MODEL = 2048
BATCH = 2
SEQ = 4096
N_TOK = BATCH * SEQ
N_Q_HEADS = 32
N_KV_HEADS = 4
HEAD_DIM = 64
GQA = N_Q_HEADS // N_KV_HEADS
WINDOW = 128
ROPE_THETA = 500000.0
ROT_DIM = HEAD_DIM // 4
SGU_GROUPS = 8
SGU_CH = 128
N_EXPERTS = 64
N_EXPERT_GROUPS = 8
GROUP_SIZE = N_EXPERTS // N_EXPERT_GROUPS
TOPK_GROUPS = 4
TOP_K = 8
D_EXPERT = 512
D_SHARED = 512
ROUTED_SCALE = 2.5
ATTN_W = N_Q_HEADS * HEAD_DIM
KV_W = N_KV_HEADS * HEAD_DIM
SGU_W = SGU_GROUPS * SGU_CH
IN_W = ATTN_W + 2 * KV_W + 2 * SGU_W + 2 * D_MODEL
ALPHA = 2.0 ** 0.25
LN_EPS = 1e-5
N_ASSIGN = N_TOK * TOP_K

LANES = 128
V7X_VMEM_BYTES = 64 * 1024 * 1024


def _vmem_limit(*buffers):
    need = sum(copies * math.prod(shape) * jnp.dtype(dtype).itemsize
               for shape, dtype, copies in buffers)
    assert need <= V7X_VMEM_BYTES * 7 // 8, need
    return need

PROJ_TM = 1024
PROJ_TN = 1024
MERGE_TM = 256
ROUTE_TT = 512
EXPERT_BM = 256
N_GATHER_BUFS = 3
COMBINE_TM = 128
N_EXPERT_BLOCKS = -(-(N_ASSIGN + N_EXPERTS * (EXPERT_BM - 1)) // EXPERT_BM)
N_SLOTS = N_EXPERT_BLOCKS * EXPERT_BM

H_Q, H_GA, H_GB, H_U, H_VG = 0, 2048, 4096, 6144, 7168
H_W = 8192
PROJ_UNIT = 512
_SRC_UNIT = np.array([0, 1, 2, 3, 9, 10, 11, 12, 13, 14, 15, 16, 5, 6, 7, 8], np.int32)
_SRC_UNIT_KV = ATTN_W // PROJ_UNIT
_N_PROJ_TILES = H_W // PROJ_TN
_Q_TILES = ATTN_W // PROJ_TN
_GATE_END = H_U // PROJ_TN


def _bf16(a):
    return a.astype(jnp.bfloat16)


def _dot(a, b):
    return jnp.dot(a, b, preferred_element_type=jnp.float32)


def _dot_nt(a, b):
    return lax.dot_general(a, b, (((1,), (1,)), ((), ())), preferred_element_type=jnp.float32)


def _rope_slab(x, c, s_next, s_prev):
    return (x * c + pltpu.roll(x, LANES - ROT_DIM // 2, axis=1) * s_next
            + pltpu.roll(x, ROT_DIM // 2, axis=1) * s_prev)


def _in_proj_kernel(src_ref, x_ref, wa_ref, wb_ref, ba_ref, bb_ref, c_ref, sn_ref, sp_ref, o_ref,
                    wbf_ref):
    j = pl.program_id(0)
    i = pl.program_id(1)

    @pl.when(i == 0)
    def _():
        wbf_ref[:, :PROJ_UNIT] = _bf16(wa_ref[...])
        wbf_ref[:, PROJ_UNIT:] = _bf16(wb_ref[...])

    def project():
        bias = jnp.concatenate([ba_ref[...], bb_ref[...]], axis=1)
        return _dot(x_ref[...], wbf_ref[...]) + bias

    @pl.when(j < _Q_TILES)
    def _():
        acc = project()
        c, sn, sp = c_ref[...], sn_ref[...], sp_ref[...]
        scale = HEAD_DIM ** -0.5
        for t in range(PROJ_TN // LANES):
            sl = slice(t * LANES, (t + 1) * LANES)
            o_ref[:, sl] = _bf16(_rope_slab(acc[:, sl], c, sn, sp) * scale)

    @pl.when((j >= _Q_TILES) & (j < _GATE_END))
    def _():
        o_ref[...] = _bf16(jax.nn.sigmoid(project()))

    @pl.when(j >= _GATE_END)
    def _():
        o_ref[...] = _bf16(jax.nn.gelu(project()))


def _kv_proj_kernel(x_ref, w_ref, b_ref, c_ref, sn_ref, sp_ref, o_ref, xb_ref, wbf_ref):
    @pl.when(pl.program_id(0) == 0)
    def _():
        wbf_ref[...] = _bf16(w_ref[...])

    xb = _bf16(x_ref[...])
    xb_ref[...] = xb
    acc = _dot(xb, wbf_ref[...]) + b_ref[...]
    c, sn, sp = c_ref[...], sn_ref[...], sp_ref[...]
    for t in range(2 * KV_W // LANES):
        sl = slice(t * LANES, (t + 1) * LANES)
        if t < KV_W // LANES:
            o_ref[:, sl] = _bf16(_rope_slab(acc[:, sl], c, sn, sp))
        else:
            o_ref[:, sl] = _bf16(acc[:, sl])


def _rope_tables():
    half = ROT_DIM // 2
    inv_freq = ROPE_THETA ** (-np.arange(0, ROT_DIM, 2, dtype=np.float32) / ROT_DIM)
    pos = np.arange(SEQ, dtype=np.float32)
    ang = jnp.asarray(pos[:, None] * inv_freq[None, :].astype(np.float32), jnp.float32)
    cos, sin = jnp.cos(ang), jnp.sin(ang)
    ones = jnp.ones((SEQ, HEAD_DIM - ROT_DIM), jnp.float32)
    zeros = jnp.zeros((SEQ, HEAD_DIM - ROT_DIM), jnp.float32)
    zh = jnp.zeros((SEQ, half), jnp.float32)
    c = jnp.concatenate([cos, cos, ones], axis=1)
    s_next = jnp.concatenate([-sin, zh, zeros], axis=1)
    s_prev = jnp.concatenate([zh, sin, zeros], axis=1)
    rep = LANES // HEAD_DIM
    return tuple(jnp.tile(t, (1, rep)) for t in (c, s_next, s_prev))


def _in_proj(x_bf, w_in, b_in):
    c, sn, sp = _rope_tables()
    n_i = N_TOK // PROJ_TM
    pos_tiles = SEQ // PROJ_TM
    tbl = pl.BlockSpec((PROJ_TM, LANES), lambda j, i, src: (i % pos_tiles, 0))
    grid_spec = pltpu.PrefetchScalarGridSpec(
        num_scalar_prefetch=1,
        grid=(_N_PROJ_TILES, n_i),
        in_specs=[
            pl.BlockSpec((PROJ_TM, D_MODEL), lambda j, i, src: (i, 0)),
            pl.BlockSpec((D_MODEL, PROJ_UNIT), lambda j, i, src: (0, src[2 * j])),
            pl.BlockSpec((D_MODEL, PROJ_UNIT), lambda j, i, src: (0, src[2 * j + 1])),
            pl.BlockSpec((1, PROJ_UNIT), lambda j, i, src: (0, src[2 * j])),
            pl.BlockSpec((1, PROJ_UNIT), lambda j, i, src: (0, src[2 * j + 1])),
            tbl, tbl, tbl,
        ],
        out_specs=pl.BlockSpec((PROJ_TM, PROJ_TN), lambda j, i, src: (i, j)),
        scratch_shapes=[pltpu.VMEM((D_MODEL, PROJ_TN), jnp.bfloat16)],
    )
    return pl.pallas_call(
        _in_proj_kernel,
        grid_spec=grid_spec,
        out_shape=jax.ShapeDtypeStruct((N_TOK, H_W), jnp.bfloat16),
        compiler_params=pltpu.CompilerParams(
            dimension_semantics=("arbitrary", "arbitrary"),
            vmem_limit_bytes=_vmem_limit(
                ((PROJ_TM, D_MODEL), jnp.bfloat16, 2), ((D_MODEL, PROJ_TN), jnp.float32, 2),
                ((PROJ_TM, LANES), jnp.float32, 6), ((PROJ_TM, PROJ_TN), jnp.bfloat16, 2),
                ((D_MODEL, PROJ_TN), jnp.bfloat16, 1), ((PROJ_TM, PROJ_TN), jnp.float32, 3))),
        name="in_proj",
    )(jnp.asarray(_SRC_UNIT), x_bf, w_in, w_in, b_in, b_in, c, sn, sp)


def _kv_proj(x2d, w_in, b_in):
    c, sn, sp = _rope_tables()
    tm = PROJ_TM // 2
    pos_tiles = SEQ // tm
    tbl = pl.BlockSpec((tm, LANES), lambda i: (i % pos_tiles, 0))
    return pl.pallas_call(
        _kv_proj_kernel,
        grid=(N_TOK // tm,),
        in_specs=[
            pl.BlockSpec((tm, D_MODEL), lambda i: (i, 0)),
            pl.BlockSpec((D_MODEL, 2 * KV_W), lambda i: (0, _SRC_UNIT_KV)),
            pl.BlockSpec((1, 2 * KV_W), lambda i: (0, _SRC_UNIT_KV)),
            tbl, tbl, tbl,
        ],
        out_specs=[pl.BlockSpec((tm, 2 * KV_W), lambda i: (i, 0)),
                   pl.BlockSpec((tm, D_MODEL), lambda i: (i, 0))],
        out_shape=[jax.ShapeDtypeStruct((N_TOK, 2 * KV_W), jnp.bfloat16),
                   jax.ShapeDtypeStruct((N_TOK, D_MODEL), jnp.bfloat16)],
        scratch_shapes=[pltpu.VMEM((D_MODEL, 2 * KV_W), jnp.bfloat16)],
        compiler_params=pltpu.CompilerParams(
            dimension_semantics=("arbitrary",),
            vmem_limit_bytes=_vmem_limit(
                ((tm, D_MODEL), jnp.float32, 2), ((tm, D_MODEL), jnp.bfloat16, 3),
                ((D_MODEL, 2 * KV_W), jnp.float32, 2), ((D_MODEL, 2 * KV_W), jnp.bfloat16, 1),
                ((tm, LANES), jnp.float32, 6), ((tm, 2 * KV_W), jnp.float32, 3))),
        name="kv_proj",
    )(x2d, w_in, b_in, c, sn, sp)


def _mixers_kernel(sink_ref, q_ref, kvc_ref, kvp_ref, u_ref, vg_ref, lng_ref, lnb_ref,
                   ws_ref, bs_ref, attn_ref, sgu_ref):
    n = pl.program_id(0)
    w = WINDOW
    first_key = jnp.where((n % (SEQ // w)) == 0, w, 0)
    pairs = GQA // 2
    rows = pairs * w

    qi = lax.broadcasted_iota(jnp.int32, (w, 2 * w), 0)
    kj = lax.broadcasted_iota(jnp.int32, (w, 2 * w), 1)
    valid = (kj > qi) & (kj <= qi + w) & (kj >= first_key)
    valid = jnp.concatenate([valid] * pairs, axis=0)
    lane = lax.broadcasted_iota(jnp.int32, (2 * w, LANES), 1)
    low = lane < HEAD_DIM
    lane_r = lax.broadcasted_iota(jnp.int32, (rows, LANES), 1)
    low_r = lane_r < HEAD_DIM
    ones_low = jnp.where(low, 1.0, 0.0).astype(jnp.bfloat16)
    ones_high = jnp.where(low, 0.0, 1.0).astype(jnp.bfloat16)
    neg_inf = jnp.float32(-jnp.inf)

    kv = jnp.concatenate([kvp_ref[...], kvc_ref[...]], axis=0).astype(jnp.float32)

    def padded(group, head_is_high):
        rolled = pltpu.roll(group, HEAD_DIM, axis=1)
        if head_is_high:
            lo_half, hi_half = rolled, group
        else:
            lo_half, hi_half = group, rolled
        return (_bf16(jnp.where(low, lo_half, 0.0)), _bf16(jnp.where(low, 0.0, hi_half)))

    for h in range(N_KV_HEADS):
        g0 = (h // 2) * LANES
        k_lo, k_hi = padded(kv[:, g0:g0 + LANES], h % 2 == 1)
        v_lo, v_hi = padded(kv[:, KV_W + g0:KV_W + g0 + LANES], h % 2 == 1)
        r_even = jnp.concatenate([v_lo, ones_low], axis=1)
        r_odd = jnp.concatenate([v_hi, ones_high], axis=1)
        q4 = jnp.concatenate(
            [q_ref[:, (h * pairs + p) * LANES:(h * pairs + p + 1) * LANES] for p in range(pairs)],
            axis=0)
        sink_e = jnp.concatenate(
            [jnp.full((w, 1), sink_ref[h * GQA + 2 * p], jnp.float32) for p in range(pairs)], axis=0)
        sink_o = jnp.concatenate(
            [jnp.full((w, 1), sink_ref[h * GQA + 2 * p + 1], jnp.float32) for p in range(pairs)], axis=0)

        s_e = jnp.where(valid, _dot_nt(q4, k_lo), neg_inf)
        s_o = jnp.where(valid, _dot_nt(q4, k_hi), neg_inf)
        m_e = jnp.maximum(jnp.max(s_e, axis=1, keepdims=True), sink_e)
        m_o = jnp.maximum(jnp.max(s_o, axis=1, keepdims=True), sink_o)
        p_e = _bf16(jnp.exp(s_e - m_e))
        p_o = _bf16(jnp.exp(s_o - m_o))
        acc = _dot(p_e, r_even) + _dot(p_o, r_odd)
        sink_term = jnp.exp(jnp.where(low_r, sink_e - m_e, sink_o - m_o))
        out = acc[:, :LANES] / (acc[:, LANES:] + sink_term)
        for p in range(pairs):
            c0 = (h * pairs + p) * LANES
            attn_ref[:, c0:c0 + LANES] = _bf16(out[p * w:(p + 1) * w])

    ti = lax.broadcasted_iota(jnp.int32, (w, w), 0)
    si = lax.broadcasted_iota(jnp.int32, (w, w), 1)
    causal = si <= ti
    for g in range(SGU_GROUPS):
        sl = slice(g * SGU_CH, (g + 1) * SGU_CH)
        x = vg_ref[:, sl].astype(jnp.float32)
        mu = jnp.mean(x, axis=-1, keepdims=True)
        xc = x - mu
        var = jnp.mean(xc * xc, axis=-1, keepdims=True)
        vn = xc * lax.rsqrt(var + LN_EPS) * lng_ref[:, sl] + lnb_ref[:, sl]
        wsg = _bf16(jnp.where(causal, ws_ref[g], 0.0))
        sv = _dot(wsg, _bf16(vn)) + bs_ref[g]
        sgu_ref[:, sl] = _bf16(u_ref[:, sl].astype(jnp.float32) * sv)


def _mixers(h, hkv, sinks, ln_g, ln_b, w_s, b_s):
    w = WINDOW
    nb = N_TOK // w
    grid_spec = pltpu.PrefetchScalarGridSpec(
        num_scalar_prefetch=0,
        grid=(nb,),
        in_specs=[
            pl.BlockSpec(memory_space=pltpu.SMEM),
            pl.BlockSpec((w, ATTN_W), lambda n: (n, H_Q // ATTN_W)),
            pl.BlockSpec((w, 2 * KV_W), lambda n: (n, 0)),
            pl.BlockSpec((w, 2 * KV_W), lambda n: (jnp.maximum(n - 1, 0), 0)),
            pl.BlockSpec((w, SGU_W), lambda n: (n, H_U // SGU_W)),
            pl.BlockSpec((w, SGU_W), lambda n: (n, H_VG // SGU_W)),
            pl.BlockSpec((1, SGU_W), lambda n: (0, 0)),
            pl.BlockSpec((1, SGU_W), lambda n: (0, 0)),
            pl.BlockSpec((SGU_GROUPS, w, w), lambda n: (0, 0, 0)),
            pl.BlockSpec((SGU_GROUPS, w, 1), lambda n: (0, 0, 0)),
        ],
        out_specs=[
            pl.BlockSpec((w, ATTN_W), lambda n: (n, 0)),
            pl.BlockSpec((w, SGU_W), lambda n: (n, 0)),
        ],
    )
    return pl.pallas_call(
        _mixers_kernel,
        grid_spec=grid_spec,
        out_shape=[jax.ShapeDtypeStruct((N_TOK, ATTN_W), jnp.bfloat16),
                   jax.ShapeDtypeStruct((N_TOK, SGU_W), jnp.bfloat16)],
        compiler_params=pltpu.CompilerParams(
            dimension_semantics=("arbitrary",),
            vmem_limit_bytes=_vmem_limit(
                ((w, ATTN_W), jnp.bfloat16, 4), ((w, 2 * KV_W), jnp.bfloat16, 4),
                ((w, SGU_W), jnp.bfloat16, 6), ((SGU_GROUPS, w, w), jnp.float32, 2),
                ((SGU_GROUPS, w, LANES), jnp.float32, 2),
                ((GQA // 2 * w, 2 * w), jnp.float32, 16))),
        name="mixers",
    )(sinks, h, hkv, hkv, h, h, ln_g.reshape(1, SGU_W), ln_b.reshape(1, SGU_W), w_s,
      b_s.reshape(SGU_GROUPS, w, 1))


def _layer_norm(z, g, b):
    mu = jnp.mean(z, axis=-1, keepdims=True)
    zc = z - mu
    var = jnp.mean(zc * zc, axis=-1, keepdims=True)
    return zc * lax.rsqrt(var + LN_EPS) * g + b


def _merge_kernel(attn_ref, sgu_ref, ga_ref, gb_ref, x_ref, wa_ref, wb_ref, wo_ref, g_ref, b_ref,
                  wr_ref, x1_ref, lg_ref):
    mix = (ga_ref[...].astype(jnp.float32) * _dot(attn_ref[...], wa_ref[...])
           + gb_ref[...].astype(jnp.float32) * _dot(sgu_ref[...], wb_ref[...]))
    z = ALPHA * x_ref[...] + _dot(_bf16(mix), wo_ref[...])
    x1 = _layer_norm(z, g_ref[...], b_ref[...])
    x1_ref[...] = x1
    x_hi = _bf16(x1)
    x_lo = _bf16(x1 - x_hi.astype(jnp.float32))
    parts = _dot(x_hi, wr_ref[...]) + _dot(x_lo, wr_ref[...])
    lg_ref[...] = parts + pltpu.roll(parts, N_EXPERTS, axis=1)


def _router_parts(w_router):
    hi = _bf16(w_router)
    lo = _bf16(w_router - hi.astype(jnp.float32))
    return jnp.concatenate([hi, lo], axis=1)


def _merge(attn, sgu, h, x2d, wa, wb, wo, g, b, wr):
    tm = MERGE_TM
    resident = pl.Buffered(1)
    grid_spec = pltpu.PrefetchScalarGridSpec(
        num_scalar_prefetch=0,
        grid=(N_TOK // tm,),
        in_specs=[
            pl.BlockSpec((tm, ATTN_W), lambda i: (i, 0)),
            pl.BlockSpec((tm, SGU_W), lambda i: (i, 0)),
            pl.BlockSpec((tm, D_MODEL), lambda i: (i, H_GA // D_MODEL)),
            pl.BlockSpec((tm, D_MODEL), lambda i: (i, H_GB // D_MODEL)),
            pl.BlockSpec((tm, D_MODEL), lambda i: (i, 0)),
            pl.BlockSpec((ATTN_W, D_MODEL), lambda i: (0, 0), pipeline_mode=resident),
            pl.BlockSpec((SGU_W, D_MODEL), lambda i: (0, 0), pipeline_mode=resident),
            pl.BlockSpec((D_MODEL, D_MODEL), lambda i: (0, 0), pipeline_mode=resident),
            pl.BlockSpec((1, D_MODEL), lambda i: (0, 0)),
            pl.BlockSpec((1, D_MODEL), lambda i: (0, 0)),
            pl.BlockSpec((D_MODEL, 2 * N_EXPERTS), lambda i: (0, 0)),
        ],
        out_specs=[
            pl.BlockSpec((tm, D_MODEL), lambda i: (i, 0)),
            pl.BlockSpec((tm, 2 * N_EXPERTS), lambda i: (i, 0)),
        ],
    )
    x1, lg = pl.pallas_call(
        _merge_kernel,
        grid_spec=grid_spec,
        out_shape=[jax.ShapeDtypeStruct((N_TOK, D_MODEL), jnp.float32),
                   jax.ShapeDtypeStruct((N_TOK, 2 * N_EXPERTS), jnp.float32)],
        compiler_params=pltpu.CompilerParams(
            dimension_semantics=("arbitrary",),
            vmem_limit_bytes=_vmem_limit(
                ((tm, ATTN_W), jnp.bfloat16, 2), ((tm, SGU_W), jnp.bfloat16, 2),
                ((tm, D_MODEL), jnp.bfloat16, 4), ((tm, D_MODEL), jnp.float32, 4),
                ((ATTN_W + SGU_W + D_MODEL, D_MODEL), jnp.bfloat16, 1),
                ((D_MODEL, 2 * N_EXPERTS), jnp.bfloat16, 2),
                ((tm, D_MODEL), jnp.float32, 8))),
        name="merge",
    )(attn, sgu, h, h, x2d, wa, wb, wo, g, b, wr)
    return x1, lg[:, :N_EXPERTS].T


def _first_argmax(v, rows):
    m = jnp.max(v, axis=0, keepdims=True)
    i = jnp.min(jnp.where(v == m, rows, float(v.shape[0])), axis=0, keepdims=True)
    return m, i


def _row_index(shape):
    return lax.broadcasted_iota(jnp.int32, shape, 0).astype(jnp.float32)


def _route_kernel(lg_ref, bias_ref, idx_ref, w_ref):
    tt = lg_ref.shape[1]
    neg_inf = jnp.float32(-jnp.inf)
    scores = jax.nn.sigmoid(lg_ref[...])
    biased = scores + bias_ref[...]
    row_g = _row_index((GROUP_SIZE, tt))
    gs = []
    for g in range(N_EXPERT_GROUPS):
        blk = biased[g * GROUP_SIZE:(g + 1) * GROUP_SIZE]
        m1, i1 = _first_argmax(blk, row_g)
        m2 = jnp.max(jnp.where(row_g == i1, neg_inf, blk), axis=0, keepdims=True)
        gs.append(m1 + m2)
    cur = jnp.concatenate(gs, axis=0)
    row_n = _row_index((N_EXPERT_GROUPS, tt))
    sel = jnp.zeros((N_EXPERT_GROUPS, tt), jnp.float32)
    for _ in range(TOPK_GROUPS):
        _, i = _first_argmax(cur, row_n)
        hit = row_n == i
        sel = jnp.where(hit, 1.0, sel)
        cur = jnp.where(hit, neg_inf, cur)
    emask = jnp.concatenate(
        [jnp.broadcast_to(sel[g:g + 1], (GROUP_SIZE, tt)) for g in range(N_EXPERT_GROUPS)], axis=0)
    masked = jnp.where(emask > 0.5, biased, neg_inf)
    row_e = _row_index((N_EXPERTS, tt))
    idx_rows, w_rows = [], []
    for _ in range(TOP_K):
        _, i = _first_argmax(masked, row_e)
        hit = row_e == i
        w_rows.append(jnp.sum(jnp.where(hit, scores, 0.0), axis=0, keepdims=True))
        idx_rows.append(i)
        masked = jnp.where(hit, neg_inf, masked)
    wsel = jnp.concatenate(w_rows, axis=0)
    idx_ref[...] = jnp.concatenate(idx_rows, axis=0).astype(jnp.int32)
    w_ref[...] = wsel / (jnp.sum(wsel, axis=0, keepdims=True) + 1e-20) * ROUTED_SCALE


def _route(logits_t, bias):
    tt = ROUTE_TT
    return pl.pallas_call(
        _route_kernel,
        grid=(N_TOK // tt,),
        in_specs=[pl.BlockSpec((N_EXPERTS, tt), lambda i: (0, i)),
                  pl.BlockSpec((N_EXPERTS, 1), lambda i: (0, 0))],
        out_specs=[pl.BlockSpec((TOP_K, tt), lambda i: (0, i)),
                   pl.BlockSpec((TOP_K, tt), lambda i: (0, i))],
        out_shape=[jax.ShapeDtypeStruct((TOP_K, N_TOK), jnp.int32),
                   jax.ShapeDtypeStruct((TOP_K, N_TOK), jnp.float32)],
        compiler_params=pltpu.CompilerParams(dimension_semantics=("arbitrary",)),
        name="route",
    )(logits_t, bias.reshape(N_EXPERTS, 1))


def _dispatch_plan(idx_t):
    bm = EXPERT_BM
    nb = N_EXPERT_BLOCKS
    flat_e = idx_t.reshape(-1)
    counts = jnp.sum((flat_e[:, None] == jnp.arange(N_EXPERTS, dtype=jnp.int32)[None, :])
                     .astype(jnp.int32), axis=0)
    e_ids = jnp.arange(N_EXPERTS, dtype=jnp.int32)
    upto = (e_ids[:, None] <= e_ids[None, :]).astype(jnp.int32)
    padded = (counts + bm - 1) // bm * bm
    pad_end = jnp.sum(padded[:, None] * upto, axis=0)
    fill_end = jnp.sum((padded - counts)[:, None] * upto, axis=0)
    n_fill = N_SLOTS - N_ASSIGN
    fill_key = jnp.sum((jnp.arange(n_fill, dtype=jnp.int32)[:, None] >= fill_end[None, :])
                       .astype(jnp.int32), axis=1)
    keys = jnp.concatenate([flat_e, fill_key])
    pos_bits = (N_SLOTS - 1).bit_length()
    slot = jnp.arange(N_SLOTS, dtype=jnp.int32)
    src = jnp.sort((keys << pos_bits) | slot) & ((1 << pos_bits) - 1)
    real = src < N_ASSIGN
    dst = jnp.where(real, src, N_ASSIGN + (slot & (2 * bm - 1)))
    dump_block = N_ASSIGN + jnp.arange(bm, dtype=jnp.int32)
    dst = jnp.concatenate([dump_block, dst, dst[-bm:]]).reshape(nb + 2, 1, bm)
    n_used = jnp.sum(jnp.any(real.reshape(nb, bm), axis=1).astype(jnp.int32))
    block_start = jnp.arange(nb + 1, dtype=jnp.int32) * bm
    block_e = jnp.sum((block_start[:, None] >= pad_end[None, :]).astype(jnp.int32), axis=1)
    block_e = jnp.minimum(block_e, N_EXPERTS - 1).astype(jnp.int32)
    prev_e = jnp.concatenate([jnp.full((1,), -1, jnp.int32), block_e[:-1]])
    fresh = (block_e != prev_e).astype(jnp.int32)
    later = (counts > 0)[None, :] & (e_ids[None, :] > block_e[:, None])
    next_e = jnp.min(jnp.where(later, e_ids[None, :], N_EXPERTS), axis=1).astype(jnp.int32)
    return dst, n_used.reshape(1), block_e, fresh, next_e


def _experts_kernel(nused_ref, be_ref, fresh_ref, nxt_ref, dstp_ref, dstc_ref, dstn_ref, dstnn_ref,
                    x_hbm, w1_hbm, w3_hbm, w2_hbm, y_hbm, xbuf, obuf, w1s, w3s, w2s, w1b, w3b, w2b,
                    gsem, ssem, wsem):
    b = pl.program_id(0)
    bm = EXPERT_BM
    slot = b % 2
    gslot = lax.rem(b, N_GATHER_BUFS)
    n_used = nused_ref[0]

    def start_gather(idx_ref, s):
        for r in range(bm):
            tok = idx_ref[0, 0, r] & (N_TOK - 1)
            pltpu.make_async_copy(x_hbm.at[pl.ds(tok, 1)], xbuf.at[s, pl.ds(r, 1)],
                                  gsem.at[s]).start()

    def wait_gather(s):
        pltpu.make_async_copy(x_hbm.at[pl.ds(0, bm)], xbuf.at[s], gsem.at[s]).wait()

    def start_scatter_prev(s):
        for r in range(bm):
            row = dstp_ref[0, 0, r]
            pltpu.make_async_copy(obuf.at[s, pl.ds(r, 1)], y_hbm.at[pl.ds(row, 1)],
                                  ssem.at[s]).start()

    def wait_scatter(s):
        pltpu.make_async_copy(obuf.at[s], y_hbm.at[pl.ds(0, bm)], ssem.at[s]).wait()

    def weight_copies(e):
        return (pltpu.make_async_copy(w1_hbm.at[e], w1s, wsem.at[0]),
                pltpu.make_async_copy(w3_hbm.at[e], w3s, wsem.at[1]),
                pltpu.make_async_copy(w2_hbm.at[e], w2s, wsem.at[2]))

    def block_step(w1v, w3v, w2v):
        start_scatter_prev(1 - slot)
        xb = _bf16(xbuf[gslot])
        a = _bf16(jax.nn.silu(_dot(xb, w1v)) * _dot(xb, w3v))
        start_gather(dstnn_ref, lax.rem(b + 2, N_GATHER_BUFS))
        obuf[slot] = _dot(a, w2v)

    @pl.when(b == 0)
    def _():
        for c in weight_copies(be_ref[0]):
            c.start(priority=1)
        obuf[1] = jnp.zeros(obuf.shape[1:], obuf.dtype)
        fill = pltpu.make_async_copy(obuf.at[1], y_hbm.at[pl.ds(N_ASSIGN + bm, bm)], ssem.at[0])
        fill.start()
        fill.wait()
        start_gather(dstc_ref, 0)
        start_gather(dstn_ref, 1)

    @pl.when((b >= 1) & (b <= n_used))
    def _():
        wait_scatter(slot)

    @pl.when(b < n_used)
    def _():
        wait_gather(gslot)

        @pl.when(fresh_ref[b] == 1)
        def _():
            for c in weight_copies(be_ref[b]):
                c.wait()
            w1v, w3v, w2v = _bf16(w1s[...]), _bf16(w3s[...]), _bf16(w2s[...])
            w1b[...] = w1v
            w3b[...] = w3v
            w2b[...] = w2v
            block_step(w1v, w3v, w2v)

            @pl.when(nxt_ref[b] < N_EXPERTS)
            def _():
                for c in weight_copies(nxt_ref[b]):
                    c.start(priority=1)

        @pl.when(fresh_ref[b] == 0)
        def _():
            block_step(w1b[...], w3b[...], w2b[...])

    @pl.when(b == n_used)
    def _():
        wait_gather(gslot)
        wait_gather(lax.rem(b + 1, N_GATHER_BUFS))
        start_scatter_prev(1 - slot)
        wait_scatter(1 - slot)


def _experts(x1, w1, w3, w2, dst, n_used, block_e, fresh, next_e):
    bm = EXPERT_BM
    nb = N_EXPERT_BLOCKS
    smem_blk = lambda f: pl.BlockSpec((1, 1, bm), f, memory_space=pltpu.SMEM)
    hbm = pl.BlockSpec(memory_space=pl.ANY)
    grid_spec = pltpu.PrefetchScalarGridSpec(
        num_scalar_prefetch=4,
        grid=(nb + 1,),
        in_specs=[
            smem_blk(lambda b, *_: (b, 0, 0)),
            smem_blk(lambda b, *_: (b + 1, 0, 0)),
            smem_blk(lambda b, *_: (jnp.minimum(b + 2, nb + 1), 0, 0)),
            smem_blk(lambda b, *_: (jnp.minimum(b + 3, nb + 1), 0, 0)),
            hbm, hbm, hbm, hbm,
        ],
        out_specs=hbm,
        scratch_shapes=[
            pltpu.VMEM((N_GATHER_BUFS, bm, D_MODEL), jnp.float32),
            pltpu.VMEM((2, bm, D_MODEL), jnp.float32),
            pltpu.VMEM((D_MODEL, D_EXPERT), jnp.float32),
            pltpu.VMEM((D_MODEL, D_EXPERT), jnp.float32),
            pltpu.VMEM((D_EXPERT, D_MODEL), jnp.float32),
            pltpu.VMEM((D_MODEL, D_EXPERT), jnp.bfloat16),
            pltpu.VMEM((D_MODEL, D_EXPERT), jnp.bfloat16),
            pltpu.VMEM((D_EXPERT, D_MODEL), jnp.bfloat16),
            pltpu.SemaphoreType.DMA((N_GATHER_BUFS,)),
            pltpu.SemaphoreType.DMA((2,)),
            pltpu.SemaphoreType.DMA((3,)),
        ],
    )
    return pl.pallas_call(
        _experts_kernel,
        grid_spec=grid_spec,
        out_shape=jax.ShapeDtypeStruct((N_ASSIGN + 2 * bm, D_MODEL), jnp.float32),
        compiler_params=pltpu.CompilerParams(
            dimension_semantics=("arbitrary",),
            vmem_limit_bytes=_vmem_limit(
                ((N_GATHER_BUFS + 2, bm, D_MODEL), jnp.float32, 1),
                ((3, D_MODEL, D_EXPERT), jnp.float32, 1),
                ((3, D_MODEL, D_EXPERT), jnp.bfloat16, 2),
                ((bm, D_MODEL), jnp.float32, 2), ((bm, D_EXPERT), jnp.float32, 4))),
        name="experts",
    )(n_used, block_e, fresh, next_e, dst, dst, dst, dst, x1, w1, w3, w2)


def _combine_kernel(*refs):
    y_refs = refs[:TOP_K]
    w_ref, x_ref, s1_ref, s3_ref, s2_ref, g_ref, b_ref, o_ref = refs[TOP_K:]
    x1 = x_ref[...]
    xb = _bf16(x1)
    a = _bf16(jax.nn.silu(_dot(xb, s1_ref[...])) * _dot(xb, s3_ref[...]))
    ffn = _dot(a, s2_ref[...])
    wts = w_ref[...]
    routed = wts[:, 0:1] * y_refs[0][...]
    for k in range(1, TOP_K):
        routed = routed + wts[:, k:k + 1] * y_refs[k][...]
    o_ref[...] = _layer_norm(ALPHA * x1 + (routed + ffn), g_ref[...], b_ref[...])


def _combine(y, wts, x1, s1, s3, s2, g, b):
    tm = COMBINE_TM
    nt = N_TOK // tm
    y_specs = [pl.BlockSpec((tm, D_MODEL), functools.partial(lambda i, k: (k * nt + i, 0), k=k))
               for k in range(TOP_K)]
    return pl.pallas_call(
        _combine_kernel,
        grid=(nt,),
        in_specs=y_specs + [
            pl.BlockSpec((tm, TOP_K), lambda i: (i, 0)),
            pl.BlockSpec((tm, D_MODEL), lambda i: (i, 0)),
            pl.BlockSpec((D_MODEL, D_SHARED), lambda i: (0, 0)),
            pl.BlockSpec((D_MODEL, D_SHARED), lambda i: (0, 0)),
            pl.BlockSpec((D_SHARED, D_MODEL), lambda i: (0, 0)),
            pl.BlockSpec((1, D_MODEL), lambda i: (0, 0)),
            pl.BlockSpec((1, D_MODEL), lambda i: (0, 0)),
        ],
        out_specs=pl.BlockSpec((tm, D_MODEL), lambda i: (i, 0)),
        out_shape=jax.ShapeDtypeStruct((N_TOK, D_MODEL), jnp.float32),
        compiler_params=pltpu.CompilerParams(
            dimension_semantics=("arbitrary",),
            vmem_limit_bytes=_vmem_limit(
                ((tm, D_MODEL), jnp.float32, 2 * (TOP_K + 2)), ((tm, LANES), jnp.float32, 2),
                ((3, D_MODEL, D_SHARED), jnp.bfloat16, 2), ((tm, D_MODEL), jnp.float32, 6))),
        name="combine",
    )(*([y] * TOP_K), wts, x1, s1, s3, s2, g, b)


def kernel(x, w_in, b_in, sinks, sgu_ln_g, sgu_ln_b, w_spatial, b_spatial, w_branch_attn,
           w_branch_sgu, w_out, ln1_g, ln1_b, w_router, router_bias, w1, w3, w2, ws1, ws3, ws2,
           ln2_g, ln2_b):
    assert x.shape == (BATCH, SEQ, D_MODEL) and w_in.shape == (1, D_MODEL, IN_W)
    x2d = x.reshape(N_TOK, D_MODEL)
    hkv, x_bf = _kv_proj(x2d, w_in[0], b_in)
    h = _in_proj(x_bf, w_in[0], b_in)
    attn, sgu = _mixers(h, hkv, sinks[0], sgu_ln_g[0], sgu_ln_b[0], w_spatial[0], b_spatial[0])
    x1, logits_t = _merge(attn, sgu, h, x2d, _bf16(w_branch_attn[0]), _bf16(w_branch_sgu[0]),
                          _bf16(w_out[0]), ln1_g, ln1_b, _router_parts(w_router[0]))
    idx_t, w_t = _route(logits_t, router_bias[0])
    y = _experts(x1, w1[0], w3[0], w2[0], *_dispatch_plan(idx_t))
    out = _combine(y, w_t.T, x1, _bf16(ws1[0]), _bf16(ws3[0]), _bf16(ws2[0]), ln2_g, ln2_b)
    return out.reshape(BATCH, SEQ, D_MODEL)
```

```python
import functools
import math

import numpy as np
import jax
import jax.numpy as jnp
from jax import lax
from jax.experimental import pallas as pl
from jax.experimental.pallas import tpu as pltpu

D_MODEL = 2048
BATCH = 2
SEQ = 4096
N_TOK = BATCH * SEQ
N_Q_HEADS = 32
N_KV_HEADS = 4
HEAD_DIM = 64
GQA = N_Q_HEADS // N_KV_HEADS
WINDOW = 128
ROPE_THETA = 500000.0
ROT_DIM = HEAD_DIM // 4
SGU_GROUPS = 8
SGU_CH = 128
N_EXPERTS = 64
N_EXPERT_GROUPS = 8
GROUP_SIZE = N_EXPERTS // N_EXPERT_GROUPS
TOPK_GROUPS = 4
TOP_K = 8
D_EXPERT = 512
D_SHARED = 512
ROUTED_SCALE = 2.5
ATTN_W = N_Q_HEADS * HEAD_DIM
KV_W = N_KV_HEADS * HEAD_DIM
SGU_W = SGU_GROUPS * SGU_CH
IN_W = ATTN_W + 2 * KV_W + 2 * SGU_W + 2 * D_MODEL
ALPHA = 2.0 ** 0.25
LN_EPS = 1e-5
N_ASSIGN = N_TOK * TOP_K

LANES = 128
V7X_VMEM_BYTES = 64 * 1024 * 1024


def _vmem_limit(*buffers):
    need = sum(copies * math.prod(shape) * jnp.dtype(dtype).itemsize
               for shape, dtype, copies in buffers)
    assert need <= V7X_VMEM_BYTES * 7 // 8, need
    return need

PROJ_TM = 1024
PROJ_TN = 1024
MERGE_TM = 256
ROUTE_TT = 512
EXPERT_BM = 256
N_GATHER_BUFS = 3
COMBINE_TM = 128
N_EXPERT_BLOCKS = -(-(N_ASSIGN + N_EXPERTS * (EXPERT_BM - 1)) // EXPERT_BM)
N_SLOTS = N_EXPERT_BLOCKS * EXPERT_BM

H_Q, H_GA, H_GB, H_U, H_VG = 0, 2048, 4096, 6144, 7168
H_W = 8192
PROJ_UNIT = 512
_SRC_UNIT = np.array([0, 1, 2, 3, 9, 10, 11, 12, 13, 14, 15, 16, 5, 6, 7, 8], np.int32)
_SRC_UNIT_KV = ATTN_W // PROJ_UNIT
_N_PROJ_TILES = H_W // PROJ_TN
_Q_TILES = ATTN_W // PROJ_TN
_GATE_END = H_U // PROJ_TN


def _bf16(a):
    return a.astype(jnp.bfloat16)


def _dot(a, b):
    return jnp.dot(a, b, preferred_element_type=jnp.float32)


def _dot_nt(a, b):
    return lax.dot_general(a, b, (((1,), (1,)), ((), ())), preferred_element_type=jnp.float32)


def _rope_slab(x, c, s_next, s_prev):
    return (x * c + pltpu.roll(x, LANES - ROT_DIM // 2, axis=1) * s_next
            + pltpu.roll(x, ROT_DIM // 2, axis=1) * s_prev)


def _in_proj_kernel(src_ref, x_ref, wa_ref, wb_ref, ba_ref, bb_ref, c_ref, sn_ref, sp_ref, o_ref,
                    wbf_ref):
    j = pl.program_id(0)
    i = pl.program_id(1)

    @pl.when(i == 0)
    def _():
        wbf_ref[:, :PROJ_UNIT] = _bf16(wa_ref[...])
        wbf_ref[:, PROJ_UNIT:] = _bf16(wb_ref[...])

    def project():
        bias = jnp.concatenate([ba_ref[...], bb_ref[...]], axis=1)
        return _dot(x_ref[...], wbf_ref[...]) + bias

    @pl.when(j < _Q_TILES)
    def _():
        acc = project()
        c, sn, sp = c_ref[...], sn_ref[...], sp_ref[...]
        scale = HEAD_DIM ** -0.5
        for t in range(PROJ_TN // LANES):
            sl = slice(t * LANES, (t + 1) * LANES)
            o_ref[:, sl] = _bf16(_rope_slab(acc[:, sl], c, sn, sp) * scale)

    @pl.when((j >= _Q_TILES) & (j < _GATE_END))
    def _():
        o_ref[...] = _bf16(jax.nn.sigmoid(project()))

    @pl.when(j >= _GATE_END)
    def _():
        o_ref[...] = _bf16(jax.nn.gelu(project()))


def _kv_proj_kernel(x_ref, w_ref, b_ref, c_ref, sn_ref, sp_ref, o_ref, xb_ref, wbf_ref):
    @pl.when(pl.program_id(0) == 0)
    def _():
        wbf_ref[...] = _bf16(w_ref[...])

    xb = _bf16(x_ref[...])
    xb_ref[...] = xb
    acc = _dot(xb, wbf_ref[...]) + b_ref[...]
    c, sn, sp = c_ref[...], sn_ref[...], sp_ref[...]
    for t in range(2 * KV_W // LANES):
        sl = slice(t * LANES, (t + 1) * LANES)
        if t < KV_W // LANES:
            o_ref[:, sl] = _bf16(_rope_slab(acc[:, sl], c, sn, sp))
        else:
            o_ref[:, sl] = _bf16(acc[:, sl])


def _rope_tables():
    half = ROT_DIM // 2
    inv_freq = ROPE_THETA ** (-np.arange(0, ROT_DIM, 2, dtype=np.float32) / ROT_DIM)
    pos = np.arange(SEQ, dtype=np.float32)
    ang = jnp.asarray(pos[:, None] * inv_freq[None, :].astype(np.float32), jnp.float32)
    cos, sin = jnp.cos(ang), jnp.sin(ang)
    ones = jnp.ones((SEQ, HEAD_DIM - ROT_DIM), jnp.float32)
    zeros = jnp.zeros((SEQ, HEAD_DIM - ROT_DIM), jnp.float32)
    zh = jnp.zeros((SEQ, half), jnp.float32)
    c = jnp.concatenate([cos, cos, ones], axis=1)
    s_next = jnp.concatenate([-sin, zh, zeros], axis=1)
    s_prev = jnp.concatenate([zh, sin, zeros], axis=1)
    rep = LANES // HEAD_DIM
    return tuple(jnp.tile(t, (1, rep)) for t in (c, s_next, s_prev))


def _in_proj(x_bf, w_in, b_in):
    c, sn, sp = _rope_tables()
    n_i = N_TOK // PROJ_TM
    pos_tiles = SEQ // PROJ_TM
    tbl = pl.BlockSpec((PROJ_TM, LANES), lambda j, i, src: (i % pos_tiles, 0))
    grid_spec = pltpu.PrefetchScalarGridSpec(
        num_scalar_prefetch=1,
        grid=(_N_PROJ_TILES, n_i),
        in_specs=[
            pl.BlockSpec((PROJ_TM, D_MODEL), lambda j, i, src: (i, 0)),
            pl.BlockSpec((D_MODEL, PROJ_UNIT), lambda j, i, src: (0, src[2 * j])),
            pl.BlockSpec((D_MODEL, PROJ_UNIT), lambda j, i, src: (0, src[2 * j + 1])),
            pl.BlockSpec((1, PROJ_UNIT), lambda j, i, src: (0, src[2 * j])),
            pl.BlockSpec((1, PROJ_UNIT), lambda j, i, src: (0, src[2 * j + 1])),
            tbl, tbl, tbl,
        ],
        out_specs=pl.BlockSpec((PROJ_TM, PROJ_TN), lambda j, i, src: (i, j)),
        scratch_shapes=[pltpu.VMEM((D_MODEL, PROJ_TN), jnp.bfloat16)],
    )
    return pl.pallas_call(
        _in_proj_kernel,
        grid_spec=grid_spec,
        out_shape=jax.ShapeDtypeStruct((N_TOK, H_W), jnp.bfloat16),
        compiler_params=pltpu.CompilerParams(
            dimension_semantics=("arbitrary", "arbitrary"),
            vmem_limit_bytes=_vmem_limit(
                ((PROJ_TM, D_MODEL), jnp.bfloat16, 2), ((D_MODEL, PROJ_TN), jnp.float32, 2),
                ((PROJ_TM, LANES), jnp.float32, 6), ((PROJ_TM, PROJ_TN), jnp.bfloat16, 2),
                ((D_MODEL, PROJ_TN), jnp.bfloat16, 1), ((PROJ_TM, PROJ_TN), jnp.float32, 3))),
        name="in_proj",
    )(jnp.asarray(_SRC_UNIT), x_bf, w_in, w_in, b_in, b_in, c, sn, sp)


def _kv_proj(x2d, w_in, b_in):
    c, sn, sp = _rope_tables()
    tm = PROJ_TM // 2
    pos_tiles = SEQ // tm
    tbl = pl.BlockSpec((tm, LANES), lambda i: (i % pos_tiles, 0))
    return pl.pallas_call(
        _kv_proj_kernel,
        grid=(N_TOK // tm,),
        in_specs=[
            pl.BlockSpec((tm, D_MODEL), lambda i: (i, 0)),
            pl.BlockSpec((D_MODEL, 2 * KV_W), lambda i: (0, _SRC_UNIT_KV)),
            pl.BlockSpec((1, 2 * KV_W), lambda i: (0, _SRC_UNIT_KV)),
            tbl, tbl, tbl,
        ],
        out_specs=[pl.BlockSpec((tm, 2 * KV_W), lambda i: (i, 0)),
                   pl.BlockSpec((tm, D_MODEL), lambda i: (i, 0))],
        out_shape=[jax.ShapeDtypeStruct((N_TOK, 2 * KV_W), jnp.bfloat16),
                   jax.ShapeDtypeStruct((N_TOK, D_MODEL), jnp.bfloat16)],
        scratch_shapes=[pltpu.VMEM((D_MODEL, 2 * KV_W), jnp.bfloat16)],
        compiler_params=pltpu.CompilerParams(
            dimension_semantics=("arbitrary",),
            vmem_limit_bytes=_vmem_limit(
                ((tm, D_MODEL), jnp.float32, 2), ((tm, D_MODEL), jnp.bfloat16, 3),
                ((D_MODEL, 2 * KV_W), jnp.float32, 2), ((D_MODEL, 2 * KV_W), jnp.bfloat16, 1),
                ((tm, LANES), jnp.float32, 6), ((tm, 2 * KV_W), jnp.float32, 3))),
        name="kv_proj",
    )(x2d, w_in, b_in, c, sn, sp)


def _mixers_kernel(sink_ref, q_ref, kvc_ref, kvp_ref, u_ref, vg_ref, lng_ref, lnb_ref,
                   ws_ref, bs_ref, attn_ref, sgu_ref):
    n = pl.program_id(0)
    w = WINDOW
    first_key = jnp.where((n % (SEQ // w)) == 0, w, 0)
    pairs = GQA // 2
    rows = pairs * w

    qi = lax.broadcasted_iota(jnp.int32, (w, 2 * w), 0)
    kj = lax.broadcasted_iota(jnp.int32, (w, 2 * w), 1)
    valid = (kj > qi) & (kj <= qi + w) & (kj >= first_key)
    valid = jnp.concatenate([valid] * pairs, axis=0)
    lane = lax.broadcasted_iota(jnp.int32, (2 * w, LANES), 1)
    low = lane < HEAD_DIM
    lane_r = lax.broadcasted_iota(jnp.int32, (rows, LANES), 1)
    low_r = lane_r < HEAD_DIM
    ones_low = jnp.where(low, 1.0, 0.0).astype(jnp.bfloat16)
    ones_high = jnp.where(low, 0.0, 1.0).astype(jnp.bfloat16)
    neg_inf = jnp.float32(-jnp.inf)

    kv = jnp.concatenate([kvp_ref[...], kvc_ref[...]], axis=0).astype(jnp.float32)

    def padded(group, head_is_high):
        rolled = pltpu.roll(group, HEAD_DIM, axis=1)
        if head_is_high:
            lo_half, hi_half = rolled, group
        else:
            lo_half, hi_half = group, rolled
        return (_bf16(jnp.where(low, lo_half, 0.0)), _bf16(jnp.where(low, 0.0, hi_half)))

    for h in range(N_KV_HEADS):
        g0 = (h // 2) * LANES
        k_lo, k_hi = padded(kv[:, g0:g0 + LANES], h % 2 == 1)
        v_lo, v_hi = padded(kv[:, KV_W + g0:KV_W + g0 + LANES], h % 2 == 1)
        r_even = jnp.concatenate([v_lo, ones_low], axis=1)
        r_odd = jnp.concatenate([v_hi, ones_high], axis=1)
        q4 = jnp.concatenate(
            [q_ref[:, (h * pairs + p) * LANES:(h * pairs + p + 1) * LANES] for p in range(pairs)],
            axis=0)
        sink_e = jnp.concatenate(
            [jnp.full((w, 1), sink_ref[h * GQA + 2 * p], jnp.float32) for p in range(pairs)], axis=0)
        sink_o = jnp.concatenate(
            [jnp.full((w, 1), sink_ref[h * GQA + 2 * p + 1], jnp.float32) for p in range(pairs)], axis=0)

        s_e = jnp.where(valid, _dot_nt(q4, k_lo), neg_inf)
        s_o = jnp.where(valid, _dot_nt(q4, k_hi), neg_inf)
        m_e = jnp.maximum(jnp.max(s_e, axis=1, keepdims=True), sink_e)
        m_o = jnp.maximum(jnp.max(s_o, axis=1, keepdims=True), sink_o)
        p_e = _bf16(jnp.exp(s_e - m_e))
        p_o = _bf16(jnp.exp(s_o - m_o))
        acc = _dot(p_e, r_even) + _dot(p_o, r_odd)
        sink_term = jnp.exp(jnp.where(low_r, sink_e - m_e, sink_o - m_o))
        out = acc[:, :LANES] / (acc[:, LANES:] + sink_term)
        for p in range(pairs):
            c0 = (h * pairs + p) * LANES
            attn_ref[:, c0:c0 + LANES] = _bf16(out[p * w:(p + 1) * w])

    ti = lax.broadcasted_iota(jnp.int32, (w, w), 0)
    si = lax.broadcasted_iota(jnp.int32, (w, w), 1)
    causal = si <= ti
    for g in range(SGU_GROUPS):
        sl = slice(g * SGU_CH, (g + 1) * SGU_CH)
        x = vg_ref[:, sl].astype(jnp.float32)
        mu = jnp.mean(x, axis=-1, keepdims=True)
        xc = x - mu
        var = jnp.mean(xc * xc, axis=-1, keepdims=True)
        vn = xc * lax.rsqrt(var + LN_EPS) * lng_ref[:, sl] + lnb_ref[:, sl]
        wsg = _bf16(jnp.where(causal, ws_ref[g], 0.0))
        sv = _dot(wsg, _bf16(vn)) + bs_ref[g]
        sgu_ref[:, sl] = _bf16(u_ref[:, sl].astype(jnp.float32) * sv)


def _mixers(h, hkv, sinks, ln_g, ln_b, w_s, b_s):
    w = WINDOW
    nb = N_TOK // w
    grid_spec = pltpu.PrefetchScalarGridSpec(
        num_scalar_prefetch=0,
        grid=(nb,),
        in_specs=[
            pl.BlockSpec(memory_space=pltpu.SMEM),
            pl.BlockSpec((w, ATTN_W), lambda n: (n, H_Q // ATTN_W)),
            pl.BlockSpec((w, 2 * KV_W), lambda n: (n, 0)),
            pl.BlockSpec((w, 2 * KV_W), lambda n: (jnp.maximum(n - 1, 0), 0)),
            pl.BlockSpec((w, SGU_W), lambda n: (n, H_U // SGU_W)),
            pl.BlockSpec((w, SGU_W), lambda n: (n, H_VG // SGU_W)),
            pl.BlockSpec((1, SGU_W), lambda n: (0, 0)),
            pl.BlockSpec((1, SGU_W), lambda n: (0, 0)),
            pl.BlockSpec((SGU_GROUPS, w, w), lambda n: (0, 0, 0)),
            pl.BlockSpec((SGU_GROUPS, w, 1), lambda n: (0, 0, 0)),
        ],
        out_specs=[
            pl.BlockSpec((w, ATTN_W), lambda n: (n, 0)),
            pl.BlockSpec((w, SGU_W), lambda n: (n, 0)),
        ],
    )
    return pl.pallas_call(
        _mixers_kernel,
        grid_spec=grid_spec,
        out_shape=[jax.ShapeDtypeStruct((N_TOK, ATTN_W), jnp.bfloat16),
                   jax.ShapeDtypeStruct((N_TOK, SGU_W), jnp.bfloat16)],
        compiler_params=pltpu.CompilerParams(
            dimension_semantics=("arbitrary",),
            vmem_limit_bytes=_vmem_limit(
                ((w, ATTN_W), jnp.bfloat16, 4), ((w, 2 * KV_W), jnp.bfloat16, 4),
                ((w, SGU_W), jnp.bfloat16, 6), ((SGU_GROUPS, w, w), jnp.float32, 2),
                ((SGU_GROUPS, w, LANES), jnp.float32, 2),
                ((GQA // 2 * w, 2 * w), jnp.float32, 16))),
        name="mixers",
    )(sinks, h, hkv, hkv, h, h, ln_g.reshape(1, SGU_W), ln_b.reshape(1, SGU_W), w_s,
      b_s.reshape(SGU_GROUPS, w, 1))


def _layer_norm(z, g, b):
    mu = jnp.mean(z, axis=-1, keepdims=True)
    zc = z - mu
    var = jnp.mean(zc * zc, axis=-1, keepdims=True)
    return zc * lax.rsqrt(var + LN_EPS) * g + b


def _merge_kernel(attn_ref, sgu_ref, ga_ref, gb_ref, x_ref, wa_ref, wb_ref, wo_ref, g_ref, b_ref,
                  wr_ref, x1_ref, lg_ref):
    mix = (ga_ref[...].astype(jnp.float32) * _dot(attn_ref[...], wa_ref[...])
           + gb_ref[...].astype(jnp.float32) * _dot(sgu_ref[...], wb_ref[...]))
    z = ALPHA * x_ref[...] + _dot(_bf16(mix), wo_ref[...])
    x1 = _layer_norm(z, g_ref[...], b_ref[...])
    x1_ref[...] = x1
    x_hi = _bf16(x1)
    x_lo = _bf16(x1 - x_hi.astype(jnp.float32))
    parts = _dot(x_hi, wr_ref[...]) + _dot(x_lo, wr_ref[...])
    lg_ref[...] = parts + pltpu.roll(parts, N_EXPERTS, axis=1)


def _router_parts(w_router):
    hi = _bf16(w_router)
    lo = _bf16(w_router - hi.astype(jnp.float32))
    return jnp.concatenate([hi, lo], axis=1)


def _merge(attn, sgu, h, x2d, wa, wb, wo, g, b, wr):
    tm = MERGE_TM
    resident = pl.Buffered(1)
    grid_spec = pltpu.PrefetchScalarGridSpec(
        num_scalar_prefetch=0,
        grid=(N_TOK // tm,),
        in_specs=[
            pl.BlockSpec((tm, ATTN_W), lambda i: (i, 0)),
            pl.BlockSpec((tm, SGU_W), lambda i: (i, 0)),
            pl.BlockSpec((tm, D_MODEL), lambda i: (i, H_GA // D_MODEL)),
            pl.BlockSpec((tm, D_MODEL), lambda i: (i, H_GB // D_MODEL)),
            pl.BlockSpec((tm, D_MODEL), lambda i: (i, 0)),
            pl.BlockSpec((ATTN_W, D_MODEL), lambda i: (0, 0), pipeline_mode=resident),
            pl.BlockSpec((SGU_W, D_MODEL), lambda i: (0, 0), pipeline_mode=resident),
            pl.BlockSpec((D_MODEL, D_MODEL), lambda i: (0, 0), pipeline_mode=resident),
            pl.BlockSpec((1, D_MODEL), lambda i: (0, 0)),
            pl.BlockSpec((1, D_MODEL), lambda i: (0, 0)),
            pl.BlockSpec((D_MODEL, 2 * N_EXPERTS), lambda i: (0, 0)),
        ],
        out_specs=[
            pl.BlockSpec((tm, D_MODEL), lambda i: (i, 0)),
            pl.BlockSpec((tm, 2 * N_EXPERTS), lambda i: (i, 0)),
        ],
    )
    x1, lg = pl.pallas_call(
        _merge_kernel,
        grid_spec=grid_spec,
        out_shape=[jax.ShapeDtypeStruct((N_TOK, D_MODEL), jnp.float32),
                   jax.ShapeDtypeStruct((N_TOK, 2 * N_EXPERTS), jnp.float32)],
        compiler_params=pltpu.CompilerParams(
            dimension_semantics=("arbitrary",),
            vmem_limit_bytes=_vmem_limit(
                ((tm, ATTN_W), jnp.bfloat16, 2), ((tm, SGU_W), jnp.bfloat16, 2),
                ((tm, D_MODEL), jnp.bfloat16, 4), ((tm, D_MODEL), jnp.float32, 4),
                ((ATTN_W + SGU_W + D_MODEL, D_MODEL), jnp.bfloat16, 1),
                ((D_MODEL, 2 * N_EXPERTS), jnp.bfloat16, 2),
                ((tm, D_MODEL), jnp.float32, 8))),
        name="merge",
    )(attn, sgu, h, h, x2d, wa, wb, wo, g, b, wr)
    return x1, lg[:, :N_EXPERTS].T


def _first_argmax(v, rows):
    m = jnp.max(v, axis=0, keepdims=True)
    i = jnp.min(jnp.where(v == m, rows, float(v.shape[0])), axis=0, keepdims=True)
    return m, i


def _row_index(shape):
    return lax.broadcasted_iota(jnp.int32, shape, 0).astype(jnp.float32)


def _route_kernel(lg_ref, bias_ref, idx_ref, w_ref):
    tt = lg_ref.shape[1]
    neg_inf = jnp.float32(-jnp.inf)
    scores = jax.nn.sigmoid(lg_ref[...])
    biased = scores + bias_ref[...]
    row_g = _row_index((GROUP_SIZE, tt))
    gs = []
    for g in range(N_EXPERT_GROUPS):
        blk = biased[g * GROUP_SIZE:(g + 1) * GROUP_SIZE]
        m1, i1 = _first_argmax(blk, row_g)
        m2 = jnp.max(jnp.where(row_g == i1, neg_inf, blk), axis=0, keepdims=True)
        gs.append(m1 + m2)
    cur = jnp.concatenate(gs, axis=0)
    row_n = _row_index((N_EXPERT_GROUPS, tt))
    sel = jnp.zeros((N_EXPERT_GROUPS, tt), jnp.float32)
    for _ in range(TOPK_GROUPS):
        _, i = _first_argmax(cur, row_n)
        hit = row_n == i
        sel = jnp.where(hit, 1.0, sel)
        cur = jnp.where(hit, neg_inf, cur)
    emask = jnp.concatenate(
        [jnp.broadcast_to(sel[g:g + 1], (GROUP_SIZE, tt)) for g in range(N_EXPERT_GROUPS)], axis=0)
    masked = jnp.where(emask > 0.5, biased, neg_inf)
    row_e = _row_index((N_EXPERTS, tt))
    idx_rows, w_rows = [], []
    for _ in range(TOP_K):
        _, i = _first_argmax(masked, row_e)
        hit = row_e == i
        w_rows.append(jnp.sum(jnp.where(hit, scores, 0.0), axis=0, keepdims=True))
        idx_rows.append(i)
        masked = jnp.where(hit, neg_inf, masked)
    wsel = jnp.concatenate(w_rows, axis=0)
    idx_ref[...] = jnp.concatenate(idx_rows, axis=0).astype(jnp.int32)
    w_ref[...] = wsel / (jnp.sum(wsel, axis=0, keepdims=True) + 1e-20) * ROUTED_SCALE


def _route(logits_t, bias):
    tt = ROUTE_TT
    return pl.pallas_call(
        _route_kernel,
        grid=(N_TOK // tt,),
        in_specs=[pl.BlockSpec((N_EXPERTS, tt), lambda i: (0, i)),
                  pl.BlockSpec((N_EXPERTS, 1), lambda i: (0, 0))],
        out_specs=[pl.BlockSpec((TOP_K, tt), lambda i: (0, i)),
                   pl.BlockSpec((TOP_K, tt), lambda i: (0, i))],
        out_shape=[jax.ShapeDtypeStruct((TOP_K, N_TOK), jnp.int32),
                   jax.ShapeDtypeStruct((TOP_K, N_TOK), jnp.float32)],
        compiler_params=pltpu.CompilerParams(dimension_semantics=("arbitrary",)),
        name="route",
    )(logits_t, bias.reshape(N_EXPERTS, 1))


def _dispatch_plan(idx_t):
    bm = EXPERT_BM
    nb = N_EXPERT_BLOCKS
    flat_e = idx_t.reshape(-1)
    counts = jnp.sum((flat_e[:, None] == jnp.arange(N_EXPERTS, dtype=jnp.int32)[None, :])
                     .astype(jnp.int32), axis=0)
    e_ids = jnp.arange(N_EXPERTS, dtype=jnp.int32)
    upto = (e_ids[:, None] <= e_ids[None, :]).astype(jnp.int32)
    padded = (counts + bm - 1) // bm * bm
    pad_end = jnp.sum(padded[:, None] * upto, axis=0)
    fill_end = jnp.sum((padded - counts)[:, None] * upto, axis=0)
    n_fill = N_SLOTS - N_ASSIGN
    fill_key = jnp.sum((jnp.arange(n_fill, dtype=jnp.int32)[:, None] >= fill_end[None, :])
                       .astype(jnp.int32), axis=1)
    keys = jnp.concatenate([flat_e, fill_key])
    pos_bits = (N_SLOTS - 1).bit_length()
    slot = jnp.arange(N_SLOTS, dtype=jnp.int32)
    src = jnp.sort((keys << pos_bits) | slot) & ((1 << pos_bits) - 1)
    real = src < N_ASSIGN
    dst = jnp.where(real, src, N_ASSIGN + (slot & (2 * bm - 1)))
    dump_block = N_ASSIGN + jnp.arange(bm, dtype=jnp.int32)
    dst = jnp.concatenate([dump_block, dst, dst[-bm:]]).reshape(nb + 2, 1, bm)
    n_used = jnp.sum(jnp.any(real.reshape(nb, bm), axis=1).astype(jnp.int32))
    block_start = jnp.arange(nb + 1, dtype=jnp.int32) * bm
    block_e = jnp.sum((block_start[:, None] >= pad_end[None, :]).astype(jnp.int32), axis=1)
    block_e = jnp.minimum(block_e, N_EXPERTS - 1).astype(jnp.int32)
    prev_e = jnp.concatenate([jnp.full((1,), -1, jnp.int32), block_e[:-1]])
    fresh = (block_e != prev_e).astype(jnp.int32)
    later = (counts > 0)[None, :] & (e_ids[None, :] > block_e[:, None])
    next_e = jnp.min(jnp.where(later, e_ids[None, :], N_EXPERTS), axis=1).astype(jnp.int32)
    return dst, n_used.reshape(1), block_e, fresh, next_e


def _experts_kernel(nused_ref, be_ref, fresh_ref, nxt_ref, dstp_ref, dstc_ref, dstn_ref, dstnn_ref,
                    x_hbm, w1_hbm, w3_hbm, w2_hbm, y_hbm, xbuf, obuf, w1s, w3s, w2s, w1b, w3b, w2b,
                    gsem, ssem, wsem):
    b = pl.program_id(0)
    bm = EXPERT_BM
    slot = b % 2
    gslot = lax.rem(b, N_GATHER_BUFS)
    n_used = nused_ref[0]

    def start_gather(idx_ref, s):
        for r in range(bm):
            tok = idx_ref[0, 0, r] & (N_TOK - 1)
            pltpu.make_async_copy(x_hbm.at[pl.ds(tok, 1)], xbuf.at[s, pl.ds(r, 1)],
                                  gsem.at[s]).start(priority=r % 2)

    def wait_gather(s):
        pltpu.make_async_copy(x_hbm.at[pl.ds(0, bm)], xbuf.at[s], gsem.at[s]).wait()

    def start_scatter_prev(s):
        for r in range(bm):
            row = dstp_ref[0, 0, r]
            pltpu.make_async_copy(obuf.at[s, pl.ds(r, 1)], y_hbm.at[pl.ds(row, 1)],
                                  ssem.at[s]).start(priority=r % 2)

    def wait_scatter(s):
        pltpu.make_async_copy(obuf.at[s], y_hbm.at[pl.ds(0, bm)], ssem.at[s]).wait()

    def weight_copies(e):
        return (pltpu.make_async_copy(w1_hbm.at[e], w1s, wsem.at[0]),
                pltpu.make_async_copy(w3_hbm.at[e], w3s, wsem.at[1]),
                pltpu.make_async_copy(w2_hbm.at[e], w2s, wsem.at[2]))

    def block_step(w1v, w3v, w2v):
        start_scatter_prev(1 - slot)
        xb = _bf16(xbuf[gslot])
        a = _bf16(jax.nn.silu(_dot(xb, w1v)) * _dot(xb, w3v))
        start_gather(dstnn_ref, lax.rem(b + 2, N_GATHER_BUFS))
        obuf[slot] = _dot(a, w2v)

    @pl.when(b == 0)
    def _():
        for c in weight_copies(be_ref[0]):
            c.start(priority=1)
        obuf[1] = jnp.zeros(obuf.shape[1:], obuf.dtype)
        fill = pltpu.make_async_copy(obuf.at[1], y_hbm.at[pl.ds(N_ASSIGN + bm, bm)], ssem.at[0])
        fill.start()
        fill.wait()
        start_gather(dstc_ref, 0)
        start_gather(dstn_ref, 1)

    @pl.when((b >= 1) & (b <= n_used))
    def _():
        wait_scatter(slot)

    @pl.when(b < n_used)
    def _():
        wait_gather(gslot)

        @pl.when(fresh_ref[b] == 1)
        def _():
            for c in weight_copies(be_ref[b]):
                c.wait()
            w1v, w3v, w2v = _bf16(w1s[...]), _bf16(w3s[...]), _bf16(w2s[...])
            w1b[...] = w1v
            w3b[...] = w3v
            w2b[...] = w2v
            block_step(w1v, w3v, w2v)

            @pl.when(nxt_ref[b] < N_EXPERTS)
            def _():
                for c in weight_copies(nxt_ref[b]):
                    c.start(priority=1)

        @pl.when(fresh_ref[b] == 0)
        def _():
            block_step(w1b[...], w3b[...], w2b[...])

    @pl.when(b == n_used)
    def _():
        wait_gather(gslot)
        wait_gather(lax.rem(b + 1, N_GATHER_BUFS))
        start_scatter_prev(1 - slot)
        wait_scatter(1 - slot)


def _experts(x1, w1, w3, w2, dst, n_used, block_e, fresh, next_e):
    bm = EXPERT_BM
    nb = N_EXPERT_BLOCKS
    smem_blk = lambda f: pl.BlockSpec((1, 1, bm), f, memory_space=pltpu.SMEM)
    hbm = pl.BlockSpec(memory_space=pl.ANY)
    grid_spec = pltpu.PrefetchScalarGridSpec(
        num_scalar_prefetch=4,
        grid=(nb + 1,),
        in_specs=[
            smem_blk(lambda b, *_: (b, 0, 0)),
            smem_blk(lambda b, *_: (b + 1, 0, 0)),
            smem_blk(lambda b, *_: (jnp.minimum(b + 2, nb + 1), 0, 0)),
            smem_blk(lambda b, *_: (jnp.minimum(b + 3, nb + 1), 0, 0)),
            hbm, hbm, hbm, hbm,
        ],
        out_specs=hbm,
        scratch_shapes=[
            pltpu.VMEM((N_GATHER_BUFS, bm, D_MODEL), jnp.float32),
            pltpu.VMEM((2, bm, D_MODEL), jnp.float32),
            pltpu.VMEM((D_MODEL, D_EXPERT), jnp.float32),
            pltpu.VMEM((D_MODEL, D_EXPERT), jnp.float32),
            pltpu.VMEM((D_EXPERT, D_MODEL), jnp.float32),
            pltpu.VMEM((D_MODEL, D_EXPERT), jnp.bfloat16),
            pltpu.VMEM((D_MODEL, D_EXPERT), jnp.bfloat16),
            pltpu.VMEM((D_EXPERT, D_MODEL), jnp.bfloat16),
            pltpu.SemaphoreType.DMA((N_GATHER_BUFS,)),
            pltpu.SemaphoreType.DMA((2,)),
            pltpu.SemaphoreType.DMA((3,)),
        ],
    )
    return pl.pallas_call(
        _experts_kernel,
        grid_spec=grid_spec,
        out_shape=jax.ShapeDtypeStruct((N_ASSIGN + 2 * bm, D_MODEL), jnp.float32),
        compiler_params=pltpu.CompilerParams(
            dimension_semantics=("arbitrary",),
            vmem_limit_bytes=_vmem_limit(
                ((N_GATHER_BUFS + 2, bm, D_MODEL), jnp.float32, 1),
                ((3, D_MODEL, D_EXPERT), jnp.float32, 1),
                ((3, D_MODEL, D_EXPERT), jnp.bfloat16, 2),
                ((bm, D_MODEL), jnp.float32, 2), ((bm, D_EXPERT), jnp.float32, 4))),
        name="experts",
    )(n_used, block_e, fresh, next_e, dst, dst, dst, dst, x1, w1, w3, w2)


def _combine_kernel(*refs):
    y_refs = refs[:TOP_K]
    w_ref, x_ref, s1_ref, s3_ref, s2_ref, g_ref, b_ref, o_ref = refs[TOP_K:]
    x1 = x_ref[...]
    xb = _bf16(x1)
    a = _bf16(jax.nn.silu(_dot(xb, s1_ref[...])) * _dot(xb, s3_ref[...]))
    ffn = _dot(a, s2_ref[...])
    wts = w_ref[...]
    routed = wts[:, 0:1] * y_refs[0][...]
    for k in range(1, TOP_K):
        routed = routed + wts[:, k:k + 1] * y_refs[k][...]
    o_ref[...] = _layer_norm(ALPHA * x1 + (routed + ffn), g_ref[...], b_ref[...])


def _combine(y, wts, x1, s1, s3, s2, g, b):
    tm = COMBINE_TM
    nt = N_TOK // tm
    y_specs = [pl.BlockSpec((tm, D_MODEL), functools.partial(lambda i, k: (k * nt + i, 0), k=k))
               for k in range(TOP_K)]
    return pl.pallas_call(
        _combine_kernel,
        grid=(nt,),
        in_specs=y_specs + [
            pl.BlockSpec((tm, TOP_K), lambda i: (i, 0)),
            pl.BlockSpec((tm, D_MODEL), lambda i: (i, 0)),
            pl.BlockSpec((D_MODEL, D_SHARED), lambda i: (0, 0)),
            pl.BlockSpec((D_MODEL, D_SHARED), lambda i: (0, 0)),
            pl.BlockSpec((D_SHARED, D_MODEL), lambda i: (0, 0)),
            pl.BlockSpec((1, D_MODEL), lambda i: (0, 0)),
            pl.BlockSpec((1, D_MODEL), lambda i: (0, 0)),
        ],
        out_specs=pl.BlockSpec((tm, D_MODEL), lambda i: (i, 0)),
        out_shape=jax.ShapeDtypeStruct((N_TOK, D_MODEL), jnp.float32),
        compiler_params=pltpu.CompilerParams(
            dimension_semantics=("arbitrary",),
            vmem_limit_bytes=_vmem_limit(
                ((tm, D_MODEL), jnp.float32, 2 * (TOP_K + 2)), ((tm, LANES), jnp.float32, 2),
                ((3, D_MODEL, D_SHARED), jnp.bfloat16, 2), ((tm, D_MODEL), jnp.float32, 6))),
        name="combine",
    )(*([y] * TOP_K), wts, x1, s1, s3, s2, g, b)


def kernel(x, w_in, b_in, sinks, sgu_ln_g, sgu_ln_b, w_spatial, b_spatial, w_branch_attn,
           w_branch_sgu, w_out, ln1_g, ln1_b, w_router, router_bias, w1, w3, w2, ws1, ws3, ws2,
           ln2_g, ln2_b):
    assert x.shape == (BATCH, SEQ, D_MODEL) and w_in.shape == (1, D_MODEL, IN_W)
    x2d = x.reshape(N_TOK, D_MODEL)
    hkv, x_bf = _kv_proj(x2d, w_in[0], b_in)
    h = _in_proj(x_bf, w_in[0], b_in)
    attn, sgu = _mixers(h, hkv, sinks[0], sgu_ln_g[0], sgu_ln_b[0], w_spatial[0], b_spatial[0])
    x1, logits_t = _merge(attn, sgu, h, x2d, _bf16(w_branch_attn[0]), _bf16(w_branch_sgu[0]),
                          _bf16(w_out[0]), ln1_g, ln1_b, _router_parts(w_router[0]))
    idx_t, w_t = _route(logits_t, router_bias[0])
    y = _experts(x1, w1[0], w3[0], w2[0], *_dispatch_plan(idx_t))
    out = _combine(y, w_t.T, x1, _bf16(ws1[0]), _bf16(ws3[0]), _bf16(ws2[0]), ln2_g, ln2_b)
    return out.reshape(BATCH, SEQ, D_MODEL)
```

```python
import functools
import math

import numpy as np
import jax
import jax.numpy as jnp
from jax import lax
from jax.experimental import pallas as pl
from jax.experimental.pallas import tpu as pltpu

D_MODEL = 2048
BATCH = 2
SEQ = 4096
N_TOK = BATCH * SEQ
N_Q_HEADS = 32
N_KV_HEADS = 4
HEAD_DIM = 64
GQA = N_Q_HEADS // N_KV_HEADS
WINDOW = 128
ROPE_THETA = 500000.0
ROT_DIM = HEAD_DIM // 4
SGU_GROUPS = 8
SGU_CH = 128
N_EXPERTS = 64
N_EXPERT_GROUPS = 8
GROUP_SIZE = N_EXPERTS // N_EXPERT_GROUPS
TOPK_GROUPS = 4
TOP_K = 8
D_EXPERT = 512
D_SHARED = 512
ROUTED_SCALE = 2.5
ATTN_W = N_Q_HEADS * HEAD_DIM
KV_W = N_KV_HEADS * HEAD_DIM
SGU_W = SGU_GROUPS * SGU_CH
IN_W = ATTN_W + 2 * KV_W + 2 * SGU_W + 2 * D_MODEL
ALPHA = 2.0 ** 0.25
LN_EPS = 1e-5
N_ASSIGN = N_TOK * TOP_K

LANES = 128
V7X_VMEM_BYTES = 64 * 1024 * 1024


def _vmem_limit(*buffers):
    need = sum(copies * math.prod(shape) * jnp.dtype(dtype).itemsize
               for shape, dtype, copies in buffers)
    assert need <= V7X_VMEM_BYTES * 7 // 8, need
    return need

PROJ_TM = 1024
PROJ_TN = 1024
MERGE_TM = 256
ROUTE_TT = 512
EXPERT_BM = 128
N_GATHER_BUFS = 3
COMBINE_TM = 128
N_EXPERT_BLOCKS = -(-(N_ASSIGN + N_EXPERTS * (EXPERT_BM - 1)) // EXPERT_BM)
N_SLOTS = N_EXPERT_BLOCKS * EXPERT_BM

H_Q, H_GA, H_GB, H_U, H_VG = 0, 2048, 4096, 6144, 7168
H_W = 8192
PROJ_UNIT = 512
_SRC_UNIT = np.array([0, 1, 2, 3, 9, 10, 11, 12, 13, 14, 15, 16, 5, 6, 7, 8], np.int32)
_SRC_UNIT_KV = ATTN_W // PROJ_UNIT
_N_PROJ_TILES = H_W // PROJ_TN
_Q_TILES = ATTN_W // PROJ_TN
_GATE_END = H_U // PROJ_TN


def _bf16(a):
    return a.astype(jnp.bfloat16)


def _dot(a, b):
    return jnp.dot(a, b, preferred_element_type=jnp.float32)


def _dot_nt(a, b):
    return lax.dot_general(a, b, (((1,), (1,)), ((), ())), preferred_element_type=jnp.float32)


def _rope_slab(x, c, s_next, s_prev):
    return (x * c + pltpu.roll(x, LANES - ROT_DIM // 2, axis=1) * s_next
            + pltpu.roll(x, ROT_DIM // 2, axis=1) * s_prev)


def _in_proj_kernel(src_ref, x_ref, wa_ref, wb_ref, ba_ref, bb_ref, c_ref, sn_ref, sp_ref, o_ref,
                    wbf_ref):
    j = pl.program_id(0)
    i = pl.program_id(1)

    @pl.when(i == 0)
    def _():
        wbf_ref[:, :PROJ_UNIT] = _bf16(wa_ref[...])
        wbf_ref[:, PROJ_UNIT:] = _bf16(wb_ref[...])

    def project():
        bias = jnp.concatenate([ba_ref[...], bb_ref[...]], axis=1)
        return _dot(x_ref[...], wbf_ref[...]) + bias

    @pl.when(j < _Q_TILES)
    def _():
        acc = project()
        c, sn, sp = c_ref[...], sn_ref[...], sp_ref[...]
        scale = HEAD_DIM ** -0.5
        for t in range(PROJ_TN // LANES):
            sl = slice(t * LANES, (t + 1) * LANES)
            o_ref[:, sl] = _bf16(_rope_slab(acc[:, sl], c, sn, sp) * scale)

    @pl.when((j >= _Q_TILES) & (j < _GATE_END))
    def _():
        o_ref[...] = _bf16(jax.nn.sigmoid(project()))

    @pl.when(j >= _GATE_END)
    def _():
        o_ref[...] = _bf16(jax.nn.gelu(project()))


def _kv_proj_kernel(x_ref, w_ref, b_ref, c_ref, sn_ref, sp_ref, o_ref, xb_ref, wbf_ref):
    @pl.when(pl.program_id(0) == 0)
    def _():
        wbf_ref[...] = _bf16(w_ref[...])

    xb = _bf16(x_ref[...])
    xb_ref[...] = xb
    acc = _dot(xb, wbf_ref[...]) + b_ref[...]
    c, sn, sp = c_ref[...], sn_ref[...], sp_ref[...]
    for t in range(2 * KV_W // LANES):
        sl = slice(t * LANES, (t + 1) * LANES)
        if t < KV_W // LANES:
            o_ref[:, sl] = _bf16(_rope_slab(acc[:, sl], c, sn, sp))
        else:
            o_ref[:, sl] = _bf16(acc[:, sl])


def _rope_tables():
    half = ROT_DIM // 2
    inv_freq = ROPE_THETA ** (-np.arange(0, ROT_DIM, 2, dtype=np.float32) / ROT_DIM)
    pos = np.arange(SEQ, dtype=np.float32)
    ang = jnp.asarray(pos[:, None] * inv_freq[None, :].astype(np.float32), jnp.float32)
    cos, sin = jnp.cos(ang), jnp.sin(ang)
    ones = jnp.ones((SEQ, HEAD_DIM - ROT_DIM), jnp.float32)
    zeros = jnp.zeros((SEQ, HEAD_DIM - ROT_DIM), jnp.float32)
    zh = jnp.zeros((SEQ, half), jnp.float32)
    c = jnp.concatenate([cos, cos, ones], axis=1)
    s_next = jnp.concatenate([-sin, zh, zeros], axis=1)
    s_prev = jnp.concatenate([zh, sin, zeros], axis=1)
    rep = LANES // HEAD_DIM
    return tuple(jnp.tile(t, (1, rep)) for t in (c, s_next, s_prev))


def _in_proj(x_bf, w_in, b_in):
    c, sn, sp = _rope_tables()
    n_i = N_TOK // PROJ_TM
    pos_tiles = SEQ // PROJ_TM
    tbl = pl.BlockSpec((PROJ_TM, LANES), lambda j, i, src: (i % pos_tiles, 0))
    grid_spec = pltpu.PrefetchScalarGridSpec(
        num_scalar_prefetch=1,
        grid=(_N_PROJ_TILES, n_i),
        in_specs=[
            pl.BlockSpec((PROJ_TM, D_MODEL), lambda j, i, src: (i, 0)),
            pl.BlockSpec((D_MODEL, PROJ_UNIT), lambda j, i, src: (0, src[2 * j])),
            pl.BlockSpec((D_MODEL, PROJ_UNIT), lambda j, i, src: (0, src[2 * j + 1])),
            pl.BlockSpec((1, PROJ_UNIT), lambda j, i, src: (0, src[2 * j])),
            pl.BlockSpec((1, PROJ_UNIT), lambda j, i, src: (0, src[2 * j + 1])),
            tbl, tbl, tbl,
        ],
        out_specs=pl.BlockSpec((PROJ_TM, PROJ_TN), lambda j, i, src: (i, j)),
        scratch_shapes=[pltpu.VMEM((D_MODEL, PROJ_TN), jnp.bfloat16)],
    )
    return pl.pallas_call(
        _in_proj_kernel,
        grid_spec=grid_spec,
        out_shape=jax.ShapeDtypeStruct((N_TOK, H_W), jnp.bfloat16),
        compiler_params=pltpu.CompilerParams(
            dimension_semantics=("arbitrary", "arbitrary"),
            vmem_limit_bytes=_vmem_limit(
                ((PROJ_TM, D_MODEL), jnp.bfloat16, 2), ((D_MODEL, PROJ_TN), jnp.float32, 2),
                ((PROJ_TM, LANES), jnp.float32, 6), ((PROJ_TM, PROJ_TN), jnp.bfloat16, 2),
                ((D_MODEL, PROJ_TN), jnp.bfloat16, 1), ((PROJ_TM, PROJ_TN), jnp.float32, 3))),
        name="in_proj",
    )(jnp.asarray(_SRC_UNIT), x_bf, w_in, w_in, b_in, b_in, c, sn, sp)


def _kv_proj(x2d, w_in, b_in):
    c, sn, sp = _rope_tables()
    tm = PROJ_TM // 2
    pos_tiles = SEQ // tm
    tbl = pl.BlockSpec((tm, LANES), lambda i: (i % pos_tiles, 0))
    return pl.pallas_call(
        _kv_proj_kernel,
        grid=(N_TOK // tm,),
        in_specs=[
            pl.BlockSpec((tm, D_MODEL), lambda i: (i, 0)),
            pl.BlockSpec((D_MODEL, 2 * KV_W), lambda i: (0, _SRC_UNIT_KV)),
            pl.BlockSpec((1, 2 * KV_W), lambda i: (0, _SRC_UNIT_KV)),
            tbl, tbl, tbl,
        ],
        out_specs=[pl.BlockSpec((tm, 2 * KV_W), lambda i: (i, 0)),
                   pl.BlockSpec((tm, D_MODEL), lambda i: (i, 0))],
        out_shape=[jax.ShapeDtypeStruct((N_TOK, 2 * KV_W), jnp.bfloat16),
                   jax.ShapeDtypeStruct((N_TOK, D_MODEL), jnp.bfloat16)],
        scratch_shapes=[pltpu.VMEM((D_MODEL, 2 * KV_W), jnp.bfloat16)],
        compiler_params=pltpu.CompilerParams(
            dimension_semantics=("arbitrary",),
            vmem_limit_bytes=_vmem_limit(
                ((tm, D_MODEL), jnp.float32, 2), ((tm, D_MODEL), jnp.bfloat16, 3),
                ((D_MODEL, 2 * KV_W), jnp.float32, 2), ((D_MODEL, 2 * KV_W), jnp.bfloat16, 1),
                ((tm, LANES), jnp.float32, 6), ((tm, 2 * KV_W), jnp.float32, 3))),
        name="kv_proj",
    )(x2d, w_in, b_in, c, sn, sp)


def _mixers_kernel(sink_ref, q_ref, kvc_ref, kvp_ref, u_ref, vg_ref, lng_ref, lnb_ref,
                   ws_ref, bs_ref, attn_ref, sgu_ref):
    n = pl.program_id(0)
    w = WINDOW
    first_key = jnp.where((n % (SEQ // w)) == 0, w, 0)
    pairs = GQA // 2
    rows = pairs * w

    qi = lax.broadcasted_iota(jnp.int32, (w, 2 * w), 0)
    kj = lax.broadcasted_iota(jnp.int32, (w, 2 * w), 1)
    valid = (kj > qi) & (kj <= qi + w) & (kj >= first_key)
    valid = jnp.concatenate([valid] * pairs, axis=0)
    lane = lax.broadcasted_iota(jnp.int32, (2 * w, LANES), 1)
    low = lane < HEAD_DIM
    lane_r = lax.broadcasted_iota(jnp.int32, (rows, LANES), 1)
    low_r = lane_r < HEAD_DIM
    ones_low = jnp.where(low, 1.0, 0.0).astype(jnp.bfloat16)
    ones_high = jnp.where(low, 0.0, 1.0).astype(jnp.bfloat16)
    neg_inf = jnp.float32(-jnp.inf)

    kv = jnp.concatenate([kvp_ref[...], kvc_ref[...]], axis=0).astype(jnp.float32)

    def padded(group, head_is_high):
        rolled = pltpu.roll(group, HEAD_DIM, axis=1)
        if head_is_high:
            lo_half, hi_half = rolled, group
        else:
            lo_half, hi_half = group, rolled
        return (_bf16(jnp.where(low, lo_half, 0.0)), _bf16(jnp.where(low, 0.0, hi_half)))

    for h in range(N_KV_HEADS):
        g0 = (h // 2) * LANES
        k_lo, k_hi = padded(kv[:, g0:g0 + LANES], h % 2 == 1)
        v_lo, v_hi = padded(kv[:, KV_W + g0:KV_W + g0 + LANES], h % 2 == 1)
        r_even = jnp.concatenate([v_lo, ones_low], axis=1)
        r_odd = jnp.concatenate([v_hi, ones_high], axis=1)
        q4 = jnp.concatenate(
            [q_ref[:, (h * pairs + p) * LANES:(h * pairs + p + 1) * LANES] for p in range(pairs)],
            axis=0)
        sink_e = jnp.concatenate(
            [jnp.full((w, 1), sink_ref[h * GQA + 2 * p], jnp.float32) for p in range(pairs)], axis=0)
        sink_o = jnp.concatenate(
            [jnp.full((w, 1), sink_ref[h * GQA + 2 * p + 1], jnp.float32) for p in range(pairs)], axis=0)

        s_e = jnp.where(valid, _dot_nt(q4, k_lo), neg_inf)
        s_o = jnp.where(valid, _dot_nt(q4, k_hi), neg_inf)
        m_e = jnp.maximum(jnp.max(s_e, axis=1, keepdims=True), sink_e)
        m_o = jnp.maximum(jnp.max(s_o, axis=1, keepdims=True), sink_o)
        p_e = _bf16(jnp.exp(s_e - m_e))
        p_o = _bf16(jnp.exp(s_o - m_o))
        acc = _dot(p_e, r_even) + _dot(p_o, r_odd)
        sink_term = jnp.exp(jnp.where(low_r, sink_e - m_e, sink_o - m_o))
        out = acc[:, :LANES] / (acc[:, LANES:] + sink_term)
        for p in range(pairs):
            c0 = (h * pairs + p) * LANES
            attn_ref[:, c0:c0 + LANES] = _bf16(out[p * w:(p + 1) * w])

    ti = lax.broadcasted_iota(jnp.int32, (w, w), 0)
    si = lax.broadcasted_iota(jnp.int32, (w, w), 1)
    causal = si <= ti
    for g in range(SGU_GROUPS):
        sl = slice(g * SGU_CH, (g + 1) * SGU_CH)
        x = vg_ref[:, sl].astype(jnp.float32)
        mu = jnp.mean(x, axis=-1, keepdims=True)
        xc = x - mu
        var = jnp.mean(xc * xc, axis=-1, keepdims=True)
        vn = xc * lax.rsqrt(var + LN_EPS) * lng_ref[:, sl] + lnb_ref[:, sl]
        wsg = _bf16(jnp.where(causal, ws_ref[g], 0.0))
        sv = _dot(wsg, _bf16(vn)) + bs_ref[g]
        sgu_ref[:, sl] = _bf16(u_ref[:, sl].astype(jnp.float32) * sv)


def _mixers(h, hkv, sinks, ln_g, ln_b, w_s, b_s):
    w = WINDOW
    nb = N_TOK // w
    grid_spec = pltpu.PrefetchScalarGridSpec(
        num_scalar_prefetch=0,
        grid=(nb,),
        in_specs=[
            pl.BlockSpec(memory_space=pltpu.SMEM),
            pl.BlockSpec((w, ATTN_W), lambda n: (n, H_Q // ATTN_W)),
            pl.BlockSpec((w, 2 * KV_W), lambda n: (n, 0)),
            pl.BlockSpec((w, 2 * KV_W), lambda n: (jnp.maximum(n - 1, 0), 0)),
            pl.BlockSpec((w, SGU_W), lambda n: (n, H_U // SGU_W)),
            pl.BlockSpec((w, SGU_W), lambda n: (n, H_VG // SGU_W)),
            pl.BlockSpec((1, SGU_W), lambda n: (0, 0)),
            pl.BlockSpec((1, SGU_W), lambda n: (0, 0)),
            pl.BlockSpec((SGU_GROUPS, w, w), lambda n: (0, 0, 0)),
            pl.BlockSpec((SGU_GROUPS, w, 1), lambda n: (0, 0, 0)),
        ],
        out_specs=[
            pl.BlockSpec((w, ATTN_W), lambda n: (n, 0)),
            pl.BlockSpec((w, SGU_W), lambda n: (n, 0)),
        ],
    )
    return pl.pallas_call(
        _mixers_kernel,
        grid_spec=grid_spec,
        out_shape=[jax.ShapeDtypeStruct((N_TOK, ATTN_W), jnp.bfloat16),
                   jax.ShapeDtypeStruct((N_TOK, SGU_W), jnp.bfloat16)],
        compiler_params=pltpu.CompilerParams(
            dimension_semantics=("arbitrary",),
            vmem_limit_bytes=_vmem_limit(
                ((w, ATTN_W), jnp.bfloat16, 4), ((w, 2 * KV_W), jnp.bfloat16, 4),
                ((w, SGU_W), jnp.bfloat16, 6), ((SGU_GROUPS, w, w), jnp.float32, 2),
                ((SGU_GROUPS, w, LANES), jnp.float32, 2),
                ((GQA // 2 * w, 2 * w), jnp.float32, 16))),
        name="mixers",
    )(sinks, h, hkv, hkv, h, h, ln_g.reshape(1, SGU_W), ln_b.reshape(1, SGU_W), w_s,
      b_s.reshape(SGU_GROUPS, w, 1))


def _layer_norm(z, g, b):
    mu = jnp.mean(z, axis=-1, keepdims=True)
    zc = z - mu
    var = jnp.mean(zc * zc, axis=-1, keepdims=True)
    return zc * lax.rsqrt(var + LN_EPS) * g + b


def _merge_kernel(attn_ref, sgu_ref, ga_ref, gb_ref, x_ref, wa_ref, wb_ref, wo_ref, g_ref, b_ref,
                  wr_ref, x1_ref, lg_ref):
    mix = (ga_ref[...].astype(jnp.float32) * _dot(attn_ref[...], wa_ref[...])
           + gb_ref[...].astype(jnp.float32) * _dot(sgu_ref[...], wb_ref[...]))
    z = ALPHA * x_ref[...] + _dot(_bf16(mix), wo_ref[...])
    x1 = _layer_norm(z, g_ref[...], b_ref[...])
    x1_ref[...] = x1
    x_hi = _bf16(x1)
    x_lo = _bf16(x1 - x_hi.astype(jnp.float32))
    parts = _dot(x_hi, wr_ref[...]) + _dot(x_lo, wr_ref[...])
    lg_ref[...] = parts + pltpu.roll(parts, N_EXPERTS, axis=1)


def _router_parts(w_router):
    hi = _bf16(w_router)
    lo = _bf16(w_router - hi.astype(jnp.float32))
    return jnp.concatenate([hi, lo], axis=1)


def _merge(attn, sgu, h, x2d, wa, wb, wo, g, b, wr):
    tm = MERGE_TM
    resident = pl.Buffered(1)
    grid_spec = pltpu.PrefetchScalarGridSpec(
        num_scalar_prefetch=0,
        grid=(N_TOK // tm,),
        in_specs=[
            pl.BlockSpec((tm, ATTN_W), lambda i: (i, 0)),
            pl.BlockSpec((tm, SGU_W), lambda i: (i, 0)),
            pl.BlockSpec((tm, D_MODEL), lambda i: (i, H_GA // D_MODEL)),
            pl.BlockSpec((tm, D_MODEL), lambda i: (i, H_GB // D_MODEL)),
            pl.BlockSpec((tm, D_MODEL), lambda i: (i, 0)),
            pl.BlockSpec((ATTN_W, D_MODEL), lambda i: (0, 0), pipeline_mode=resident),
            pl.BlockSpec((SGU_W, D_MODEL), lambda i: (0, 0), pipeline_mode=resident),
            pl.BlockSpec((D_MODEL, D_MODEL), lambda i: (0, 0), pipeline_mode=resident),
            pl.BlockSpec((1, D_MODEL), lambda i: (0, 0)),
            pl.BlockSpec((1, D_MODEL), lambda i: (0, 0)),
            pl.BlockSpec((D_MODEL, 2 * N_EXPERTS), lambda i: (0, 0)),
        ],
        out_specs=[
            pl.BlockSpec((tm, D_MODEL), lambda i: (i, 0)),
            pl.BlockSpec((tm, 2 * N_EXPERTS), lambda i: (i, 0)),
        ],
    )
    x1, lg = pl.pallas_call(
        _merge_kernel,
        grid_spec=grid_spec,
        out_shape=[jax.ShapeDtypeStruct((N_TOK, D_MODEL), jnp.float32),
                   jax.ShapeDtypeStruct((N_TOK, 2 * N_EXPERTS), jnp.float32)],
        compiler_params=pltpu.CompilerParams(
            dimension_semantics=("arbitrary",),
            vmem_limit_bytes=_vmem_limit(
                ((tm, ATTN_W), jnp.bfloat16, 2), ((tm, SGU_W), jnp.bfloat16, 2),
                ((tm, D_MODEL), jnp.bfloat16, 4), ((tm, D_MODEL), jnp.float32, 4),
                ((ATTN_W + SGU_W + D_MODEL, D_MODEL), jnp.bfloat16, 1),
                ((D_MODEL, 2 * N_EXPERTS), jnp.bfloat16, 2),
                ((tm, D_MODEL), jnp.float32, 8))),
        name="merge",
    )(attn, sgu, h, h, x2d, wa, wb, wo, g, b, wr)
    return x1, lg[:, :N_EXPERTS].T


def _first_argmax(v, rows):
    m = jnp.max(v, axis=0, keepdims=True)
    i = jnp.min(jnp.where(v == m, rows, float(v.shape[0])), axis=0, keepdims=True)
    return m, i


def _row_index(shape):
    return lax.broadcasted_iota(jnp.int32, shape, 0).astype(jnp.float32)


def _route_kernel(lg_ref, bias_ref, idx_ref, w_ref):
    tt = lg_ref.shape[1]
    neg_inf = jnp.float32(-jnp.inf)
    scores = jax.nn.sigmoid(lg_ref[...])
    biased = scores + bias_ref[...]
    row_g = _row_index((GROUP_SIZE, tt))
    gs = []
    for g in range(N_EXPERT_GROUPS):
        blk = biased[g * GROUP_SIZE:(g + 1) * GROUP_SIZE]
        m1, i1 = _first_argmax(blk, row_g)
        m2 = jnp.max(jnp.where(row_g == i1, neg_inf, blk), axis=0, keepdims=True)
        gs.append(m1 + m2)
    cur = jnp.concatenate(gs, axis=0)
    row_n = _row_index((N_EXPERT_GROUPS, tt))
    sel = jnp.zeros((N_EXPERT_GROUPS, tt), jnp.float32)
    for _ in range(TOPK_GROUPS):
        _, i = _first_argmax(cur, row_n)
        hit = row_n == i
        sel = jnp.where(hit, 1.0, sel)
        cur = jnp.where(hit, neg_inf, cur)
    emask = jnp.concatenate(
        [jnp.broadcast_to(sel[g:g + 1], (GROUP_SIZE, tt)) for g in range(N_EXPERT_GROUPS)], axis=0)
    masked = jnp.where(emask > 0.5, biased, neg_inf)
    row_e = _row_index((N_EXPERTS, tt))
    idx_rows, w_rows = [], []
    for _ in range(TOP_K):
        _, i = _first_argmax(masked, row_e)
        hit = row_e == i
        w_rows.append(jnp.sum(jnp.where(hit, scores, 0.0), axis=0, keepdims=True))
        idx_rows.append(i)
        masked = jnp.where(hit, neg_inf, masked)
    wsel = jnp.concatenate(w_rows, axis=0)
    idx_ref[...] = jnp.concatenate(idx_rows, axis=0).astype(jnp.int32)
    w_ref[...] = wsel / (jnp.sum(wsel, axis=0, keepdims=True) + 1e-20) * ROUTED_SCALE


def _route(logits_t, bias):
    tt = ROUTE_TT
    return pl.pallas_call(
        _route_kernel,
        grid=(N_TOK // tt,),
        in_specs=[pl.BlockSpec((N_EXPERTS, tt), lambda i: (0, i)),
                  pl.BlockSpec((N_EXPERTS, 1), lambda i: (0, 0))],
        out_specs=[pl.BlockSpec((TOP_K, tt), lambda i: (0, i)),
                   pl.BlockSpec((TOP_K, tt), lambda i: (0, i))],
        out_shape=[jax.ShapeDtypeStruct((TOP_K, N_TOK), jnp.int32),
                   jax.ShapeDtypeStruct((TOP_K, N_TOK), jnp.float32)],
        compiler_params=pltpu.CompilerParams(dimension_semantics=("arbitrary",)),
        name="route",
    )(logits_t, bias.reshape(N_EXPERTS, 1))


def _dispatch_plan(idx_t):
    bm = EXPERT_BM
    nb = N_EXPERT_BLOCKS
    flat_e = idx_t.reshape(-1)
    counts = jnp.sum((flat_e[:, None] == jnp.arange(N_EXPERTS, dtype=jnp.int32)[None, :])
                     .astype(jnp.int32), axis=0)
    e_ids = jnp.arange(N_EXPERTS, dtype=jnp.int32)
    upto = (e_ids[:, None] <= e_ids[None, :]).astype(jnp.int32)
    padded = (counts + bm - 1) // bm * bm
    pad_end = jnp.sum(padded[:, None] * upto, axis=0)
    fill_end = jnp.sum((padded - counts)[:, None] * upto, axis=0)
    n_fill = N_SLOTS - N_ASSIGN
    fill_key = jnp.sum((jnp.arange(n_fill, dtype=jnp.int32)[:, None] >= fill_end[None, :])
                       .astype(jnp.int32), axis=1)
    keys = jnp.concatenate([flat_e, fill_key])
    pos_bits = (N_SLOTS - 1).bit_length()
    slot = jnp.arange(N_SLOTS, dtype=jnp.int32)
    src = jnp.sort((keys << pos_bits) | slot) & ((1 << pos_bits) - 1)
    real = src < N_ASSIGN
    dst = jnp.where(real, src, N_ASSIGN + (slot & (2 * bm - 1)))
    dump_block = N_ASSIGN + jnp.arange(bm, dtype=jnp.int32)
    dst = jnp.concatenate([dump_block, dst, dst[-bm:]]).reshape(nb + 2, 1, bm)
    n_used = jnp.sum(jnp.any(real.reshape(nb, bm), axis=1).astype(jnp.int32))
    block_start = jnp.arange(nb + 1, dtype=jnp.int32) * bm
    block_e = jnp.sum((block_start[:, None] >= pad_end[None, :]).astype(jnp.int32), axis=1)
    block_e = jnp.minimum(block_e, N_EXPERTS - 1).astype(jnp.int32)
    prev_e = jnp.concatenate([jnp.full((1,), -1, jnp.int32), block_e[:-1]])
    fresh = (block_e != prev_e).astype(jnp.int32)
    later = (counts > 0)[None, :] & (e_ids[None, :] > block_e[:, None])
    next_e = jnp.min(jnp.where(later, e_ids[None, :], N_EXPERTS), axis=1).astype(jnp.int32)
    return dst, n_used.reshape(1), block_e, fresh, next_e


def _experts_kernel(nused_ref, be_ref, fresh_ref, nxt_ref, dstp_ref, dstc_ref, dstn_ref, dstnn_ref,
                    x_hbm, w1_hbm, w3_hbm, w2_hbm, y_hbm, xbuf, obuf, w1s, w3s, w2s, w1b, w3b, w2b,
                    gsem, ssem, wsem):
    b = pl.program_id(0)
    bm = EXPERT_BM
    slot = b % 2
    gslot = lax.rem(b, N_GATHER_BUFS)
    n_used = nused_ref[0]

    def start_gather(idx_ref, s):
        for r in range(bm):
            tok = idx_ref[0, 0, r] & (N_TOK - 1)
            pltpu.make_async_copy(x_hbm.at[pl.ds(tok, 1)], xbuf.at[s, pl.ds(r, 1)],
                                  gsem.at[s]).start()

    def wait_gather(s):
        pltpu.make_async_copy(x_hbm.at[pl.ds(0, bm)], xbuf.at[s], gsem.at[s]).wait()

    def start_scatter_prev(s):
        for r in range(bm):
            row = dstp_ref[0, 0, r]
            pltpu.make_async_copy(obuf.at[s, pl.ds(r, 1)], y_hbm.at[pl.ds(row, 1)],
                                  ssem.at[s]).start()

    def wait_scatter(s):
        pltpu.make_async_copy(obuf.at[s], y_hbm.at[pl.ds(0, bm)], ssem.at[s]).wait()

    def weight_copies(e):
        return (pltpu.make_async_copy(w1_hbm.at[e], w1s, wsem.at[0]),
                pltpu.make_async_copy(w3_hbm.at[e], w3s, wsem.at[1]),
                pltpu.make_async_copy(w2_hbm.at[e], w2s, wsem.at[2]))

    def block_step(w1v, w3v, w2v):
        start_scatter_prev(1 - slot)
        xb = _bf16(xbuf[gslot])
        a = _bf16(jax.nn.silu(_dot(xb, w1v)) * _dot(xb, w3v))
        start_gather(dstnn_ref, lax.rem(b + 2, N_GATHER_BUFS))
        obuf[slot] = _dot(a, w2v)

    @pl.when(b == 0)
    def _():
        for c in weight_copies(be_ref[0]):
            c.start(priority=1)
        obuf[1] = jnp.zeros(obuf.shape[1:], obuf.dtype)
        fill = pltpu.make_async_copy(obuf.at[1], y_hbm.at[pl.ds(N_ASSIGN + bm, bm)], ssem.at[0])
        fill.start()
        fill.wait()
        start_gather(dstc_ref, 0)
        start_gather(dstn_ref, 1)

    @pl.when((b >= 1) & (b <= n_used))
    def _():
        wait_scatter(slot)

    @pl.when(b < n_used)
    def _():
        wait_gather(gslot)

        @pl.when(fresh_ref[b] == 1)
        def _():
            for c in weight_copies(be_ref[b]):
                c.wait()
            w1v, w3v, w2v = _bf16(w1s[...]), _bf16(w3s[...]), _bf16(w2s[...])
            w1b[...] = w1v
            w3b[...] = w3v
            w2b[...] = w2v
            block_step(w1v, w3v, w2v)

            @pl.when(nxt_ref[b] < N_EXPERTS)
            def _():
                for c in weight_copies(nxt_ref[b]):
                    c.start(priority=1)

        @pl.when(fresh_ref[b] == 0)
        def _():
            block_step(w1b[...], w3b[...], w2b[...])

    @pl.when(b == n_used)
    def _():
        wait_gather(gslot)
        wait_gather(lax.rem(b + 1, N_GATHER_BUFS))
        start_scatter_prev(1 - slot)
        wait_scatter(1 - slot)


def _experts(x1, w1, w3, w2, dst, n_used, block_e, fresh, next_e):
    bm = EXPERT_BM
    nb = N_EXPERT_BLOCKS
    smem_blk = lambda f: pl.BlockSpec((1, 1, bm), f, memory_space=pltpu.SMEM)
    hbm = pl.BlockSpec(memory_space=pl.ANY)
    grid_spec = pltpu.PrefetchScalarGridSpec(
        num_scalar_prefetch=4,
        grid=(nb + 1,),
        in_specs=[
            smem_blk(lambda b, *_: (b, 0, 0)),
            smem_blk(lambda b, *_: (b + 1, 0, 0)),
            smem_blk(lambda b, *_: (jnp.minimum(b + 2, nb + 1), 0, 0)),
            smem_blk(lambda b, *_: (jnp.minimum(b + 3, nb + 1), 0, 0)),
            hbm, hbm, hbm, hbm,
        ],
        out_specs=hbm,
        scratch_shapes=[
            pltpu.VMEM((N_GATHER_BUFS, bm, D_MODEL), jnp.float32),
            pltpu.VMEM((2, bm, D_MODEL), jnp.float32),
            pltpu.VMEM((D_MODEL, D_EXPERT), jnp.float32),
            pltpu.VMEM((D_MODEL, D_EXPERT), jnp.float32),
            pltpu.VMEM((D_EXPERT, D_MODEL), jnp.float32),
            pltpu.VMEM((D_MODEL, D_EXPERT), jnp.bfloat16),
            pltpu.VMEM((D_MODEL, D_EXPERT), jnp.bfloat16),
            pltpu.VMEM((D_EXPERT, D_MODEL), jnp.bfloat16),
            pltpu.SemaphoreType.DMA((N_GATHER_BUFS,)),
            pltpu.SemaphoreType.DMA((2,)),
            pltpu.SemaphoreType.DMA((3,)),
        ],
    )
    return pl.pallas_call(
        _experts_kernel,
        grid_spec=grid_spec,
        out_shape=jax.ShapeDtypeStruct((N_ASSIGN + 2 * bm, D_MODEL), jnp.float32),
        compiler_params=pltpu.CompilerParams(
            dimension_semantics=("arbitrary",),
            vmem_limit_bytes=_vmem_limit(
                ((N_GATHER_BUFS + 2, bm, D_MODEL), jnp.float32, 1),
                ((3, D_MODEL, D_EXPERT), jnp.float32, 1),
                ((3, D_MODEL, D_EXPERT), jnp.bfloat16, 2),
                ((bm, D_MODEL), jnp.float32, 2), ((bm, D_EXPERT), jnp.float32, 4))),
        name="experts",
    )(n_used, block_e, fresh, next_e, dst, dst, dst, dst, x1, w1, w3, w2)


def _combine_kernel(*refs):
    y_refs = refs[:TOP_K]
    w_ref, x_ref, s1_ref, s3_ref, s2_ref, g_ref, b_ref, o_ref = refs[TOP_K:]
    x1 = x_ref[...]
    xb = _bf16(x1)
    a = _bf16(jax.nn.silu(_dot(xb, s1_ref[...])) * _dot(xb, s3_ref[...]))
    ffn = _dot(a, s2_ref[...])
    wts = w_ref[...]
    routed = wts[:, 0:1] * y_refs[0][...]
    for k in range(1, TOP_K):
        routed = routed + wts[:, k:k + 1] * y_refs[k][...]
    o_ref[...] = _layer_norm(ALPHA * x1 + (routed + ffn), g_ref[...], b_ref[...])


def _combine(y, wts, x1, s1, s3, s2, g, b):
    tm = COMBINE_TM
    nt = N_TOK // tm
    y_specs = [pl.BlockSpec((tm, D_MODEL), functools.partial(lambda i, k: (k * nt + i, 0), k=k))
               for k in range(TOP_K)]
    return pl.pallas_call(
        _combine_kernel,
        grid=(nt,),
        in_specs=y_specs + [
            pl.BlockSpec((tm, TOP_K), lambda i: (i, 0)),
            pl.BlockSpec((tm, D_MODEL), lambda i: (i, 0)),
            pl.BlockSpec((D_MODEL, D_SHARED), lambda i: (0, 0)),
            pl.BlockSpec((D_MODEL, D_SHARED), lambda i: (0, 0)),
            pl.BlockSpec((D_SHARED, D_MODEL), lambda i: (0, 0)),
            pl.BlockSpec((1, D_MODEL), lambda i: (0, 0)),
            pl.BlockSpec((1, D_MODEL), lambda i: (0, 0)),
        ],
        out_specs=pl.BlockSpec((tm, D_MODEL), lambda i: (i, 0)),
        out_shape=jax.ShapeDtypeStruct((N_TOK, D_MODEL), jnp.float32),
        compiler_params=pltpu.CompilerParams(
            dimension_semantics=("arbitrary",),
            vmem_limit_bytes=_vmem_limit(
                ((tm, D_MODEL), jnp.float32, 2 * (TOP_K + 2)), ((tm, LANES), jnp.float32, 2),
                ((3, D_MODEL, D_SHARED), jnp.bfloat16, 2), ((tm, D_MODEL), jnp.float32, 6))),
        name="combine",
    )(*([y] * TOP_K), wts, x1, s1, s3, s2, g, b)


def kernel(x, w_in, b_in, sinks, sgu_ln_g, sgu_ln_b, w_spatial, b_spatial, w_branch_attn,
           w_branch_sgu, w_out, ln1_g, ln1_b, w_router, router_bias, w1, w3, w2, ws1, ws3, ws2,
           ln2_g, ln2_b):
    assert x.shape == (BATCH, SEQ, D_MODEL) and w_in.shape == (1, D_MODEL, IN_W)
    x2d = x.reshape(N_TOK, D_MODEL)
    hkv, x_bf = _kv_proj(x2d, w_in[0], b_in)
    h = _in_proj(x_bf, w_in[0], b_in)
    attn, sgu = _mixers(h, hkv, sinks[0], sgu_ln_g[0], sgu_ln_b[0], w_spatial[0], b_spatial[0])
    x1, logits_t = _merge(attn, sgu, h, x2d, _bf16(w_branch_attn[0]), _bf16(w_branch_sgu[0]),
                          _bf16(w_out[0]), ln1_g, ln1_b, _router_parts(w_router[0]))
    idx_t, w_t = _route(logits_t, router_bias[0])
    y = _experts(x1, w1[0], w3[0], w2[0], *_dispatch_plan(idx_t))
    out = _combine(y, w_t.T, x1, _bf16(ws1[0]), _bf16(ws3[0]), _bf16(ws2[0]), ln2_g, ln2_b)
    return out.reshape(BATCH, SEQ, D_MODEL)
```

```python
import functools
import math

import numpy as np
import jax
import jax.numpy as jnp
from jax import lax
from jax.experimental import pallas as pl
from jax.experimental.pallas import tpu as pltpu

D_MODEL = 2048
BATCH = 2
SEQ = 4096
N_TOK = BATCH * SEQ
N_Q_HEADS = 32
N_KV_HEADS = 4
HEAD_DIM = 64
GQA = N_Q_HEADS // N_KV_HEADS
WINDOW = 128
ROPE_THETA = 500000.0
ROT_DIM = HEAD_DIM // 4
SGU_GROUPS = 8
SGU_CH = 128
N_EXPERTS = 64
N_EXPERT_GROUPS = 8
GROUP_SIZE = N_EXPERTS // N_EXPERT_GROUPS
TOPK_GROUPS = 4
TOP_K = 8
D_EXPERT = 512
D_SHARED = 512
ROUTED_SCALE = 2.5
ATTN_W = N_Q_HEADS * HEAD_DIM
KV_W = N_KV_HEADS * HEAD_DIM
SGU_W = SGU_GROUPS * SGU_CH
IN_W = ATTN_W + 2 * KV_W + 2 * SGU_W + 2 * D_MODEL
ALPHA = 2.0 ** 0.25
LN_EPS = 1e-5
N_ASSIGN = N_TOK * TOP_K

LANES = 128
V7X_VMEM_BYTES = 64 * 1024 * 1024


def _vmem_limit(*buffers):
    need = sum(copies * math.prod(shape) * jnp.dtype(dtype).itemsize
               for shape, dtype, copies in buffers)
    assert need <= V7X_VMEM_BYTES * 7 // 8, need
    return need

PROJ_TM = 1024
PROJ_TN = 1024
MERGE_TM = 256
ROUTE_TT = 512
EXPERT_BM = 256
N_GATHER_BUFS = 3
COMBINE_TM = 128
N_EXPERT_BLOCKS = -(-(N_ASSIGN + N_EXPERTS * (EXPERT_BM - 1)) // EXPERT_BM)
N_SLOTS = N_EXPERT_BLOCKS * EXPERT_BM

H_Q, H_GA, H_GB, H_U, H_VG = 0, 2048, 4096, 6144, 7168
H_W = 8192
PROJ_UNIT = 512
_SRC_UNIT = np.array([0, 1, 2, 3, 9, 10, 11, 12, 13, 14, 15, 16, 5, 6, 7, 8], np.int32)
_SRC_UNIT_KV = ATTN_W // PROJ_UNIT
_N_PROJ_TILES = H_W // PROJ_TN
_Q_TILES = ATTN_W // PROJ_TN
_GATE_END = H_U // PROJ_TN


def _bf16(a):
    return a.astype(jnp.bfloat16)


def _dot(a, b):
    return jnp.dot(a, b, preferred_element_type=jnp.float32)


def _dot_nt(a, b):
    return lax.dot_general(a, b, (((1,), (1,)), ((), ())), preferred_element_type=jnp.float32)


def _rope_slab(x, c, s_next, s_prev):
    return (x * c + pltpu.roll(x, LANES - ROT_DIM // 2, axis=1) * s_next
            + pltpu.roll(x, ROT_DIM // 2, axis=1) * s_prev)


def _in_proj_kernel(src_ref, x_ref, wa_ref, wb_ref, ba_ref, bb_ref, c_ref, sn_ref, sp_ref, o_ref,
                    wbf_ref):
    j = pl.program_id(0)
    i = pl.program_id(1)

    @pl.when(i == 0)
    def _():
        wbf_ref[:, :PROJ_UNIT] = _bf16(wa_ref[...])
        wbf_ref[:, PROJ_UNIT:] = _bf16(wb_ref[...])

    def project():
        bias = jnp.concatenate([ba_ref[...], bb_ref[...]], axis=1)
        return _dot(x_ref[...], wbf_ref[...]) + bias

    @pl.when(j < _Q_TILES)
    def _():
        acc = project()
        c, sn, sp = c_ref[...], sn_ref[...], sp_ref[...]
        scale = HEAD_DIM ** -0.5
        for t in range(PROJ_TN // LANES):
            sl = slice(t * LANES, (t + 1) * LANES)
            o_ref[:, sl] = _bf16(_rope_slab(acc[:, sl], c, sn, sp) * scale)

    @pl.when((j >= _Q_TILES) & (j < _GATE_END))
    def _():
        o_ref[...] = _bf16(jax.nn.sigmoid(project()))

    @pl.when(j >= _GATE_END)
    def _():
        o_ref[...] = _bf16(jax.nn.gelu(project()))


def _kv_proj_kernel(x_ref, w_ref, b_ref, c_ref, sn_ref, sp_ref, o_ref, xb_ref, wbf_ref):
    @pl.when(pl.program_id(0) == 0)
    def _():
        wbf_ref[...] = _bf16(w_ref[...])

    xb = _bf16(x_ref[...])
    xb_ref[...] = xb
    acc = _dot(xb, wbf_ref[...]) + b_ref[...]
    c, sn, sp = c_ref[...], sn_ref[...], sp_ref[...]
    for t in range(2 * KV_W // LANES):
        sl = slice(t * LANES, (t + 1) * LANES)
        if t < KV_W // LANES:
            o_ref[:, sl] = _bf16(_rope_slab(acc[:, sl], c, sn, sp))
        else:
            o_ref[:, sl] = _bf16(acc[:, sl])


def _rope_tables():
    half = ROT_DIM // 2
    inv_freq = ROPE_THETA ** (-np.arange(0, ROT_DIM, 2, dtype=np.float32) / ROT_DIM)
    pos = np.arange(SEQ, dtype=np.float32)
    ang = jnp.asarray(pos[:, None] * inv_freq[None, :].astype(np.float32), jnp.float32)
    cos, sin = jnp.cos(ang), jnp.sin(ang)
    ones = jnp.ones((SEQ, HEAD_DIM - ROT_DIM), jnp.float32)
    zeros = jnp.zeros((SEQ, HEAD_DIM - ROT_DIM), jnp.float32)
    zh = jnp.zeros((SEQ, half), jnp.float32)
    c = jnp.concatenate([cos, cos, ones], axis=1)
    s_next = jnp.concatenate([-sin, zh, zeros], axis=1)
    s_prev = jnp.concatenate([zh, sin, zeros], axis=1)
    rep = LANES // HEAD_DIM
    return tuple(jnp.tile(t, (1, rep)) for t in (c, s_next, s_prev))


def _in_proj(x_bf, w_in, b_in):
    c, sn, sp = _rope_tables()
    n_i = N_TOK // PROJ_TM
    pos_tiles = SEQ // PROJ_TM
    tbl = pl.BlockSpec((PROJ_TM, LANES), lambda j, i, src: (i % pos_tiles, 0))
    grid_spec = pltpu.PrefetchScalarGridSpec(
        num_scalar_prefetch=1,
        grid=(_N_PROJ_TILES, n_i),
        in_specs=[
            pl.BlockSpec((PROJ_TM, D_MODEL), lambda j, i, src: (i, 0)),
            pl.BlockSpec((D_MODEL, PROJ_UNIT), lambda j, i, src: (0, src[2 * j])),
            pl.BlockSpec((D_MODEL, PROJ_UNIT), lambda j, i, src: (0, src[2 * j + 1])),
            pl.BlockSpec((1, PROJ_UNIT), lambda j, i, src: (0, src[2 * j])),
            pl.BlockSpec((1, PROJ_UNIT), lambda j, i, src: (0, src[2 * j + 1])),
            tbl, tbl, tbl,
        ],
        out_specs=pl.BlockSpec((PROJ_TM, PROJ_TN), lambda j, i, src: (i, j)),
        scratch_shapes=[pltpu.VMEM((D_MODEL, PROJ_TN), jnp.bfloat16)],
    )
    return pl.pallas_call(
        _in_proj_kernel,
        grid_spec=grid_spec,
        out_shape=jax.ShapeDtypeStruct((N_TOK, H_W), jnp.bfloat16),
        compiler_params=pltpu.CompilerParams(
            dimension_semantics=("arbitrary", "arbitrary"),
            vmem_limit_bytes=_vmem_limit(
                ((PROJ_TM, D_MODEL), jnp.bfloat16, 2), ((D_MODEL, PROJ_TN), jnp.float32, 2),
                ((PROJ_TM, LANES), jnp.float32, 6), ((PROJ_TM, PROJ_TN), jnp.bfloat16, 2),
                ((D_MODEL, PROJ_TN), jnp.bfloat16, 1), ((PROJ_TM, PROJ_TN), jnp.float32, 3))),
        name="in_proj",
    )(jnp.asarray(_SRC_UNIT), x_bf, w_in, w_in, b_in, b_in, c, sn, sp)


def _kv_proj(x2d, w_in, b_in):
    c, sn, sp = _rope_tables()
    tm = PROJ_TM // 2
    pos_tiles = SEQ // tm
    tbl = pl.BlockSpec((tm, LANES), lambda i: (i % pos_tiles, 0))
    return pl.pallas_call(
        _kv_proj_kernel,
        grid=(N_TOK // tm,),
        in_specs=[
            pl.BlockSpec((tm, D_MODEL), lambda i: (i, 0)),
            pl.BlockSpec((D_MODEL, 2 * KV_W), lambda i: (0, _SRC_UNIT_KV)),
            pl.BlockSpec((1, 2 * KV_W), lambda i: (0, _SRC_UNIT_KV)),
            tbl, tbl, tbl,
        ],
        out_specs=[pl.BlockSpec((tm, 2 * KV_W), lambda i: (i, 0)),
                   pl.BlockSpec((tm, D_MODEL), lambda i: (i, 0))],
        out_shape=[jax.ShapeDtypeStruct((N_TOK, 2 * KV_W), jnp.bfloat16),
                   jax.ShapeDtypeStruct((N_TOK, D_MODEL), jnp.bfloat16)],
        scratch_shapes=[pltpu.VMEM((D_MODEL, 2 * KV_W), jnp.bfloat16)],
        compiler_params=pltpu.CompilerParams(
            dimension_semantics=("arbitrary",),
            vmem_limit_bytes=_vmem_limit(
                ((tm, D_MODEL), jnp.float32, 2), ((tm, D_MODEL), jnp.bfloat16, 3),
                ((D_MODEL, 2 * KV_W), jnp.float32, 2), ((D_MODEL, 2 * KV_W), jnp.bfloat16, 1),
                ((tm, LANES), jnp.float32, 6), ((tm, 2 * KV_W), jnp.float32, 3))),
        name="kv_proj",
    )(x2d, w_in, b_in, c, sn, sp)


def _mixers_kernel(sink_ref, q_ref, kvc_ref, kvp_ref, u_ref, vg_ref, lng_ref, lnb_ref,
                   ws_ref, bs_ref, attn_ref, sgu_ref):
    n = pl.program_id(0)
    w = WINDOW
    first_key = jnp.where((n % (SEQ // w)) == 0, w, 0)
    pairs = GQA // 2
    rows = pairs * w

    qi = lax.broadcasted_iota(jnp.int32, (w, 2 * w), 0)
    kj = lax.broadcasted_iota(jnp.int32, (w, 2 * w), 1)
    valid = (kj > qi) & (kj <= qi + w) & (kj >= first_key)
    valid = jnp.concatenate([valid] * pairs, axis=0)
    lane = lax.broadcasted_iota(jnp.int32, (2 * w, LANES), 1)
    low = lane < HEAD_DIM
    lane_r = lax.broadcasted_iota(jnp.int32, (rows, LANES), 1)
    low_r = lane_r < HEAD_DIM
    ones_low = jnp.where(low, 1.0, 0.0).astype(jnp.bfloat16)
    ones_high = jnp.where(low, 0.0, 1.0).astype(jnp.bfloat16)
    neg_inf = jnp.float32(-jnp.inf)

    kv = jnp.concatenate([kvp_ref[...], kvc_ref[...]], axis=0).astype(jnp.float32)

    def padded(group, head_is_high):
        rolled = pltpu.roll(group, HEAD_DIM, axis=1)
        if head_is_high:
            lo_half, hi_half = rolled, group
        else:
            lo_half, hi_half = group, rolled
        return (_bf16(jnp.where(low, lo_half, 0.0)), _bf16(jnp.where(low, 0.0, hi_half)))

    for h in range(N_KV_HEADS):
        g0 = (h // 2) * LANES
        k_lo, k_hi = padded(kv[:, g0:g0 + LANES], h % 2 == 1)
        v_lo, v_hi = padded(kv[:, KV_W + g0:KV_W + g0 + LANES], h % 2 == 1)
        r_even = jnp.concatenate([v_lo, ones_low], axis=1)
        r_odd = jnp.concatenate([v_hi, ones_high], axis=1)
        q4 = jnp.concatenate(
            [q_ref[:, (h * pairs + p) * LANES:(h * pairs + p + 1) * LANES] for p in range(pairs)],
            axis=0)
        sink_e = jnp.concatenate(
            [jnp.full((w, 1), sink_ref[h * GQA + 2 * p], jnp.float32) for p in range(pairs)], axis=0)
        sink_o = jnp.concatenate(
            [jnp.full((w, 1), sink_ref[h * GQA + 2 * p + 1], jnp.float32) for p in range(pairs)], axis=0)

        s_e = jnp.where(valid, _dot_nt(q4, k_lo), neg_inf)
        s_o = jnp.where(valid, _dot_nt(q4, k_hi), neg_inf)
        m_e = jnp.maximum(jnp.max(s_e, axis=1, keepdims=True), sink_e)
        m_o = jnp.maximum(jnp.max(s_o, axis=1, keepdims=True), sink_o)
        p_e = _bf16(jnp.exp(s_e - m_e))
        p_o = _bf16(jnp.exp(s_o - m_o))
        acc = _dot(p_e, r_even) + _dot(p_o, r_odd)
        sink_term = jnp.exp(jnp.where(low_r, sink_e - m_e, sink_o - m_o))
        out = acc[:, :LANES] / (acc[:, LANES:] + sink_term)
        for p in range(pairs):
            c0 = (h * pairs + p) * LANES
            attn_ref[:, c0:c0 + LANES] = _bf16(out[p * w:(p + 1) * w])

    ti = lax.broadcasted_iota(jnp.int32, (w, w), 0)
    si = lax.broadcasted_iota(jnp.int32, (w, w), 1)
    causal = si <= ti
    for g in range(SGU_GROUPS):
        sl = slice(g * SGU_CH, (g + 1) * SGU_CH)
        x = vg_ref[:, sl].astype(jnp.float32)
        mu = jnp.mean(x, axis=-1, keepdims=True)
        xc = x - mu
        var = jnp.mean(xc * xc, axis=-1, keepdims=True)
        vn = xc * lax.rsqrt(var + LN_EPS) * lng_ref[:, sl] + lnb_ref[:, sl]
        wsg = _bf16(jnp.where(causal, ws_ref[g], 0.0))
        sv = _dot(wsg, _bf16(vn)) + bs_ref[g]
        sgu_ref[:, sl] = _bf16(u_ref[:, sl].astype(jnp.float32) * sv)


def _mixers(h, hkv, sinks, ln_g, ln_b, w_s, b_s):
    w = WINDOW
    nb = N_TOK // w
    grid_spec = pltpu.PrefetchScalarGridSpec(
        num_scalar_prefetch=0,
        grid=(nb,),
        in_specs=[
            pl.BlockSpec(memory_space=pltpu.SMEM),
            pl.BlockSpec((w, ATTN_W), lambda n: (n, H_Q // ATTN_W)),
            pl.BlockSpec((w, 2 * KV_W), lambda n: (n, 0)),
            pl.BlockSpec((w, 2 * KV_W), lambda n: (jnp.maximum(n - 1, 0), 0)),
            pl.BlockSpec((w, SGU_W), lambda n: (n, H_U // SGU_W)),
            pl.BlockSpec((w, SGU_W), lambda n: (n, H_VG // SGU_W)),
            pl.BlockSpec((1, SGU_W), lambda n: (0, 0)),
            pl.BlockSpec((1, SGU_W), lambda n: (0, 0)),
            pl.BlockSpec((SGU_GROUPS, w, w), lambda n: (0, 0, 0)),
            pl.BlockSpec((SGU_GROUPS, w, 1), lambda n: (0, 0, 0)),
        ],
        out_specs=[
            pl.BlockSpec((w, ATTN_W), lambda n: (n, 0)),
            pl.BlockSpec((w, SGU_W), lambda n: (n, 0)),
        ],
    )
    return pl.pallas_call(
        _mixers_kernel,
        grid_spec=grid_spec,
        out_shape=[jax.ShapeDtypeStruct((N_TOK, ATTN_W), jnp.bfloat16),
                   jax.ShapeDtypeStruct((N_TOK, SGU_W), jnp.bfloat16)],
        compiler_params=pltpu.CompilerParams(
            dimension_semantics=("arbitrary",),
            vmem_limit_bytes=_vmem_limit(
                ((w, ATTN_W), jnp.bfloat16, 4), ((w, 2 * KV_W), jnp.bfloat16, 4),
                ((w, SGU_W), jnp.bfloat16, 6), ((SGU_GROUPS, w, w), jnp.float32, 2),
                ((SGU_GROUPS, w, LANES), jnp.float32, 2),
                ((GQA // 2 * w, 2 * w), jnp.float32, 16))),
        name="mixers",
    )(sinks, h, hkv, hkv, h, h, ln_g.reshape(1, SGU_W), ln_b.reshape(1, SGU_W), w_s,
      b_s.reshape(SGU_GROUPS, w, 1))


def _layer_norm(z, g, b):
    mu = jnp.mean(z, axis=-1, keepdims=True)
    zc = z - mu
    var = jnp.mean(zc * zc, axis=-1, keepdims=True)
    return zc * lax.rsqrt(var + LN_EPS) * g + b


def _merge_kernel(attn_ref, sgu_ref, ga_ref, gb_ref, x_ref, wa_ref, wb_ref, wo_ref, g_ref, b_ref,
                  wr_ref, x1_ref, lg_ref):
    mix = (ga_ref[...].astype(jnp.float32) * _dot(attn_ref[...], wa_ref[...])
           + gb_ref[...].astype(jnp.float32) * _dot(sgu_ref[...], wb_ref[...]))
    z = ALPHA * x_ref[...] + _dot(_bf16(mix), wo_ref[...])
    x1 = _layer_norm(z, g_ref[...], b_ref[...])
    x1_ref[...] = x1
    x_hi = _bf16(x1)
    x_lo = _bf16(x1 - x_hi.astype(jnp.float32))
    parts = _dot(x_hi, wr_ref[...]) + _dot(x_lo, wr_ref[...])
    lg_ref[...] = parts + pltpu.roll(parts, N_EXPERTS, axis=1)


def _router_parts(w_router):
    hi = _bf16(w_router)
    lo = _bf16(w_router - hi.astype(jnp.float32))
    return jnp.concatenate([hi, lo], axis=1)


def _merge(attn, sgu, h, x2d, wa, wb, wo, g, b, wr):
    tm = MERGE_TM
    resident = pl.Buffered(1)
    grid_spec = pltpu.PrefetchScalarGridSpec(
        num_scalar_prefetch=0,
        grid=(N_TOK // tm,),
        in_specs=[
            pl.BlockSpec((tm, ATTN_W), lambda i: (i, 0)),
            pl.BlockSpec((tm, SGU_W), lambda i: (i, 0)),
            pl.BlockSpec((tm, D_MODEL), lambda i: (i, H_GA // D_MODEL)),
            pl.BlockSpec((tm, D_MODEL), lambda i: (i, H_GB // D_MODEL)),
            pl.BlockSpec((tm, D_MODEL), lambda i: (i, 0)),
            pl.BlockSpec((ATTN_W, D_MODEL), lambda i: (0, 0), pipeline_mode=resident),
            pl.BlockSpec((SGU_W, D_MODEL), lambda i: (0, 0), pipeline_mode=resident),
            pl.BlockSpec((D_MODEL, D_MODEL), lambda i: (0, 0), pipeline_mode=resident),
            pl.BlockSpec((1, D_MODEL), lambda i: (0, 0)),
            pl.BlockSpec((1, D_MODEL), lambda i: (0, 0)),
            pl.BlockSpec((D_MODEL, 2 * N_EXPERTS), lambda i: (0, 0)),
        ],
        out_specs=[
            pl.BlockSpec((tm, D_MODEL), lambda i: (i, 0)),
            pl.BlockSpec((tm, 2 * N_EXPERTS), lambda i: (i, 0)),
        ],
    )
    x1, lg = pl.pallas_call(
        _merge_kernel,
        grid_spec=grid_spec,
        out_shape=[jax.ShapeDtypeStruct((N_TOK, D_MODEL), jnp.float32),
                   jax.ShapeDtypeStruct((N_TOK, 2 * N_EXPERTS), jnp.float32)],
        compiler_params=pltpu.CompilerParams(
            dimension_semantics=("arbitrary",),
            vmem_limit_bytes=_vmem_limit(
                ((tm, ATTN_W), jnp.bfloat16, 2), ((tm, SGU_W), jnp.bfloat16, 2),
                ((tm, D_MODEL), jnp.bfloat16, 4), ((tm, D_MODEL), jnp.float32, 4),
                ((ATTN_W + SGU_W + D_MODEL, D_MODEL), jnp.bfloat16, 1),
                ((D_MODEL, 2 * N_EXPERTS), jnp.bfloat16, 2),
                ((tm, D_MODEL), jnp.float32, 8))),
        name="merge",
    )(attn, sgu, h, h, x2d, wa, wb, wo, g, b, wr)
    return x1, lg[:, :N_EXPERTS].T


def _first_argmax(v, rows):
    m = jnp.max(v, axis=0, keepdims=True)
    i = jnp.min(jnp.where(v == m, rows, float(v.shape[0])), axis=0, keepdims=True)
    return m, i


def _row_index(shape):
    return lax.broadcasted_iota(jnp.int32, shape, 0).astype(jnp.float32)


def _route_kernel(lg_ref, bias_ref, idx_ref, w_ref):
    tt = lg_ref.shape[1]
    neg_inf = jnp.float32(-jnp.inf)
    scores = jax.nn.sigmoid(lg_ref[...])
    biased = scores + bias_ref[...]
    row_g = _row_index((GROUP_SIZE, tt))
    gs = []
    for g in range(N_EXPERT_GROUPS):
        blk = biased[g * GROUP_SIZE:(g + 1) * GROUP_SIZE]
        m1, i1 = _first_argmax(blk, row_g)
        m2 = jnp.max(jnp.where(row_g == i1, neg_inf, blk), axis=0, keepdims=True)
        gs.append(m1 + m2)
    cur = jnp.concatenate(gs, axis=0)
    row_n = _row_index((N_EXPERT_GROUPS, tt))
    sel = jnp.zeros((N_EXPERT_GROUPS, tt), jnp.float32)
    for _ in range(TOPK_GROUPS):
        _, i = _first_argmax(cur, row_n)
        hit = row_n == i
        sel = jnp.where(hit, 1.0, sel)
        cur = jnp.where(hit, neg_inf, cur)
    emask = jnp.concatenate(
        [jnp.broadcast_to(sel[g:g + 1], (GROUP_SIZE, tt)) for g in range(N_EXPERT_GROUPS)], axis=0)
    masked = jnp.where(emask > 0.5, biased, neg_inf)
    row_e = _row_index((N_EXPERTS, tt))
    idx_rows, w_rows = [], []
    for _ in range(TOP_K):
        _, i = _first_argmax(masked, row_e)
        hit = row_e == i
        w_rows.append(jnp.sum(jnp.where(hit, scores, 0.0), axis=0, keepdims=True))
        idx_rows.append(i)
        masked = jnp.where(hit, neg_inf, masked)
    wsel = jnp.concatenate(w_rows, axis=0)
    idx_ref[...] = jnp.concatenate(idx_rows, axis=0).astype(jnp.int32)
    w_ref[...] = wsel / (jnp.sum(wsel, axis=0, keepdims=True) + 1e-20) * ROUTED_SCALE


def _route(logits_t, bias):
    tt = ROUTE_TT
    return pl.pallas_call(
        _route_kernel,
        grid=(N_TOK // tt,),
        in_specs=[pl.BlockSpec((N_EXPERTS, tt), lambda i: (0, i)),
                  pl.BlockSpec((N_EXPERTS, 1), lambda i: (0, 0))],
        out_specs=[pl.BlockSpec((TOP_K, tt), lambda i: (0, i)),
                   pl.BlockSpec((TOP_K, tt), lambda i: (0, i))],
        out_shape=[jax.ShapeDtypeStruct((TOP_K, N_TOK), jnp.int32),
                   jax.ShapeDtypeStruct((TOP_K, N_TOK), jnp.float32)],
        compiler_params=pltpu.CompilerParams(dimension_semantics=("arbitrary",)),
        name="route",
    )(logits_t, bias.reshape(N_EXPERTS, 1))


def _dispatch_plan(idx_t):
    bm = EXPERT_BM
    nb = N_EXPERT_BLOCKS
    flat_e = idx_t.reshape(-1)
    counts = jnp.sum((flat_e[:, None] == jnp.arange(N_EXPERTS, dtype=jnp.int32)[None, :])
                     .astype(jnp.int32), axis=0)
    e_ids = jnp.arange(N_EXPERTS, dtype=jnp.int32)
    upto = (e_ids[:, None] <= e_ids[None, :]).astype(jnp.int32)
    padded = (counts + bm - 1) // bm * bm
    pad_end = jnp.sum(padded[:, None] * upto, axis=0)
    fill_end = jnp.sum((padded - counts)[:, None] * upto, axis=0)
    n_fill = N_SLOTS - N_ASSIGN
    fill_key = jnp.sum((jnp.arange(n_fill, dtype=jnp.int32)[:, None] >= fill_end[None, :])
                       .astype(jnp.int32), axis=1)
    keys = jnp.concatenate([flat_e, fill_key])
    pos_bits = (N_SLOTS - 1).bit_length()
    slot = jnp.arange(N_SLOTS, dtype=jnp.int32)
    src = jnp.sort((keys << pos_bits) | slot) & ((1 << pos_bits) - 1)
    real = src < N_ASSIGN
    dst = jnp.where(real, src, N_ASSIGN + (slot & (2 * bm - 1)))
    dump_block = N_ASSIGN + jnp.arange(bm, dtype=jnp.int32)
    dst = jnp.concatenate([dump_block, dst, dst[-bm:]]).reshape(nb + 2, 1, bm)
    n_used = jnp.sum(jnp.any(real.reshape(nb, bm), axis=1).astype(jnp.int32))
    block_start = jnp.arange(nb + 1, dtype=jnp.int32) * bm
    block_e = jnp.sum((block_start[:, None] >= pad_end[None, :]).astype(jnp.int32), axis=1)
    block_e = jnp.minimum(block_e, N_EXPERTS - 1).astype(jnp.int32)
    prev_e = jnp.concatenate([jnp.full((1,), -1, jnp.int32), block_e[:-1]])
    fresh = (block_e != prev_e).astype(jnp.int32)
    later = (counts > 0)[None, :] & (e_ids[None, :] > block_e[:, None])
    next_e = jnp.min(jnp.where(later, e_ids[None, :], N_EXPERTS), axis=1).astype(jnp.int32)
    return dst, n_used.reshape(1), block_e, fresh, next_e


def _experts_kernel(nused_ref, be_ref, fresh_ref, nxt_ref, dstp_ref, dstc_ref, dstn_ref, dstnn_ref,
                    x_hbm, w1_hbm, w3_hbm, w2_hbm, y_hbm, xbuf, obuf, w1s, w3s, w2s, w1b, w3b, w2b,
                    gsem, ssem, wsem):
    b = pl.program_id(0)
    bm = EXPERT_BM
    slot = b % 2
    gslot = lax.rem(b, N_GATHER_BUFS)
    n_used = nused_ref[0]

    def start_gather(idx_ref, s):
        for r in range(bm):
            tok = idx_ref[0, 0, r] & (N_TOK - 1)
            pltpu.make_async_copy(x_hbm.at[pl.ds(tok, 1)], xbuf.at[s, pl.ds(r, 1)],
                                  gsem.at[s]).start()

    def wait_gather(s):
        pltpu.make_async_copy(x_hbm.at[pl.ds(0, bm)], xbuf.at[s], gsem.at[s]).wait()

    def start_scatter_prev(s):
        for r in range(bm):
            row = dstp_ref[0, 0, r]
            pltpu.make_async_copy(obuf.at[s, pl.ds(r, 1)], y_hbm.at[pl.ds(row, 1)],
                                  ssem.at[s]).start(priority=1)

    def wait_scatter(s):
        pltpu.make_async_copy(obuf.at[s], y_hbm.at[pl.ds(0, bm)], ssem.at[s]).wait()

    def weight_copies(e):
        return (pltpu.make_async_copy(w1_hbm.at[e], w1s, wsem.at[0]),
                pltpu.make_async_copy(w3_hbm.at[e], w3s, wsem.at[1]),
                pltpu.make_async_copy(w2_hbm.at[e], w2s, wsem.at[2]))

    def block_step(w1v, w3v, w2v):
        start_scatter_prev(1 - slot)
        xb = _bf16(xbuf[gslot])
        a = _bf16(jax.nn.silu(_dot(xb, w1v)) * _dot(xb, w3v))
        start_gather(dstnn_ref, lax.rem(b + 2, N_GATHER_BUFS))
        obuf[slot] = _dot(a, w2v)

    @pl.when(b == 0)
    def _():
        for c in weight_copies(be_ref[0]):
            c.start(priority=1)
        obuf[1] = jnp.zeros(obuf.shape[1:], obuf.dtype)
        fill = pltpu.make_async_copy(obuf.at[1], y_hbm.at[pl.ds(N_ASSIGN + bm, bm)], ssem.at[0])
        fill.start()
        fill.wait()
        start_gather(dstc_ref, 0)
        start_gather(dstn_ref, 1)

    @pl.when((b >= 1) & (b <= n_used))
    def _():
        wait_scatter(slot)

    @pl.when(b < n_used)
    def _():
        wait_gather(gslot)

        @pl.when(fresh_ref[b] == 1)
        def _():
            for c in weight_copies(be_ref[b]):
                c.wait()
            w1v, w3v, w2v = _bf16(w1s[...]), _bf16(w3s[...]), _bf16(w2s[...])
            w1b[...] = w1v
            w3b[...] = w3v
            w2b[...] = w2v
            block_step(w1v, w3v, w2v)

            @pl.when(nxt_ref[b] < N_EXPERTS)
            def _():
                for c in weight_copies(nxt_ref[b]):
                    c.start(priority=1)

        @pl.when(fresh_ref[b] == 0)
        def _():
            block_step(w1b[...], w3b[...], w2b[...])

    @pl.when(b == n_used)
    def _():
        wait_gather(gslot)
        wait_gather(lax.rem(b + 1, N_GATHER_BUFS))
        start_scatter_prev(1 - slot)
        wait_scatter(1 - slot)


def _experts(x1, w1, w3, w2, dst, n_used, block_e, fresh, next_e):
    bm = EXPERT_BM
    nb = N_EXPERT_BLOCKS
    smem_blk = lambda f: pl.BlockSpec((1, 1, bm), f, memory_space=pltpu.SMEM)
    hbm = pl.BlockSpec(memory_space=pl.ANY)
    grid_spec = pltpu.PrefetchScalarGridSpec(
        num_scalar_prefetch=4,
        grid=(nb + 1,),
        in_specs=[
            smem_blk(lambda b, *_: (b, 0, 0)),
            smem_blk(lambda b, *_: (b + 1, 0, 0)),
            smem_blk(lambda b, *_: (jnp.minimum(b + 2, nb + 1), 0, 0)),
            smem_blk(lambda b, *_: (jnp.minimum(b + 3, nb + 1), 0, 0)),
            hbm, hbm, hbm, hbm,
        ],
        out_specs=hbm,
        scratch_shapes=[
            pltpu.VMEM((N_GATHER_BUFS, bm, D_MODEL), jnp.float32),
            pltpu.VMEM((2, bm, D_MODEL), jnp.float32),
            pltpu.VMEM((D_MODEL, D_EXPERT), jnp.float32),
            pltpu.VMEM((D_MODEL, D_EXPERT), jnp.float32),
            pltpu.VMEM((D_EXPERT, D_MODEL), jnp.float32),
            pltpu.VMEM((D_MODEL, D_EXPERT), jnp.bfloat16),
            pltpu.VMEM((D_MODEL, D_EXPERT), jnp.bfloat16),
            pltpu.VMEM((D_EXPERT, D_MODEL), jnp.bfloat16),
            pltpu.SemaphoreType.DMA((N_GATHER_BUFS,)),
            pltpu.SemaphoreType.DMA((2,)),
            pltpu.SemaphoreType.DMA((3,)),
        ],
    )
    return pl.pallas_call(
        _experts_kernel,
        grid_spec=grid_spec,
        out_shape=jax.ShapeDtypeStruct((N_ASSIGN + 2 * bm, D_MODEL), jnp.float32),
        compiler_params=pltpu.CompilerParams(
            dimension_semantics=("arbitrary",),
            vmem_limit_bytes=_vmem_limit(
                ((N_GATHER_BUFS + 2, bm, D_MODEL), jnp.float32, 1),
                ((3, D_MODEL, D_EXPERT), jnp.float32, 1),
                ((3, D_MODEL, D_EXPERT), jnp.bfloat16, 2),
                ((bm, D_MODEL), jnp.float32, 2), ((bm, D_EXPERT), jnp.float32, 4))),
        name="experts",
    )(n_used, block_e, fresh, next_e, dst, dst, dst, dst, x1, w1, w3, w2)


def _combine_kernel(*refs):
    y_refs = refs[:TOP_K]
    w_ref, x_ref, s1_ref, s3_ref, s2_ref, g_ref, b_ref, o_ref = refs[TOP_K:]
    x1 = x_ref[...]
    xb = _bf16(x1)
    a = _bf16(jax.nn.silu(_dot(xb, s1_ref[...])) * _dot(xb, s3_ref[...]))
    ffn = _dot(a, s2_ref[...])
    wts = w_ref[...]
    routed = wts[:, 0:1] * y_refs[0][...]
    for k in range(1, TOP_K):
        routed = routed + wts[:, k:k + 1] * y_refs[k][...]
    o_ref[...] = _layer_norm(ALPHA * x1 + (routed + ffn), g_ref[...], b_ref[...])


def _combine(y, wts, x1, s1, s3, s2, g, b):
    tm = COMBINE_TM
    nt = N_TOK // tm
    y_specs = [pl.BlockSpec((tm, D_MODEL), functools.partial(lambda i, k: (k * nt + i, 0), k=k))
               for k in range(TOP_K)]
    return pl.pallas_call(
        _combine_kernel,
        grid=(nt,),
        in_specs=y_specs + [
            pl.BlockSpec((tm, TOP_K), lambda i: (i, 0)),
            pl.BlockSpec((tm, D_MODEL), lambda i: (i, 0)),
            pl.BlockSpec((D_MODEL, D_SHARED), lambda i: (0, 0)),
            pl.BlockSpec((D_MODEL, D_SHARED), lambda i: (0, 0)),
            pl.BlockSpec((D_SHARED, D_MODEL), lambda i: (0, 0)),
            pl.BlockSpec((1, D_MODEL), lambda i: (0, 0)),
            pl.BlockSpec((1, D_MODEL), lambda i: (0, 0)),
        ],
        out_specs=pl.BlockSpec((tm, D_MODEL), lambda i: (i, 0)),
        out_shape=jax.ShapeDtypeStruct((N_TOK, D_MODEL), jnp.float32),
        compiler_params=pltpu.CompilerParams(
            dimension_semantics=("arbitrary",),
            vmem_limit_bytes=_vmem_limit(
                ((tm, D_MODEL), jnp.float32, 2 * (TOP_K + 2)), ((tm, LANES), jnp.float32, 2),
                ((3, D_MODEL, D_SHARED), jnp.bfloat16, 2), ((tm, D_MODEL), jnp.float32, 6))),
        name="combine",
    )(*([y] * TOP_K), wts, x1, s1, s3, s2, g, b)


def kernel(x, w_in, b_in, sinks, sgu_ln_g, sgu_ln_b, w_spatial, b_spatial, w_branch_attn,
           w_branch_sgu, w_out, ln1_g, ln1_b, w_router, router_bias, w1, w3, w2, ws1, ws3, ws2,
           ln2_g, ln2_b):
    assert x.shape == (BATCH, SEQ, D_MODEL) and w_in.shape == (1, D_MODEL, IN_W)
    x2d = x.reshape(N_TOK, D_MODEL)
    hkv, x_bf = _kv_proj(x2d, w_in[0], b_in)
    h = _in_proj(x_bf, w_in[0], b_in)
    attn, sgu = _mixers(h, hkv, sinks[0], sgu_ln_g[0], sgu_ln_b[0], w_spatial[0], b_spatial[0])
    x1, logits_t = _merge(attn, sgu, h, x2d, _bf16(w_branch_attn[0]), _bf16(w_branch_sgu[0]),
                          _bf16(w_out[0]), ln1_g, ln1_b, _router_parts(w_router[0]))
    idx_t, w_t = _route(logits_t, router_bias[0])
    y = _experts(x1, w1[0], w3[0], w2[0], *_dispatch_plan(idx_t))
    out = _combine(y, w_t.T, x1, _bf16(ws1[0]), _bf16(ws3[0]), _bf16(ws2[0]), ln2_g, ln2_b)
    return out.reshape(BATCH, SEQ, D_MODEL)
```

```python
import functools
import math

import numpy as np
import jax
import jax.numpy as jnp
from jax import lax
from jax.experimental import pallas as pl
from jax.experimental.pallas import tpu as pltpu

D_MODEL = 2048
BATCH = 2
SEQ = 4096
N_TOK = BATCH * SEQ
N_Q_HEADS = 32
N_KV_HEADS = 4
HEAD_DIM = 64
GQA = N_Q_HEADS // N_KV_HEADS
WINDOW = 128
ROPE_THETA = 500000.0
ROT_DIM = HEAD_DIM // 4
SGU_GROUPS = 8
SGU_CH = 128
N_EXPERTS = 64
N_EXPERT_GROUPS = 8
GROUP_SIZE = N_EXPERTS // N_EXPERT_GROUPS
TOPK_GROUPS = 4
TOP_K = 8
D_EXPERT = 512
D_SHARED = 512
ROUTED_SCALE = 2.5
ATTN_W = N_Q_HEADS * HEAD_DIM
KV_W = N_KV_HEADS * HEAD_DIM
SGU_W = SGU_GROUPS * SGU_CH
IN_W = ATTN_W + 2 * KV_W + 2 * SGU_W + 2 * D_MODEL
ALPHA = 2.0 ** 0.25
LN_EPS = 1e-5
N_ASSIGN = N_TOK * TOP_K

LANES = 128
V7X_VMEM_BYTES = 64 * 1024 * 1024


def _vmem_limit(*buffers):
    need = sum(copies * math.prod(shape) * jnp.dtype(dtype).itemsize
               for shape, dtype, copies in buffers)
    assert need <= V7X_VMEM_BYTES * 7 // 8, need
    return need

PROJ_TM = 1024
PROJ_TN = 1024
MERGE_TM = 256
ROUTE_TT = 512
EXPERT_BM = 256
N_GATHER_BUFS = 3
COMBINE_TM = 128
N_EXPERT_BLOCKS = -(-(N_ASSIGN + N_EXPERTS * (EXPERT_BM - 1)) // EXPERT_BM)
N_SLOTS = N_EXPERT_BLOCKS * EXPERT_BM

H_Q, H_GA, H_GB, H_U, H_VG = 0, 2048, 4096, 6144, 7168
H_W = 8192
PROJ_UNIT = 512
_SRC_UNIT = np.array([0, 1, 2, 3, 9, 10, 11, 12, 13, 14, 15, 16, 5, 6, 7, 8], np.int32)
_SRC_UNIT_KV = ATTN_W // PROJ_UNIT
_N_PROJ_TILES = H_W // PROJ_TN
_Q_TILES = ATTN_W // PROJ_TN
_GATE_END = H_U // PROJ_TN


def _bf16(a):
    return a.astype(jnp.bfloat16)


def _dot(a, b):
    return jnp.dot(a, b, preferred_element_type=jnp.float32)


def _dot_nt(a, b):
    return lax.dot_general(a, b, (((1,), (1,)), ((), ())), preferred_element_type=jnp.float32)


def _rope_slab(x, c, s_next, s_prev):
    return (x * c + pltpu.roll(x, LANES - ROT_DIM // 2, axis=1) * s_next
            + pltpu.roll(x, ROT_DIM // 2, axis=1) * s_prev)


def _in_proj_kernel(src_ref, x_ref, wa_ref, wb_ref, ba_ref, bb_ref, c_ref, sn_ref, sp_ref, o_ref,
                    wbf_ref):
    j = pl.program_id(0)
    i = pl.program_id(1)

    @pl.when(i == 0)
    def _():
        wbf_ref[:, :PROJ_UNIT] = _bf16(wa_ref[...])
        wbf_ref[:, PROJ_UNIT:] = _bf16(wb_ref[...])

    def project():
        bias = jnp.concatenate([ba_ref[...], bb_ref[...]], axis=1)
        return _dot(x_ref[...], wbf_ref[...]) + bias

    @pl.when(j < _Q_TILES)
    def _():
        acc = project()
        c, sn, sp = c_ref[...], sn_ref[...], sp_ref[...]
        scale = HEAD_DIM ** -0.5
        for t in range(PROJ_TN // LANES):
            sl = slice(t * LANES, (t + 1) * LANES)
            o_ref[:, sl] = _bf16(_rope_slab(acc[:, sl], c, sn, sp) * scale)

    @pl.when((j >= _Q_TILES) & (j < _GATE_END))
    def _():
        o_ref[...] = _bf16(jax.nn.sigmoid(project()))

    @pl.when(j >= _GATE_END)
    def _():
        o_ref[...] = _bf16(jax.nn.gelu(project()))


def _kv_proj_kernel(x_ref, w_ref, b_ref, c_ref, sn_ref, sp_ref, o_ref, xb_ref, wbf_ref):
    @pl.when(pl.program_id(0) == 0)
    def _():
        wbf_ref[...] = _bf16(w_ref[...])

    xb = _bf16(x_ref[...])
    xb_ref[...] = xb
    acc = _dot(xb, wbf_ref[...]) + b_ref[...]
    c, sn, sp = c_ref[...], sn_ref[...], sp_ref[...]
    for t in range(2 * KV_W // LANES):
        sl = slice(t * LANES, (t + 1) * LANES)
        if t < KV_W // LANES:
            o_ref[:, sl] = _bf16(_rope_slab(acc[:, sl], c, sn, sp))
        else:
            o_ref[:, sl] = _bf16(acc[:, sl])


def _rope_tables():
    half = ROT_DIM // 2
    inv_freq = ROPE_THETA ** (-np.arange(0, ROT_DIM, 2, dtype=np.float32) / ROT_DIM)
    pos = np.arange(SEQ, dtype=np.float32)
    ang = jnp.asarray(pos[:, None] * inv_freq[None, :].astype(np.float32), jnp.float32)
    cos, sin = jnp.cos(ang), jnp.sin(ang)
    ones = jnp.ones((SEQ, HEAD_DIM - ROT_DIM), jnp.float32)
    zeros = jnp.zeros((SEQ, HEAD_DIM - ROT_DIM), jnp.float32)
    zh = jnp.zeros((SEQ, half), jnp.float32)
    c = jnp.concatenate([cos, cos, ones], axis=1)
    s_next = jnp.concatenate([-sin, zh, zeros], axis=1)
    s_prev = jnp.concatenate([zh, sin, zeros], axis=1)
    rep = LANES // HEAD_DIM
    return tuple(jnp.tile(t, (1, rep)) for t in (c, s_next, s_prev))


def _in_proj(x_bf, w_in, b_in):
    c, sn, sp = _rope_tables()
    n_i = N_TOK // PROJ_TM
    pos_tiles = SEQ // PROJ_TM
    tbl = pl.BlockSpec((PROJ_TM, LANES), lambda j, i, src: (i % pos_tiles, 0))
    grid_spec = pltpu.PrefetchScalarGridSpec(
        num_scalar_prefetch=1,
        grid=(_N_PROJ_TILES, n_i),
        in_specs=[
            pl.BlockSpec((PROJ_TM, D_MODEL), lambda j, i, src: (i, 0)),
            pl.BlockSpec((D_MODEL, PROJ_UNIT), lambda j, i, src: (0, src[2 * j])),
            pl.BlockSpec((D_MODEL, PROJ_UNIT), lambda j, i, src: (0, src[2 * j + 1])),
            pl.BlockSpec((1, PROJ_UNIT), lambda j, i, src: (0, src[2 * j])),
            pl.BlockSpec((1, PROJ_UNIT), lambda j, i, src: (0, src[2 * j + 1])),
            tbl, tbl, tbl,
        ],
        out_specs=pl.BlockSpec((PROJ_TM, PROJ_TN), lambda j, i, src: (i, j)),
        scratch_shapes=[pltpu.VMEM((D_MODEL, PROJ_TN), jnp.bfloat16)],
    )
    return pl.pallas_call(
        _in_proj_kernel,
        grid_spec=grid_spec,
        out_shape=jax.ShapeDtypeStruct((N_TOK, H_W), jnp.bfloat16),
        compiler_params=pltpu.CompilerParams(
            dimension_semantics=("arbitrary", "arbitrary"),
            vmem_limit_bytes=_vmem_limit(
                ((PROJ_TM, D_MODEL), jnp.bfloat16, 2), ((D_MODEL, PROJ_TN), jnp.float32, 2),
                ((PROJ_TM, LANES), jnp.float32, 6), ((PROJ_TM, PROJ_TN), jnp.bfloat16, 2),
                ((D_MODEL, PROJ_TN), jnp.bfloat16, 1), ((PROJ_TM, PROJ_TN), jnp.float32, 3))),
        name="in_proj",
    )(jnp.asarray(_SRC_UNIT), x_bf, w_in, w_in, b_in, b_in, c, sn, sp)


def _kv_proj(x2d, w_in, b_in):
    c, sn, sp = _rope_tables()
    tm = PROJ_TM // 2
    pos_tiles = SEQ // tm
    tbl = pl.BlockSpec((tm, LANES), lambda i: (i % pos_tiles, 0))
    return pl.pallas_call(
        _kv_proj_kernel,
        grid=(N_TOK // tm,),
        in_specs=[
            pl.BlockSpec((tm, D_MODEL), lambda i: (i, 0)),
            pl.BlockSpec((D_MODEL, 2 * KV_W), lambda i: (0, _SRC_UNIT_KV)),
            pl.BlockSpec((1, 2 * KV_W), lambda i: (0, _SRC_UNIT_KV)),
            tbl, tbl, tbl,
        ],
        out_specs=[pl.BlockSpec((tm, 2 * KV_W), lambda i: (i, 0)),
                   pl.BlockSpec((tm, D_MODEL), lambda i: (i, 0))],
        out_shape=[jax.ShapeDtypeStruct((N_TOK, 2 * KV_W), jnp.bfloat16),
                   jax.ShapeDtypeStruct((N_TOK, D_MODEL), jnp.bfloat16)],
        scratch_shapes=[pltpu.VMEM((D_MODEL, 2 * KV_W), jnp.bfloat16)],
        compiler_params=pltpu.CompilerParams(
            dimension_semantics=("arbitrary",),
            vmem_limit_bytes=_vmem_limit(
                ((tm, D_MODEL), jnp.float32, 2), ((tm, D_MODEL), jnp.bfloat16, 3),
                ((D_MODEL, 2 * KV_W), jnp.float32, 2), ((D_MODEL, 2 * KV_W), jnp.bfloat16, 1),
                ((tm, LANES), jnp.float32, 6), ((tm, 2 * KV_W), jnp.float32, 3))),
        name="kv_proj",
    )(x2d, w_in, b_in, c, sn, sp)


def _mixers_kernel(sink_ref, q_ref, kvc_ref, kvp_ref, u_ref, vg_ref, lng_ref, lnb_ref,
                   ws_ref, bs_ref, attn_ref, sgu_ref):
    n = pl.program_id(0)
    w = WINDOW
    first_key = jnp.where((n % (SEQ // w)) == 0, w, 0)
    pairs = GQA // 2
    rows = pairs * w

    qi = lax.broadcasted_iota(jnp.int32, (w, 2 * w), 0)
    kj = lax.broadcasted_iota(jnp.int32, (w, 2 * w), 1)
    valid = (kj > qi) & (kj <= qi + w) & (kj >= first_key)
    valid = jnp.concatenate([valid] * pairs, axis=0)
    lane = lax.broadcasted_iota(jnp.int32, (2 * w, LANES), 1)
    low = lane < HEAD_DIM
    lane_r = lax.broadcasted_iota(jnp.int32, (rows, LANES), 1)
    low_r = lane_r < HEAD_DIM
    ones_low = jnp.where(low, 1.0, 0.0).astype(jnp.bfloat16)
    ones_high = jnp.where(low, 0.0, 1.0).astype(jnp.bfloat16)
    neg_inf = jnp.float32(-jnp.inf)

    kv = jnp.concatenate([kvp_ref[...], kvc_ref[...]], axis=0).astype(jnp.float32)

    def padded(group, head_is_high):
        rolled = pltpu.roll(group, HEAD_DIM, axis=1)
        if head_is_high:
            lo_half, hi_half = rolled, group
        else:
            lo_half, hi_half = group, rolled
        return (_bf16(jnp.where(low, lo_half, 0.0)), _bf16(jnp.where(low, 0.0, hi_half)))

    for h in range(N_KV_HEADS):
        g0 = (h // 2) * LANES
        k_lo, k_hi = padded(kv[:, g0:g0 + LANES], h % 2 == 1)
        v_lo, v_hi = padded(kv[:, KV_W + g0:KV_W + g0 + LANES], h % 2 == 1)
        r_even = jnp.concatenate([v_lo, ones_low], axis=1)
        r_odd = jnp.concatenate([v_hi, ones_high], axis=1)
        q4 = jnp.concatenate(
            [q_ref[:, (h * pairs + p) * LANES:(h * pairs + p + 1) * LANES] for p in range(pairs)],
            axis=0)
        sink_e = jnp.concatenate(
            [jnp.full((w, 1), sink_ref[h * GQA + 2 * p], jnp.float32) for p in range(pairs)], axis=0)
        sink_o = jnp.concatenate(
            [jnp.full((w, 1), sink_ref[h * GQA + 2 * p + 1], jnp.float32) for p in range(pairs)], axis=0)

        s_e = jnp.where(valid, _dot_nt(q4, k_lo), neg_inf)
        s_o = jnp.where(valid, _dot_nt(q4, k_hi), neg_inf)
        m_e = jnp.maximum(jnp.max(s_e, axis=1, keepdims=True), sink_e)
        m_o = jnp.maximum(jnp.max(s_o, axis=1, keepdims=True), sink_o)
        p_e = _bf16(jnp.exp(s_e - m_e))
        p_o = _bf16(jnp.exp(s_o - m_o))
        acc = _dot(p_e, r_even) + _dot(p_o, r_odd)
        sink_term = jnp.exp(jnp.where(low_r, sink_e - m_e, sink_o - m_o))
        out = acc[:, :LANES] / (acc[:, LANES:] + sink_term)
        for p in range(pairs):
            c0 = (h * pairs + p) * LANES
            attn_ref[:, c0:c0 + LANES] = _bf16(out[p * w:(p + 1) * w])

    ti = lax.broadcasted_iota(jnp.int32, (w, w), 0)
    si = lax.broadcasted_iota(jnp.int32, (w, w), 1)
    causal = si <= ti
    for g in range(SGU_GROUPS):
        sl = slice(g * SGU_CH, (g + 1) * SGU_CH)
        x = vg_ref[:, sl].astype(jnp.float32)
        mu = jnp.mean(x, axis=-1, keepdims=True)
        xc = x - mu
        var = jnp.mean(xc * xc, axis=-1, keepdims=True)
        vn = xc * lax.rsqrt(var + LN_EPS) * lng_ref[:, sl] + lnb_ref[:, sl]
        wsg = _bf16(jnp.where(causal, ws_ref[g], 0.0))
        sv = _dot(wsg, _bf16(vn)) + bs_ref[g]
        sgu_ref[:, sl] = _bf16(u_ref[:, sl].astype(jnp.float32) * sv)


def _mixers(h, hkv, sinks, ln_g, ln_b, w_s, b_s):
    w = WINDOW
    nb = N_TOK // w
    grid_spec = pltpu.PrefetchScalarGridSpec(
        num_scalar_prefetch=0,
        grid=(nb,),
        in_specs=[
            pl.BlockSpec(memory_space=pltpu.SMEM),
            pl.BlockSpec((w, ATTN_W), lambda n: (n, H_Q // ATTN_W)),
            pl.BlockSpec((w, 2 * KV_W), lambda n: (n, 0)),
            pl.BlockSpec((w, 2 * KV_W), lambda n: (jnp.maximum(n - 1, 0), 0)),
            pl.BlockSpec((w, SGU_W), lambda n: (n, H_U // SGU_W)),
            pl.BlockSpec((w, SGU_W), lambda n: (n, H_VG // SGU_W)),
            pl.BlockSpec((1, SGU_W), lambda n: (0, 0)),
            pl.BlockSpec((1, SGU_W), lambda n: (0, 0)),
            pl.BlockSpec((SGU_GROUPS, w, w), lambda n: (0, 0, 0)),
            pl.BlockSpec((SGU_GROUPS, w, 1), lambda n: (0, 0, 0)),
        ],
        out_specs=[
            pl.BlockSpec((w, ATTN_W), lambda n: (n, 0)),
            pl.BlockSpec((w, SGU_W), lambda n: (n, 0)),
        ],
    )
    return pl.pallas_call(
        _mixers_kernel,
        grid_spec=grid_spec,
        out_shape=[jax.ShapeDtypeStruct((N_TOK, ATTN_W), jnp.bfloat16),
                   jax.ShapeDtypeStruct((N_TOK, SGU_W), jnp.bfloat16)],
        compiler_params=pltpu.CompilerParams(
            dimension_semantics=("arbitrary",),
            vmem_limit_bytes=_vmem_limit(
                ((w, ATTN_W), jnp.bfloat16, 4), ((w, 2 * KV_W), jnp.bfloat16, 4),
                ((w, SGU_W), jnp.bfloat16, 6), ((SGU_GROUPS, w, w), jnp.float32, 2),
                ((SGU_GROUPS, w, LANES), jnp.float32, 2),
                ((GQA // 2 * w, 2 * w), jnp.float32, 16))),
        name="mixers",
    )(sinks, h, hkv, hkv, h, h, ln_g.reshape(1, SGU_W), ln_b.reshape(1, SGU_W), w_s,
      b_s.reshape(SGU_GROUPS, w, 1))


def _layer_norm(z, g, b):
    mu = jnp.mean(z, axis=-1, keepdims=True)
    zc = z - mu
    var = jnp.mean(zc * zc, axis=-1, keepdims=True)
    return zc * lax.rsqrt(var + LN_EPS) * g + b


def _merge_kernel(attn_ref, sgu_ref, ga_ref, gb_ref, x_ref, wa_ref, wb_ref, wo_ref, g_ref, b_ref,
                  wr_ref, x1_ref, lg_ref, z_ref):
    i = pl.program_id(0)
    nt = pl.num_programs(0) - 1

    def project(slot):
        mix = (ga_ref[...].astype(jnp.float32) * _dot(attn_ref[...], wa_ref[...])
               + gb_ref[...].astype(jnp.float32) * _dot(sgu_ref[...], wb_ref[...]))
        z_ref[slot] = ALPHA * x_ref[...] + _dot(_bf16(mix), wo_ref[...])

    def norm_and_route(slot):
        x1 = _layer_norm(z_ref[slot], g_ref[...], b_ref[...])
        x1_ref[...] = x1
        x_hi = _bf16(x1)
        x_lo = _bf16(x1 - x_hi.astype(jnp.float32))
        parts = _dot(x_hi, wr_ref[...]) + _dot(x_lo, wr_ref[...])
        lg_ref[...] = parts + pltpu.roll(parts, N_EXPERTS, axis=1)

    @pl.when(i == 0)
    def _():
        project(0)

    @pl.when((i >= 1) & (i < nt))
    def _():
        norm_and_route((i - 1) % 2)
        project(i % 2)

    @pl.when(i == nt)
    def _():
        norm_and_route((i - 1) % 2)


def _router_parts(w_router):
    hi = _bf16(w_router)
    lo = _bf16(w_router - hi.astype(jnp.float32))
    return jnp.concatenate([hi, lo], axis=1)


def _merge(attn, sgu, h, x2d, wa, wb, wo, g, b, wr):
    tm = MERGE_TM
    nt = N_TOK // tm
    resident = pl.Buffered(1)
    cur = lambda i: jnp.minimum(i, nt - 1)
    prev = lambda i: jnp.maximum(i - 1, 0)
    grid_spec = pltpu.PrefetchScalarGridSpec(
        num_scalar_prefetch=0,
        grid=(nt + 1,),
        in_specs=[
            pl.BlockSpec((tm, ATTN_W), lambda i: (cur(i), 0)),
            pl.BlockSpec((tm, SGU_W), lambda i: (cur(i), 0)),
            pl.BlockSpec((tm, D_MODEL), lambda i: (cur(i), H_GA // D_MODEL)),
            pl.BlockSpec((tm, D_MODEL), lambda i: (cur(i), H_GB // D_MODEL)),
            pl.BlockSpec((tm, D_MODEL), lambda i: (cur(i), 0)),
            pl.BlockSpec((ATTN_W, D_MODEL), lambda i: (0, 0), pipeline_mode=resident),
            pl.BlockSpec((SGU_W, D_MODEL), lambda i: (0, 0), pipeline_mode=resident),
            pl.BlockSpec((D_MODEL, D_MODEL), lambda i: (0, 0), pipeline_mode=resident),
            pl.BlockSpec((1, D_MODEL), lambda i: (0, 0)),
            pl.BlockSpec((1, D_MODEL), lambda i: (0, 0)),
            pl.BlockSpec((D_MODEL, 2 * N_EXPERTS), lambda i: (0, 0)),
        ],
        out_specs=[
            pl.BlockSpec((tm, D_MODEL), lambda i: (prev(i), 0)),
            pl.BlockSpec((tm, 2 * N_EXPERTS), lambda i: (prev(i), 0)),
        ],
        scratch_shapes=[pltpu.VMEM((2, tm, D_MODEL), jnp.float32)],
    )
    x1, lg = pl.pallas_call(
        _merge_kernel,
        grid_spec=grid_spec,
        out_shape=[jax.ShapeDtypeStruct((N_TOK, D_MODEL), jnp.float32),
                   jax.ShapeDtypeStruct((N_TOK, 2 * N_EXPERTS), jnp.float32)],
        compiler_params=pltpu.CompilerParams(
            dimension_semantics=("arbitrary",),
            vmem_limit_bytes=_vmem_limit(
                ((tm, ATTN_W), jnp.bfloat16, 2), ((tm, SGU_W), jnp.bfloat16, 2),
                ((tm, D_MODEL), jnp.bfloat16, 4), ((tm, D_MODEL), jnp.float32, 4),
                ((ATTN_W + SGU_W + D_MODEL, D_MODEL), jnp.bfloat16, 1),
                ((D_MODEL, 2 * N_EXPERTS), jnp.bfloat16, 2), ((2, tm, D_MODEL), jnp.float32, 1),
                ((tm, D_MODEL), jnp.float32, 8))),
        name="merge",
    )(attn, sgu, h, h, x2d, wa, wb, wo, g, b, wr)
    return x1, lg[:, :N_EXPERTS].T


def _first_argmax(v, rows):
    m = jnp.max(v, axis=0, keepdims=True)
    i = jnp.min(jnp.where(v == m, rows, float(v.shape[0])), axis=0, keepdims=True)
    return m, i


def _row_index(shape):
    return lax.broadcasted_iota(jnp.int32, shape, 0).astype(jnp.float32)


def _route_kernel(lg_ref, bias_ref, idx_ref, w_ref):
    tt = lg_ref.shape[1]
    neg_inf = jnp.float32(-jnp.inf)
    scores = jax.nn.sigmoid(lg_ref[...])
    biased = scores + bias_ref[...]
    row_g = _row_index((GROUP_SIZE, tt))
    gs = []
    for g in range(N_EXPERT_GROUPS):
        blk = biased[g * GROUP_SIZE:(g + 1) * GROUP_SIZE]
        m1, i1 = _first_argmax(blk, row_g)
        m2 = jnp.max(jnp.where(row_g == i1, neg_inf, blk), axis=0, keepdims=True)
        gs.append(m1 + m2)
    cur = jnp.concatenate(gs, axis=0)
    row_n = _row_index((N_EXPERT_GROUPS, tt))
    sel = jnp.zeros((N_EXPERT_GROUPS, tt), jnp.float32)
    for _ in range(TOPK_GROUPS):
        _, i = _first_argmax(cur, row_n)
        hit = row_n == i
        sel = jnp.where(hit, 1.0, sel)
        cur = jnp.where(hit, neg_inf, cur)
    emask = jnp.concatenate(
        [jnp.broadcast_to(sel[g:g + 1], (GROUP_SIZE, tt)) for g in range(N_EXPERT_GROUPS)], axis=0)
    masked = jnp.where(emask > 0.5, biased, neg_inf)
    row_e = _row_index((N_EXPERTS, tt))
    idx_rows, w_rows = [], []
    for _ in range(TOP_K):
        _, i = _first_argmax(masked, row_e)
        hit = row_e == i
        w_rows.append(jnp.sum(jnp.where(hit, scores, 0.0), axis=0, keepdims=True))
        idx_rows.append(i)
        masked = jnp.where(hit, neg_inf, masked)
    wsel = jnp.concatenate(w_rows, axis=0)
    idx_ref[...] = jnp.concatenate(idx_rows, axis=0).astype(jnp.int32)
    w_ref[...] = wsel / (jnp.sum(wsel, axis=0, keepdims=True) + 1e-20) * ROUTED_SCALE


def _route(logits_t, bias):
    tt = ROUTE_TT
    return pl.pallas_call(
        _route_kernel,
        grid=(N_TOK // tt,),
        in_specs=[pl.BlockSpec((N_EXPERTS, tt), lambda i: (0, i)),
                  pl.BlockSpec((N_EXPERTS, 1), lambda i: (0, 0))],
        out_specs=[pl.BlockSpec((TOP_K, tt), lambda i: (0, i)),
                   pl.BlockSpec((TOP_K, tt), lambda i: (0, i))],
        out_shape=[jax.ShapeDtypeStruct((TOP_K, N_TOK), jnp.int32),
                   jax.ShapeDtypeStruct((TOP_K, N_TOK), jnp.float32)],
        compiler_params=pltpu.CompilerParams(dimension_semantics=("arbitrary",)),
        name="route",
    )(logits_t, bias.reshape(N_EXPERTS, 1))


def _dispatch_plan(idx_t):
    bm = EXPERT_BM
    nb = N_EXPERT_BLOCKS
    flat_e = idx_t.reshape(-1)
    counts = jnp.sum((flat_e[:, None] == jnp.arange(N_EXPERTS, dtype=jnp.int32)[None, :])
                     .astype(jnp.int32), axis=0)
    e_ids = jnp.arange(N_EXPERTS, dtype=jnp.int32)
    upto = (e_ids[:, None] <= e_ids[None, :]).astype(jnp.int32)
    padded = (counts + bm - 1) // bm * bm
    pad_end = jnp.sum(padded[:, None] * upto, axis=0)
    fill_end = jnp.sum((padded - counts)[:, None] * upto, axis=0)
    n_fill = N_SLOTS - N_ASSIGN
    fill_key = jnp.sum((jnp.arange(n_fill, dtype=jnp.int32)[:, None] >= fill_end[None, :])
                       .astype(jnp.int32), axis=1)
    keys = jnp.concatenate([flat_e, fill_key])
    pos_bits = (N_SLOTS - 1).bit_length()
    slot = jnp.arange(N_SLOTS, dtype=jnp.int32)
    src = jnp.sort((keys << pos_bits) | slot) & ((1 << pos_bits) - 1)
    real = src < N_ASSIGN
    dst = jnp.where(real, src, N_ASSIGN + (slot & (2 * bm - 1)))
    dump_block = N_ASSIGN + jnp.arange(bm, dtype=jnp.int32)
    dst = jnp.concatenate([dump_block, dst, dst[-bm:]]).reshape(nb + 2, 1, bm)
    n_used = jnp.sum(jnp.any(real.reshape(nb, bm), axis=1).astype(jnp.int32))
    block_start = jnp.arange(nb + 1, dtype=jnp.int32) * bm
    block_e = jnp.sum((block_start[:, None] >= pad_end[None, :]).astype(jnp.int32), axis=1)
    block_e = jnp.minimum(block_e, N_EXPERTS - 1).astype(jnp.int32)
    prev_e = jnp.concatenate([jnp.full((1,), -1, jnp.int32), block_e[:-1]])
    fresh = (block_e != prev_e).astype(jnp.int32)
    later = (counts > 0)[None, :] & (e_ids[None, :] > block_e[:, None])
    next_e = jnp.min(jnp.where(later, e_ids[None, :], N_EXPERTS), axis=1).astype(jnp.int32)
    return dst, n_used.reshape(1), block_e, fresh, next_e


def _experts_kernel(nused_ref, be_ref, fresh_ref, nxt_ref, dstp_ref, dstc_ref, dstn_ref, dstnn_ref,
                    x_hbm, w1_hbm, w3_hbm, w2_hbm, y_hbm, xbuf, obuf, w1s, w3s, w2s, w1b, w3b, w2b,
                    gsem, ssem, wsem):
    b = pl.program_id(0)
    bm = EXPERT_BM
    slot = b % 2
    gslot = lax.rem(b, N_GATHER_BUFS)
    n_used = nused_ref[0]

    def start_gather(idx_ref, s):
        for r in range(bm):
            tok = idx_ref[0, 0, r] & (N_TOK - 1)
            pltpu.make_async_copy(x_hbm.at[pl.ds(tok, 1)], xbuf.at[s, pl.ds(r, 1)],
                                  gsem.at[s]).start()

    def wait_gather(s):
        pltpu.make_async_copy(x_hbm.at[pl.ds(0, bm)], xbuf.at[s], gsem.at[s]).wait()

    def start_scatter_prev(s):
        for r in range(bm):
            row = dstp_ref[0, 0, r]
            pltpu.make_async_copy(obuf.at[s, pl.ds(r, 1)], y_hbm.at[pl.ds(row, 1)],
                                  ssem.at[s]).start()

    def wait_scatter(s):
        pltpu.make_async_copy(obuf.at[s], y_hbm.at[pl.ds(0, bm)], ssem.at[s]).wait()

    def weight_copies(e):
        return (pltpu.make_async_copy(w1_hbm.at[e], w1s, wsem.at[0]),
                pltpu.make_async_copy(w3_hbm.at[e], w3s, wsem.at[1]),
                pltpu.make_async_copy(w2_hbm.at[e], w2s, wsem.at[2]))

    def block_step(w1v, w3v, w2v):
        start_scatter_prev(1 - slot)
        xb = _bf16(xbuf[gslot])
        a = _bf16(jax.nn.silu(_dot(xb, w1v)) * _dot(xb, w3v))
        start_gather(dstnn_ref, lax.rem(b + 2, N_GATHER_BUFS))
        obuf[slot] = _dot(a, w2v)

    @pl.when(b == 0)
    def _():
        for c in weight_copies(be_ref[0]):
            c.start(priority=1)
        obuf[1] = jnp.zeros(obuf.shape[1:], obuf.dtype)
        fill = pltpu.make_async_copy(obuf.at[1], y_hbm.at[pl.ds(N_ASSIGN + bm, bm)], ssem.at[0])
        fill.start()
        fill.wait()
        start_gather(dstc_ref, 0)
        start_gather(dstn_ref, 1)

    @pl.when((b >= 1) & (b <= n_used))
    def _():
        wait_scatter(slot)

    @pl.when(b < n_used)
    def _():
        wait_gather(gslot)

        @pl.when(fresh_ref[b] == 1)
        def _():
            for c in weight_copies(be_ref[b]):
                c.wait()
            w1v, w3v, w2v = _bf16(w1s[...]), _bf16(w3s[...]), _bf16(w2s[...])
            w1b[...] = w1v
            w3b[...] = w3v
            w2b[...] = w2v
            block_step(w1v, w3v, w2v)

            @pl.when(nxt_ref[b] < N_EXPERTS)
            def _():
                for c in weight_copies(nxt_ref[b]):
                    c.start(priority=1)

        @pl.when(fresh_ref[b] == 0)
        def _():
            block_step(w1b[...], w3b[...], w2b[...])

    @pl.when(b == n_used)
    def _():
        wait_gather(gslot)
        wait_gather(lax.rem(b + 1, N_GATHER_BUFS))
        start_scatter_prev(1 - slot)
        wait_scatter(1 - slot)


def _experts(x1, w1, w3, w2, dst, n_used, block_e, fresh, next_e):
    bm = EXPERT_BM
    nb = N_EXPERT_BLOCKS
    smem_blk = lambda f: pl.BlockSpec((1, 1, bm), f, memory_space=pltpu.SMEM)
    hbm = pl.BlockSpec(memory_space=pl.ANY)
    grid_spec = pltpu.PrefetchScalarGridSpec(
        num_scalar_prefetch=4,
        grid=(nb + 1,),
        in_specs=[
            smem_blk(lambda b, *_: (b, 0, 0)),
            smem_blk(lambda b, *_: (b + 1, 0, 0)),
            smem_blk(lambda b, *_: (jnp.minimum(b + 2, nb + 1), 0, 0)),
            smem_blk(lambda b, *_: (jnp.minimum(b + 3, nb + 1), 0, 0)),
            hbm, hbm, hbm, hbm,
        ],
        out_specs=hbm,
        scratch_shapes=[
            pltpu.VMEM((N_GATHER_BUFS, bm, D_MODEL), jnp.float32),
            pltpu.VMEM((2, bm, D_MODEL), jnp.float32),
            pltpu.VMEM((D_MODEL, D_EXPERT), jnp.float32),
            pltpu.VMEM((D_MODEL, D_EXPERT), jnp.float32),
            pltpu.VMEM((D_EXPERT, D_MODEL), jnp.float32),
            pltpu.VMEM((D_MODEL, D_EXPERT), jnp.bfloat16),
            pltpu.VMEM((D_MODEL, D_EXPERT), jnp.bfloat16),
            pltpu.VMEM((D_EXPERT, D_MODEL), jnp.bfloat16),
            pltpu.SemaphoreType.DMA((N_GATHER_BUFS,)),
            pltpu.SemaphoreType.DMA((2,)),
            pltpu.SemaphoreType.DMA((3,)),
        ],
    )
    return pl.pallas_call(
        _experts_kernel,
        grid_spec=grid_spec,
        out_shape=jax.ShapeDtypeStruct((N_ASSIGN + 2 * bm, D_MODEL), jnp.float32),
        compiler_params=pltpu.CompilerParams(
            dimension_semantics=("arbitrary",),
            vmem_limit_bytes=_vmem_limit(
                ((N_GATHER_BUFS + 2, bm, D_MODEL), jnp.float32, 1),
                ((3, D_MODEL, D_EXPERT), jnp.float32, 1),
                ((3, D_MODEL, D_EXPERT), jnp.bfloat16, 2),
                ((bm, D_MODEL), jnp.float32, 2), ((bm, D_EXPERT), jnp.float32, 4))),
        name="experts",
    )(n_used, block_e, fresh, next_e, dst, dst, dst, dst, x1, w1, w3, w2)


def _combine_kernel(*refs):
    y_refs = refs[:TOP_K]
    w_ref, x_ref, s1_ref, s3_ref, s2_ref, g_ref, b_ref, o_ref = refs[TOP_K:]
    x1 = x_ref[...]
    xb = _bf16(x1)
    a = _bf16(jax.nn.silu(_dot(xb, s1_ref[...])) * _dot(xb, s3_ref[...]))
    ffn = _dot(a, s2_ref[...])
    wts = w_ref[...]
    routed = wts[:, 0:1] * y_refs[0][...]
    for k in range(1, TOP_K):
        routed = routed + wts[:, k:k + 1] * y_refs[k][...]
    o_ref[...] = _layer_norm(ALPHA * x1 + (routed + ffn), g_ref[...], b_ref[...])


def _combine(y, wts, x1, s1, s3, s2, g, b):
    tm = COMBINE_TM
    nt = N_TOK // tm
    y_specs = [pl.BlockSpec((tm, D_MODEL), functools.partial(lambda i, k: (k * nt + i, 0), k=k))
               for k in range(TOP_K)]
    return pl.pallas_call(
        _combine_kernel,
        grid=(nt,),
        in_specs=y_specs + [
            pl.BlockSpec((tm, TOP_K), lambda i: (i, 0)),
            pl.BlockSpec((tm, D_MODEL), lambda i: (i, 0)),
            pl.BlockSpec((D_MODEL, D_SHARED), lambda i: (0, 0)),
            pl.BlockSpec((D_MODEL, D_SHARED), lambda i: (0, 0)),
            pl.BlockSpec((D_SHARED, D_MODEL), lambda i: (0, 0)),
            pl.BlockSpec((1, D_MODEL), lambda i: (0, 0)),
            pl.BlockSpec((1, D_MODEL), lambda i: (0, 0)),
        ],
        out_specs=pl.BlockSpec((tm, D_MODEL), lambda i: (i, 0)),
        out_shape=jax.ShapeDtypeStruct((N_TOK, D_MODEL), jnp.float32),
        compiler_params=pltpu.CompilerParams(
            dimension_semantics=("arbitrary",),
            vmem_limit_bytes=_vmem_limit(
                ((tm, D_MODEL), jnp.float32, 2 * (TOP_K + 2)), ((tm, LANES), jnp.float32, 2),
                ((3, D_MODEL, D_SHARED), jnp.bfloat16, 2), ((tm, D_MODEL), jnp.float32, 6))),
        name="combine",
    )(*([y] * TOP_K), wts, x1, s1, s3, s2, g, b)


def kernel(x, w_in, b_in, sinks, sgu_ln_g, sgu_ln_b, w_spatial, b_spatial, w_branch_attn,
           w_branch_sgu, w_out, ln1_g, ln1_b, w_router, router_bias, w1, w3, w2, ws1, ws3, ws2,
           ln2_g, ln2_b):
    assert x.shape == (BATCH, SEQ, D_MODEL) and w_in.shape == (1, D_MODEL, IN_W)
    x2d = x.reshape(N_TOK, D_MODEL)
    hkv, x_bf = _kv_proj(x2d, w_in[0], b_in)
    h = _in_proj(x_bf, w_in[0], b_in)
    attn, sgu = _mixers(h, hkv, sinks[0], sgu_ln_g[0], sgu_ln_b[0], w_spatial[0], b_spatial[0])
    x1, logits_t = _merge(attn, sgu, h, x2d, _bf16(w_branch_attn[0]), _bf16(w_branch_sgu[0]),
                          _bf16(w_out[0]), ln1_g, ln1_b, _router_parts(w_router[0]))
    idx_t, w_t = _route(logits_t, router_bias[0])
    y = _experts(x1, w1[0], w3[0], w2[0], *_dispatch_plan(idx_t))
    out = _combine(y, w_t.T, x1, _bf16(ws1[0]), _bf16(ws3[0]), _bf16(ws2[0]), ln2_g, ln2_b)
    return out.reshape(BATCH, SEQ, D_MODEL)
```

```python
import functools
import math

import numpy as np
import jax
import jax.numpy as jnp
from jax import lax
from jax.experimental import pallas as pl
from jax.experimental.pallas import tpu as pltpu

D_MODEL = 2048
BATCH = 2
SEQ = 4096
N_TOK = BATCH * SEQ
N_Q_HEADS = 32
N_KV_HEADS = 4
HEAD_DIM = 64
GQA = N_Q_HEADS // N_KV_HEADS
WINDOW = 128
ROPE_THETA = 500000.0
ROT_DIM = HEAD_DIM // 4
SGU_GROUPS = 8
SGU_CH = 128
N_EXPERTS = 64
N_EXPERT_GROUPS = 8
GROUP_SIZE = N_EXPERTS // N_EXPERT_GROUPS
TOPK_GROUPS = 4
TOP_K = 8
D_EXPERT = 512
D_SHARED = 512
ROUTED_SCALE = 2.5
ATTN_W = N_Q_HEADS * HEAD_DIM
KV_W = N_KV_HEADS * HEAD_DIM
SGU_W = SGU_GROUPS * SGU_CH
IN_W = ATTN_W + 2 * KV_W + 2 * SGU_W + 2 * D_MODEL
ALPHA = 2.0 ** 0.25
LN_EPS = 1e-5
N_ASSIGN = N_TOK * TOP_K

LANES = 128
V7X_VMEM_BYTES = 64 * 1024 * 1024


def _vmem_limit(*buffers):
    need = sum(copies * math.prod(shape) * jnp.dtype(dtype).itemsize
               for shape, dtype, copies in buffers)
    assert need <= V7X_VMEM_BYTES * 7 // 8, need
    return need

PROJ_TM = 1024
PROJ_TN = 1024
MERGE_TM = 256
ROUTE_TT = 512
EXPERT_BM = 256
N_GATHER_BUFS = 3
COMBINE_TM = 128
N_EXPERT_BLOCKS = -(-(N_ASSIGN + N_EXPERTS * (EXPERT_BM - 1)) // EXPERT_BM)
N_SLOTS = N_EXPERT_BLOCKS * EXPERT_BM

H_Q, H_GA, H_GB, H_U, H_VG = 0, 2048, 4096, 6144, 7168
H_W = 8192
PROJ_UNIT = 512
_SRC_UNIT = np.array([0, 1, 2, 3, 9, 10, 11, 12, 13, 14, 15, 16, 5, 6, 7, 8], np.int32)
_SRC_UNIT_KV = ATTN_W // PROJ_UNIT
_N_PROJ_TILES = H_W // PROJ_TN
_Q_TILES = ATTN_W // PROJ_TN
_GATE_END = H_U // PROJ_TN


def _bf16(a):
    return a.astype(jnp.bfloat16)


def _dot(a, b):
    return jnp.dot(a, b, preferred_element_type=jnp.float32)


def _dot_nt(a, b):
    return lax.dot_general(a, b, (((1,), (1,)), ((), ())), preferred_element_type=jnp.float32)


def _rope_slab(x, c, s_next, s_prev):
    return (x * c + pltpu.roll(x, LANES - ROT_DIM // 2, axis=1) * s_next
            + pltpu.roll(x, ROT_DIM // 2, axis=1) * s_prev)


def _in_proj_kernel(src_ref, x_ref, wa_ref, wb_ref, ba_ref, bb_ref, c_ref, sn_ref, sp_ref, o_ref,
                    wbf_ref):
    j = pl.program_id(0)
    i = pl.program_id(1)

    @pl.when(i == 0)
    def _():
        wbf_ref[:, :PROJ_UNIT] = _bf16(wa_ref[...])
        wbf_ref[:, PROJ_UNIT:] = _bf16(wb_ref[...])

    def project():
        bias = jnp.concatenate([ba_ref[...], bb_ref[...]], axis=1)
        return _dot(x_ref[...], wbf_ref[...]) + bias

    @pl.when(j < _Q_TILES)
    def _():
        acc = project()
        c, sn, sp = c_ref[...], sn_ref[...], sp_ref[...]
        scale = HEAD_DIM ** -0.5
        for t in range(PROJ_TN // LANES):
            sl = slice(t * LANES, (t + 1) * LANES)
            o_ref[:, sl] = _bf16(_rope_slab(acc[:, sl], c, sn, sp) * scale)

    @pl.when((j >= _Q_TILES) & (j < _GATE_END))
    def _():
        o_ref[...] = _bf16(jax.nn.sigmoid(project()))

    @pl.when(j >= _GATE_END)
    def _():
        o_ref[...] = _bf16(jax.nn.gelu(project()))


def _kv_proj_kernel(x_ref, w_ref, b_ref, c_ref, sn_ref, sp_ref, o_ref, xb_ref, wbf_ref):
    @pl.when(pl.program_id(0) == 0)
    def _():
        wbf_ref[...] = _bf16(w_ref[...])

    xb = _bf16(x_ref[...])
    xb_ref[...] = xb
    acc = _dot(xb, wbf_ref[...]) + b_ref[...]
    c, sn, sp = c_ref[...], sn_ref[...], sp_ref[...]
    for t in range(2 * KV_W // LANES):
        sl = slice(t * LANES, (t + 1) * LANES)
        if t < KV_W // LANES:
            o_ref[:, sl] = _bf16(_rope_slab(acc[:, sl], c, sn, sp))
        else:
            o_ref[:, sl] = _bf16(acc[:, sl])


def _rope_tables():
    half = ROT_DIM // 2
    inv_freq = ROPE_THETA ** (-np.arange(0, ROT_DIM, 2, dtype=np.float32) / ROT_DIM)
    pos = np.arange(SEQ, dtype=np.float32)
    ang = jnp.asarray(pos[:, None] * inv_freq[None, :].astype(np.float32), jnp.float32)
    cos, sin = jnp.cos(ang), jnp.sin(ang)
    ones = jnp.ones((SEQ, HEAD_DIM - ROT_DIM), jnp.float32)
    zeros = jnp.zeros((SEQ, HEAD_DIM - ROT_DIM), jnp.float32)
    zh = jnp.zeros((SEQ, half), jnp.float32)
    c = jnp.concatenate([cos, cos, ones], axis=1)
    s_next = jnp.concatenate([-sin, zh, zeros], axis=1)
    s_prev = jnp.concatenate([zh, sin, zeros], axis=1)
    rep = LANES // HEAD_DIM
    return tuple(jnp.tile(t, (1, rep)) for t in (c, s_next, s_prev))


def _in_proj(x_bf, w_in, b_in):
    c, sn, sp = _rope_tables()
    n_i = N_TOK // PROJ_TM
    pos_tiles = SEQ // PROJ_TM
    tbl = pl.BlockSpec((PROJ_TM, LANES), lambda j, i, src: (i % pos_tiles, 0))
    grid_spec = pltpu.PrefetchScalarGridSpec(
        num_scalar_prefetch=1,
        grid=(_N_PROJ_TILES, n_i),
        in_specs=[
            pl.BlockSpec((PROJ_TM, D_MODEL), lambda j, i, src: (i, 0)),
            pl.BlockSpec((D_MODEL, PROJ_UNIT), lambda j, i, src: (0, src[2 * j])),
            pl.BlockSpec((D_MODEL, PROJ_UNIT), lambda j, i, src: (0, src[2 * j + 1])),
            pl.BlockSpec((1, PROJ_UNIT), lambda j, i, src: (0, src[2 * j])),
            pl.BlockSpec((1, PROJ_UNIT), lambda j, i, src: (0, src[2 * j + 1])),
            tbl, tbl, tbl,
        ],
        out_specs=pl.BlockSpec((PROJ_TM, PROJ_TN), lambda j, i, src: (i, j)),
        scratch_shapes=[pltpu.VMEM((D_MODEL, PROJ_TN), jnp.bfloat16)],
    )
    return pl.pallas_call(
        _in_proj_kernel,
        grid_spec=grid_spec,
        out_shape=jax.ShapeDtypeStruct((N_TOK, H_W), jnp.bfloat16),
        compiler_params=pltpu.CompilerParams(
            dimension_semantics=("arbitrary", "arbitrary"),
            vmem_limit_bytes=_vmem_limit(
                ((PROJ_TM, D_MODEL), jnp.bfloat16, 2), ((D_MODEL, PROJ_TN), jnp.float32, 2),
                ((PROJ_TM, LANES), jnp.float32, 6), ((PROJ_TM, PROJ_TN), jnp.bfloat16, 2),
                ((D_MODEL, PROJ_TN), jnp.bfloat16, 1), ((PROJ_TM, PROJ_TN), jnp.float32, 3))),
        name="in_proj",
    )(jnp.asarray(_SRC_UNIT), x_bf, w_in, w_in, b_in, b_in, c, sn, sp)


def _kv_proj(x2d, w_in, b_in):
    c, sn, sp = _rope_tables()
    tm = PROJ_TM // 2
    pos_tiles = SEQ // tm
    tbl = pl.BlockSpec((tm, LANES), lambda i: (i % pos_tiles, 0))
    return pl.pallas_call(
        _kv_proj_kernel,
        grid=(N_TOK // tm,),
        in_specs=[
            pl.BlockSpec((tm, D_MODEL), lambda i: (i, 0)),
            pl.BlockSpec((D_MODEL, 2 * KV_W), lambda i: (0, _SRC_UNIT_KV)),
            pl.BlockSpec((1, 2 * KV_W), lambda i: (0, _SRC_UNIT_KV)),
            tbl, tbl, tbl,
        ],
        out_specs=[pl.BlockSpec((tm, 2 * KV_W), lambda i: (i, 0)),
                   pl.BlockSpec((tm, D_MODEL), lambda i: (i, 0))],
        out_shape=[jax.ShapeDtypeStruct((N_TOK, 2 * KV_W), jnp.bfloat16),
                   jax.ShapeDtypeStruct((N_TOK, D_MODEL), jnp.bfloat16)],
        scratch_shapes=[pltpu.VMEM((D_MODEL, 2 * KV_W), jnp.bfloat16)],
        compiler_params=pltpu.CompilerParams(
            dimension_semantics=("arbitrary",),
            vmem_limit_bytes=_vmem_limit(
                ((tm, D_MODEL), jnp.float32, 2), ((tm, D_MODEL), jnp.bfloat16, 3),
                ((D_MODEL, 2 * KV_W), jnp.float32, 2), ((D_MODEL, 2 * KV_W), jnp.bfloat16, 1),
                ((tm, LANES), jnp.float32, 6), ((tm, 2 * KV_W), jnp.float32, 3))),
        name="kv_proj",
    )(x2d, w_in, b_in, c, sn, sp)


def _mixers_kernel(sink_ref, q_ref, kvc_ref, kvp_ref, u_ref, vg_ref, lng_ref, lnb_ref,
                   ws_ref, bs_ref, attn_ref, sgu_ref):
    n = pl.program_id(0)
    w = WINDOW
    first_key = jnp.where((n % (SEQ // w)) == 0, w, 0)
    pairs = GQA // 2

    qi = lax.broadcasted_iota(jnp.int32, (w, 2 * w), 0)
    kj = lax.broadcasted_iota(jnp.int32, (w, 2 * w), 1)
    valid = (kj > qi) & (kj <= qi + w) & (kj >= first_key)
    lane = lax.broadcasted_iota(jnp.int32, (2 * w, LANES), 1)
    low = lane < HEAD_DIM
    low_r = lax.broadcasted_iota(jnp.int32, (w, LANES), 1) < HEAD_DIM
    ones_low = jnp.where(low, 1.0, 0.0).astype(jnp.bfloat16)
    ones_high = jnp.where(low, 0.0, 1.0).astype(jnp.bfloat16)
    neg_inf = jnp.float32(-jnp.inf)

    kv = jnp.concatenate([kvp_ref[...], kvc_ref[...]], axis=0).astype(jnp.float32)

    def padded(group, head_is_high):
        rolled = pltpu.roll(group, HEAD_DIM, axis=1)
        if head_is_high:
            lo_half, hi_half = rolled, group
        else:
            lo_half, hi_half = group, rolled
        return (_bf16(jnp.where(low, lo_half, 0.0)), _bf16(jnp.where(low, 0.0, hi_half)))

    for h in range(N_KV_HEADS):
        g0 = (h // 2) * LANES
        k_lo, k_hi = padded(kv[:, g0:g0 + LANES], h % 2 == 1)
        v_lo, v_hi = padded(kv[:, KV_W + g0:KV_W + g0 + LANES], h % 2 == 1)
        r_even = jnp.concatenate([v_lo, ones_low], axis=1)
        r_odd = jnp.concatenate([v_hi, ones_high], axis=1)
        q4 = jnp.concatenate(
            [q_ref[:, (h * pairs + p) * LANES:(h * pairs + p + 1) * LANES] for p in range(pairs)],
            axis=0)
        s_e = _dot_nt(q4, k_lo)
        s_o = _dot_nt(q4, k_hi)
        p_e, p_o, sink_terms = [], [], []
        for p in range(pairs):
            rs = slice(p * w, (p + 1) * w)
            sink_e, sink_o = sink_ref[h * GQA + 2 * p], sink_ref[h * GQA + 2 * p + 1]
            se = jnp.where(valid, s_e[rs], neg_inf)
            so = jnp.where(valid, s_o[rs], neg_inf)
            m_e = jnp.maximum(jnp.max(se, axis=1, keepdims=True), sink_e)
            m_o = jnp.maximum(jnp.max(so, axis=1, keepdims=True), sink_o)
            p_e.append(_bf16(jnp.exp(se - m_e)))
            p_o.append(_bf16(jnp.exp(so - m_o)))
            sink_terms.append(jnp.exp(jnp.where(low_r, sink_e - m_e, sink_o - m_o)))
        acc = (_dot(jnp.concatenate(p_e, axis=0), r_even)
               + _dot(jnp.concatenate(p_o, axis=0), r_odd))
        out = acc[:, :LANES] / (acc[:, LANES:] + jnp.concatenate(sink_terms, axis=0))
        for p in range(pairs):
            c0 = (h * pairs + p) * LANES
            attn_ref[:, c0:c0 + LANES] = _bf16(out[p * w:(p + 1) * w])

    ti = lax.broadcasted_iota(jnp.int32, (w, w), 0)
    si = lax.broadcasted_iota(jnp.int32, (w, w), 1)
    causal = si <= ti
    for g in range(SGU_GROUPS):
        sl = slice(g * SGU_CH, (g + 1) * SGU_CH)
        x = vg_ref[:, sl].astype(jnp.float32)
        mu = jnp.mean(x, axis=-1, keepdims=True)
        xc = x - mu
        var = jnp.mean(xc * xc, axis=-1, keepdims=True)
        vn = xc * lax.rsqrt(var + LN_EPS) * lng_ref[:, sl] + lnb_ref[:, sl]
        wsg = _bf16(jnp.where(causal, ws_ref[g], 0.0))
        sv = _dot(wsg, _bf16(vn)) + bs_ref[g]
        sgu_ref[:, sl] = _bf16(u_ref[:, sl].astype(jnp.float32) * sv)


def _mixers(h, hkv, sinks, ln_g, ln_b, w_s, b_s):
    w = WINDOW
    nb = N_TOK // w
    grid_spec = pltpu.PrefetchScalarGridSpec(
        num_scalar_prefetch=0,
        grid=(nb,),
        in_specs=[
            pl.BlockSpec(memory_space=pltpu.SMEM),
            pl.BlockSpec((w, ATTN_W), lambda n: (n, H_Q // ATTN_W)),
            pl.BlockSpec((w, 2 * KV_W), lambda n: (n, 0)),
            pl.BlockSpec((w, 2 * KV_W), lambda n: (jnp.maximum(n - 1, 0), 0)),
            pl.BlockSpec((w, SGU_W), lambda n: (n, H_U // SGU_W)),
            pl.BlockSpec((w, SGU_W), lambda n: (n, H_VG // SGU_W)),
            pl.BlockSpec((1, SGU_W), lambda n: (0, 0)),
            pl.BlockSpec((1, SGU_W), lambda n: (0, 0)),
            pl.BlockSpec((SGU_GROUPS, w, w), lambda n: (0, 0, 0)),
            pl.BlockSpec((SGU_GROUPS, w, 1), lambda n: (0, 0, 0)),
        ],
        out_specs=[
            pl.BlockSpec((w, ATTN_W), lambda n: (n, 0)),
            pl.BlockSpec((w, SGU_W), lambda n: (n, 0)),
        ],
    )
    return pl.pallas_call(
        _mixers_kernel,
        grid_spec=grid_spec,
        out_shape=[jax.ShapeDtypeStruct((N_TOK, ATTN_W), jnp.bfloat16),
                   jax.ShapeDtypeStruct((N_TOK, SGU_W), jnp.bfloat16)],
        compiler_params=pltpu.CompilerParams(
            dimension_semantics=("arbitrary",),
            vmem_limit_bytes=_vmem_limit(
                ((w, ATTN_W), jnp.bfloat16, 4), ((w, 2 * KV_W), jnp.bfloat16, 4),
                ((w, SGU_W), jnp.bfloat16, 6), ((SGU_GROUPS, w, w), jnp.float32, 2),
                ((SGU_GROUPS, w, LANES), jnp.float32, 2),
                ((GQA // 2 * w, 2 * w), jnp.float32, 16))),
        name="mixers",
    )(sinks, h, hkv, hkv, h, h, ln_g.reshape(1, SGU_W), ln_b.reshape(1, SGU_W), w_s,
      b_s.reshape(SGU_GROUPS, w, 1))


def _layer_norm(z, g, b):
    mu = jnp.mean(z, axis=-1, keepdims=True)
    zc = z - mu
    var = jnp.mean(zc * zc, axis=-1, keepdims=True)
    return zc * lax.rsqrt(var + LN_EPS) * g + b


def _merge_kernel(attn_ref, sgu_ref, ga_ref, gb_ref, x_ref, wa_ref, wb_ref, wo_ref, g_ref, b_ref,
                  wr_ref, x1_ref, lg_ref, z_ref):
    i = pl.program_id(0)
    nt = pl.num_programs(0) - 1

    def project(slot):
        mix = (ga_ref[...].astype(jnp.float32) * _dot(attn_ref[...], wa_ref[...])
               + gb_ref[...].astype(jnp.float32) * _dot(sgu_ref[...], wb_ref[...]))
        z_ref[slot] = ALPHA * x_ref[...] + _dot(_bf16(mix), wo_ref[...])

    def norm_and_route(slot):
        x1 = _layer_norm(z_ref[slot], g_ref[...], b_ref[...])
        x1_ref[...] = x1
        x_hi = _bf16(x1)
        x_lo = _bf16(x1 - x_hi.astype(jnp.float32))
        parts = _dot(x_hi, wr_ref[...]) + _dot(x_lo, wr_ref[...])
        lg_ref[...] = parts + pltpu.roll(parts, N_EXPERTS, axis=1)

    @pl.when(i == 0)
    def _():
        project(0)

    @pl.when((i >= 1) & (i < nt))
    def _():
        norm_and_route((i - 1) % 2)
        project(i % 2)

    @pl.when(i == nt)
    def _():
        norm_and_route((i - 1) % 2)


def _router_parts(w_router):
    hi = _bf16(w_router)
    lo = _bf16(w_router - hi.astype(jnp.float32))
    return jnp.concatenate([hi, lo], axis=1)


def _merge(attn, sgu, h, x2d, wa, wb, wo, g, b, wr):
    tm = MERGE_TM
    nt = N_TOK // tm
    resident = pl.Buffered(1)
    cur = lambda i: jnp.minimum(i, nt - 1)
    prev = lambda i: jnp.maximum(i - 1, 0)
    grid_spec = pltpu.PrefetchScalarGridSpec(
        num_scalar_prefetch=0,
        grid=(nt + 1,),
        in_specs=[
            pl.BlockSpec((tm, ATTN_W), lambda i: (cur(i), 0)),
            pl.BlockSpec((tm, SGU_W), lambda i: (cur(i), 0)),
            pl.BlockSpec((tm, D_MODEL), lambda i: (cur(i), H_GA // D_MODEL)),
            pl.BlockSpec((tm, D_MODEL), lambda i: (cur(i), H_GB // D_MODEL)),
            pl.BlockSpec((tm, D_MODEL), lambda i: (cur(i), 0)),
            pl.BlockSpec((ATTN_W, D_MODEL), lambda i: (0, 0), pipeline_mode=resident),
            pl.BlockSpec((SGU_W, D_MODEL), lambda i: (0, 0), pipeline_mode=resident),
            pl.BlockSpec((D_MODEL, D_MODEL), lambda i: (0, 0), pipeline_mode=resident),
            pl.BlockSpec((1, D_MODEL), lambda i: (0, 0)),
            pl.BlockSpec((1, D_MODEL), lambda i: (0, 0)),
            pl.BlockSpec((D_MODEL, 2 * N_EXPERTS), lambda i: (0, 0)),
        ],
        out_specs=[
            pl.BlockSpec((tm, D_MODEL), lambda i: (prev(i), 0)),
            pl.BlockSpec((tm, 2 * N_EXPERTS), lambda i: (prev(i), 0)),
        ],
        scratch_shapes=[pltpu.VMEM((2, tm, D_MODEL), jnp.float32)],
    )
    x1, lg = pl.pallas_call(
        _merge_kernel,
        grid_spec=grid_spec,
        out_shape=[jax.ShapeDtypeStruct((N_TOK, D_MODEL), jnp.float32),
                   jax.ShapeDtypeStruct((N_TOK, 2 * N_EXPERTS), jnp.float32)],
        compiler_params=pltpu.CompilerParams(
            dimension_semantics=("arbitrary",),
            vmem_limit_bytes=_vmem_limit(
                ((tm, ATTN_W), jnp.bfloat16, 2), ((tm, SGU_W), jnp.bfloat16, 2),
                ((tm, D_MODEL), jnp.bfloat16, 4), ((tm, D_MODEL), jnp.float32, 4),
                ((ATTN_W + SGU_W + D_MODEL, D_MODEL), jnp.bfloat16, 1),
                ((D_MODEL, 2 * N_EXPERTS), jnp.bfloat16, 2), ((2, tm, D_MODEL), jnp.float32, 1),
                ((tm, D_MODEL), jnp.float32, 8))),
        name="merge",
    )(attn, sgu, h, h, x2d, wa, wb, wo, g, b, wr)
    return x1, lg[:, :N_EXPERTS].T


def _first_argmax(v, rows):
    m = jnp.max(v, axis=0, keepdims=True)
    i = jnp.min(jnp.where(v == m, rows, float(v.shape[0])), axis=0, keepdims=True)
    return m, i


def _row_index(shape):
    return lax.broadcasted_iota(jnp.int32, shape, 0).astype(jnp.float32)


def _route_kernel(lg_ref, bias_ref, idx_ref, w_ref):
    tt = lg_ref.shape[1]
    neg_inf = jnp.float32(-jnp.inf)
    scores = jax.nn.sigmoid(lg_ref[...])
    biased = scores + bias_ref[...]
    row_g = _row_index((GROUP_SIZE, tt))
    gs = []
    for g in range(N_EXPERT_GROUPS):
        blk = biased[g * GROUP_SIZE:(g + 1) * GROUP_SIZE]
        m1, i1 = _first_argmax(blk, row_g)
        m2 = jnp.max(jnp.where(row_g == i1, neg_inf, blk), axis=0, keepdims=True)
        gs.append(m1 + m2)
    cur = jnp.concatenate(gs, axis=0)
    row_n = _row_index((N_EXPERT_GROUPS, tt))
    sel = jnp.zeros((N_EXPERT_GROUPS, tt), jnp.float32)
    for _ in range(TOPK_GROUPS):
        _, i = _first_argmax(cur, row_n)
        hit = row_n == i
        sel = jnp.where(hit, 1.0, sel)
        cur = jnp.where(hit, neg_inf, cur)
    emask = jnp.concatenate(
        [jnp.broadcast_to(sel[g:g + 1], (GROUP_SIZE, tt)) for g in range(N_EXPERT_GROUPS)], axis=0)
    masked = jnp.where(emask > 0.5, biased, neg_inf)
    row_e = _row_index((N_EXPERTS, tt))
    idx_rows, w_rows = [], []
    for _ in range(TOP_K):
        _, i = _first_argmax(masked, row_e)
        hit = row_e == i
        w_rows.append(jnp.sum(jnp.where(hit, scores, 0.0), axis=0, keepdims=True))
        idx_rows.append(i)
        masked = jnp.where(hit, neg_inf, masked)
    wsel = jnp.concatenate(w_rows, axis=0)
    idx_ref[...] = jnp.concatenate(idx_rows, axis=0).astype(jnp.int32)
    w_ref[...] = wsel / (jnp.sum(wsel, axis=0, keepdims=True) + 1e-20) * ROUTED_SCALE


def _route(logits_t, bias):
    tt = ROUTE_TT
    return pl.pallas_call(
        _route_kernel,
        grid=(N_TOK // tt,),
        in_specs=[pl.BlockSpec((N_EXPERTS, tt), lambda i: (0, i)),
                  pl.BlockSpec((N_EXPERTS, 1), lambda i: (0, 0))],
        out_specs=[pl.BlockSpec((TOP_K, tt), lambda i: (0, i)),
                   pl.BlockSpec((TOP_K, tt), lambda i: (0, i))],
        out_shape=[jax.ShapeDtypeStruct((TOP_K, N_TOK), jnp.int32),
                   jax.ShapeDtypeStruct((TOP_K, N_TOK), jnp.float32)],
        compiler_params=pltpu.CompilerParams(dimension_semantics=("arbitrary",)),
        name="route",
    )(logits_t, bias.reshape(N_EXPERTS, 1))


def _dispatch_plan(idx_t):
    bm = EXPERT_BM
    nb = N_EXPERT_BLOCKS
    flat_e = idx_t.reshape(-1)
    counts = jnp.sum((flat_e[:, None] == jnp.arange(N_EXPERTS, dtype=jnp.int32)[None, :])
                     .astype(jnp.int32), axis=0)
    e_ids = jnp.arange(N_EXPERTS, dtype=jnp.int32)
    upto = (e_ids[:, None] <= e_ids[None, :]).astype(jnp.int32)
    padded = (counts + bm - 1) // bm * bm
    pad_end = jnp.sum(padded[:, None] * upto, axis=0)
    fill_end = jnp.sum((padded - counts)[:, None] * upto, axis=0)
    n_fill = N_SLOTS - N_ASSIGN
    fill_key = jnp.sum((jnp.arange(n_fill, dtype=jnp.int32)[:, None] >= fill_end[None, :])
                       .astype(jnp.int32), axis=1)
    keys = jnp.concatenate([flat_e, fill_key])
    pos_bits = (N_SLOTS - 1).bit_length()
    slot = jnp.arange(N_SLOTS, dtype=jnp.int32)
    src = jnp.sort((keys << pos_bits) | slot) & ((1 << pos_bits) - 1)
    real = src < N_ASSIGN
    dst = jnp.where(real, src, N_ASSIGN + (slot & (2 * bm - 1)))
    dump_block = N_ASSIGN + jnp.arange(bm, dtype=jnp.int32)
    dst = jnp.concatenate([dump_block, dst, dst[-bm:]]).reshape(nb + 2, 1, bm)
    n_used = jnp.sum(jnp.any(real.reshape(nb, bm), axis=1).astype(jnp.int32))
    block_start = jnp.arange(nb + 1, dtype=jnp.int32) * bm
    block_e = jnp.sum((block_start[:, None] >= pad_end[None, :]).astype(jnp.int32), axis=1)
    block_e = jnp.minimum(block_e, N_EXPERTS - 1).astype(jnp.int32)
    prev_e = jnp.concatenate([jnp.full((1,), -1, jnp.int32), block_e[:-1]])
    fresh = (block_e != prev_e).astype(jnp.int32)
    later = (counts > 0)[None, :] & (e_ids[None, :] > block_e[:, None])
    next_e = jnp.min(jnp.where(later, e_ids[None, :], N_EXPERTS), axis=1).astype(jnp.int32)
    return dst, n_used.reshape(1), block_e, fresh, next_e


def _experts_kernel(nused_ref, be_ref, fresh_ref, nxt_ref, dstp_ref, dstc_ref, dstn_ref, dstnn_ref,
                    x_hbm, w1_hbm, w3_hbm, w2_hbm, y_hbm, xbuf, obuf, w1s, w3s, w2s, w1b, w3b, w2b,
                    gsem, ssem, wsem):
    b = pl.program_id(0)
    bm = EXPERT_BM
    slot = b % 2
    gslot = lax.rem(b, N_GATHER_BUFS)
    n_used = nused_ref[0]

    def start_gather(idx_ref, s):
        for r in range(bm):
            tok = idx_ref[0, 0, r] & (N_TOK - 1)
            pltpu.make_async_copy(x_hbm.at[pl.ds(tok, 1)], xbuf.at[s, pl.ds(r, 1)],
                                  gsem.at[s]).start()

    def wait_gather(s):
        pltpu.make_async_copy(x_hbm.at[pl.ds(0, bm)], xbuf.at[s], gsem.at[s]).wait()

    def start_scatter_prev(s):
        for r in range(bm):
            row = dstp_ref[0, 0, r]
            pltpu.make_async_copy(obuf.at[s, pl.ds(r, 1)], y_hbm.at[pl.ds(row, 1)],
                                  ssem.at[s]).start()

    def wait_scatter(s):
        pltpu.make_async_copy(obuf.at[s], y_hbm.at[pl.ds(0, bm)], ssem.at[s]).wait()

    def weight_copies(e):
        return (pltpu.make_async_copy(w1_hbm.at[e], w1s, wsem.at[0]),
                pltpu.make_async_copy(w3_hbm.at[e], w3s, wsem.at[1]),
                pltpu.make_async_copy(w2_hbm.at[e], w2s, wsem.at[2]))

    def block_step(w1v, w3v, w2v):
        start_scatter_prev(1 - slot)
        xb = _bf16(xbuf[gslot])
        a = _bf16(jax.nn.silu(_dot(xb, w1v)) * _dot(xb, w3v))
        start_gather(dstnn_ref, lax.rem(b + 2, N_GATHER_BUFS))
        obuf[slot] = _dot(a, w2v)

    @pl.when(b == 0)
    def _():
        for c in weight_copies(be_ref[0]):
            c.start(priority=1)
        obuf[1] = jnp.zeros(obuf.shape[1:], obuf.dtype)
        fill = pltpu.make_async_copy(obuf.at[1], y_hbm.at[pl.ds(N_ASSIGN + bm, bm)], ssem.at[0])
        fill.start()
        fill.wait()
        start_gather(dstc_ref, 0)
        start_gather(dstn_ref, 1)

    @pl.when((b >= 1) & (b <= n_used))
    def _():
        wait_scatter(slot)

    @pl.when(b < n_used)
    def _():
        wait_gather(gslot)

        @pl.when(fresh_ref[b] == 1)
        def _():
            for c in weight_copies(be_ref[b]):
                c.wait()
            w1v, w3v, w2v = _bf16(w1s[...]), _bf16(w3s[...]), _bf16(w2s[...])
            w1b[...] = w1v
            w3b[...] = w3v
            w2b[...] = w2v
            block_step(w1v, w3v, w2v)

            @pl.when(nxt_ref[b] < N_EXPERTS)
            def _():
                for c in weight_copies(nxt_ref[b]):
                    c.start(priority=1)

        @pl.when(fresh_ref[b] == 0)
        def _():
            block_step(w1b[...], w3b[...], w2b[...])

    @pl.when(b == n_used)
    def _():
        wait_gather(gslot)
        wait_gather(lax.rem(b + 1, N_GATHER_BUFS))
        start_scatter_prev(1 - slot)
        wait_scatter(1 - slot)


def _experts(x1, w1, w3, w2, dst, n_used, block_e, fresh, next_e):
    bm = EXPERT_BM
    nb = N_EXPERT_BLOCKS
    smem_blk = lambda f: pl.BlockSpec((1, 1, bm), f, memory_space=pltpu.SMEM)
    hbm = pl.BlockSpec(memory_space=pl.ANY)
    grid_spec = pltpu.PrefetchScalarGridSpec(
        num_scalar_prefetch=4,
        grid=(nb + 1,),
        in_specs=[
            smem_blk(lambda b, *_: (b, 0, 0)),
            smem_blk(lambda b, *_: (b + 1, 0, 0)),
            smem_blk(lambda b, *_: (jnp.minimum(b + 2, nb + 1), 0, 0)),
            smem_blk(lambda b, *_: (jnp.minimum(b + 3, nb + 1), 0, 0)),
            hbm, hbm, hbm, hbm,
        ],
        out_specs=hbm,
        scratch_shapes=[
            pltpu.VMEM((N_GATHER_BUFS, bm, D_MODEL), jnp.float32),
            pltpu.VMEM((2, bm, D_MODEL), jnp.float32),
            pltpu.VMEM((D_MODEL, D_EXPERT), jnp.float32),
            pltpu.VMEM((D_MODEL, D_EXPERT), jnp.float32),
            pltpu.VMEM((D_EXPERT, D_MODEL), jnp.float32),
            pltpu.VMEM((D_MODEL, D_EXPERT), jnp.bfloat16),
            pltpu.VMEM((D_MODEL, D_EXPERT), jnp.bfloat16),
            pltpu.VMEM((D_EXPERT, D_MODEL), jnp.bfloat16),
            pltpu.SemaphoreType.DMA((N_GATHER_BUFS,)),
            pltpu.SemaphoreType.DMA((2,)),
            pltpu.SemaphoreType.DMA((3,)),
        ],
    )
    return pl.pallas_call(
        _experts_kernel,
        grid_spec=grid_spec,
        out_shape=jax.ShapeDtypeStruct((N_ASSIGN + 2 * bm, D_MODEL), jnp.float32),
        compiler_params=pltpu.CompilerParams(
            dimension_semantics=("arbitrary",),
            vmem_limit_bytes=_vmem_limit(
                ((N_GATHER_BUFS + 2, bm, D_MODEL), jnp.float32, 1),
                ((3, D_MODEL, D_EXPERT), jnp.float32, 1),
                ((3, D_MODEL, D_EXPERT), jnp.bfloat16, 2),
                ((bm, D_MODEL), jnp.float32, 2), ((bm, D_EXPERT), jnp.float32, 4))),
        name="experts",
    )(n_used, block_e, fresh, next_e, dst, dst, dst, dst, x1, w1, w3, w2)


def _combine_kernel(*refs):
    y_refs = refs[:TOP_K]
    w_ref, x_ref, s1_ref, s3_ref, s2_ref, g_ref, b_ref, o_ref = refs[TOP_K:]
    x1 = x_ref[...]
    xb = _bf16(x1)
    a = _bf16(jax.nn.silu(_dot(xb, s1_ref[...])) * _dot(xb, s3_ref[...]))
    ffn = _dot(a, s2_ref[...])
    wts = w_ref[...]
    routed = wts[:, 0:1] * y_refs[0][...]
    for k in range(1, TOP_K):
        routed = routed + wts[:, k:k + 1] * y_refs[k][...]
    o_ref[...] = _layer_norm(ALPHA * x1 + (routed + ffn), g_ref[...], b_ref[...])


def _combine(y, wts, x1, s1, s3, s2, g, b):
    tm = COMBINE_TM
    nt = N_TOK // tm
    y_specs = [pl.BlockSpec((tm, D_MODEL), functools.partial(lambda i, k: (k * nt + i, 0), k=k))
               for k in range(TOP_K)]
    return pl.pallas_call(
        _combine_kernel,
        grid=(nt,),
        in_specs=y_specs + [
            pl.BlockSpec((tm, TOP_K), lambda i: (i, 0)),
            pl.BlockSpec((tm, D_MODEL), lambda i: (i, 0)),
            pl.BlockSpec((D_MODEL, D_SHARED), lambda i: (0, 0)),
            pl.BlockSpec((D_MODEL, D_SHARED), lambda i: (0, 0)),
            pl.BlockSpec((D_SHARED, D_MODEL), lambda i: (0, 0)),
            pl.BlockSpec((1, D_MODEL), lambda i: (0, 0)),
            pl.BlockSpec((1, D_MODEL), lambda i: (0, 0)),
        ],
        out_specs=pl.BlockSpec((tm, D_MODEL), lambda i: (i, 0)),
        out_shape=jax.ShapeDtypeStruct((N_TOK, D_MODEL), jnp.float32),
        compiler_params=pltpu.CompilerParams(
            dimension_semantics=("arbitrary",),
            vmem_limit_bytes=_vmem_limit(
                ((tm, D_MODEL), jnp.float32, 2 * (TOP_K + 2)), ((tm, LANES), jnp.float32, 2),
                ((3, D_MODEL, D_SHARED), jnp.bfloat16, 2), ((tm, D_MODEL), jnp.float32, 6))),
        name="combine",
    )(*([y] * TOP_K), wts, x1, s1, s3, s2, g, b)


def kernel(x, w_in, b_in, sinks, sgu_ln_g, sgu_ln_b, w_spatial, b_spatial, w_branch_attn,
           w_branch_sgu, w_out, ln1_g, ln1_b, w_router, router_bias, w1, w3, w2, ws1, ws3, ws2,
           ln2_g, ln2_b):
    assert x.shape == (BATCH, SEQ, D_MODEL) and w_in.shape == (1, D_MODEL, IN_W)
    x2d = x.reshape(N_TOK, D_MODEL)
    hkv, x_bf = _kv_proj(x2d, w_in[0], b_in)
    h = _in_proj(x_bf, w_in[0], b_in)
    attn, sgu = _mixers(h, hkv, sinks[0], sgu_ln_g[0], sgu_ln_b[0], w_spatial[0], b_spatial[0])
    x1, logits_t = _merge(attn, sgu, h, x2d, _bf16(w_branch_attn[0]), _bf16(w_branch_sgu[0]),
                          _bf16(w_out[0]), ln1_g, ln1_b, _router_parts(w_router[0]))
    idx_t, w_t = _route(logits_t, router_bias[0])
    y = _experts(x1, w1[0], w3[0], w2[0], *_dispatch_plan(idx_t))
    out = _combine(y, w_t.T, x1, _bf16(ws1[0]), _bf16(ws3[0]), _bf16(ws2[0]), ln2_g, ln2_b)
    return out.reshape(BATCH, SEQ, D_MODEL)
```

```python
import functools
import math

import numpy as np
import jax
import jax.numpy as jnp
from jax import lax
from jax.experimental import pallas as pl
from jax.experimental.pallas import tpu as pltpu

D_MODEL = 2048
BATCH = 2
SEQ = 4096
N_TOK = BATCH * SEQ
N_Q_HEADS = 32
N_KV_HEADS = 4
HEAD_DIM = 64
GQA = N_Q_HEADS // N_KV_HEADS
WINDOW = 128
ROPE_THETA = 500000.0
ROT_DIM = HEAD_DIM // 4
SGU_GROUPS = 8
SGU_CH = 128
N_EXPERTS = 64
N_EXPERT_GROUPS = 8
GROUP_SIZE = N_EXPERTS // N_EXPERT_GROUPS
TOPK_GROUPS = 4
TOP_K = 8
D_EXPERT = 512
D_SHARED = 512
ROUTED_SCALE = 2.5
ATTN_W = N_Q_HEADS * HEAD_DIM
KV_W = N_KV_HEADS * HEAD_DIM
SGU_W = SGU_GROUPS * SGU_CH
IN_W = ATTN_W + 2 * KV_W + 2 * SGU_W + 2 * D_MODEL
ALPHA = 2.0 ** 0.25
LN_EPS = 1e-5
LOG2_E = math.log2(math.e)
N_ASSIGN = N_TOK * TOP_K

LANES = 128
V7X_VMEM_BYTES = 64 * 1024 * 1024


def _vmem_limit(*buffers):
    need = sum(copies * math.prod(shape) * jnp.dtype(dtype).itemsize
               for shape, dtype, copies in buffers)
    assert need <= V7X_VMEM_BYTES * 7 // 8, need
    return need

PROJ_TM = 1024
PROJ_TN = 1024
MERGE_TM = 256
ROUTE_TT = 512
EXPERT_BM = 256
N_GATHER_BUFS = 3
COMBINE_TM = 128
N_EXPERT_BLOCKS = -(-(N_ASSIGN + N_EXPERTS * (EXPERT_BM - 1)) // EXPERT_BM)
N_SLOTS = N_EXPERT_BLOCKS * EXPERT_BM

H_Q, H_GA, H_GB, H_U, H_VG = 0, 2048, 4096, 6144, 7168
H_W = 8192
PROJ_UNIT = 512
_SRC_UNIT = np.array([0, 1, 2, 3, 9, 10, 11, 12, 13, 14, 15, 16, 5, 6, 7, 8], np.int32)
_SRC_UNIT_KV = ATTN_W // PROJ_UNIT
_N_PROJ_TILES = H_W // PROJ_TN
_Q_TILES = ATTN_W // PROJ_TN
_GATE_END = H_U // PROJ_TN


def _bf16(a):
    return a.astype(jnp.bfloat16)


def _dot(a, b):
    return jnp.dot(a, b, preferred_element_type=jnp.float32)


def _dot_nt(a, b):
    return lax.dot_general(a, b, (((1,), (1,)), ((), ())), preferred_element_type=jnp.float32)


def _rope_slab(x, c, s_next, s_prev):
    return (x * c + pltpu.roll(x, LANES - ROT_DIM // 2, axis=1) * s_next
            + pltpu.roll(x, ROT_DIM // 2, axis=1) * s_prev)


def _in_proj_kernel(src_ref, x_ref, wa_ref, wb_ref, ba_ref, bb_ref, c_ref, sn_ref, sp_ref, o_ref,
                    wbf_ref):
    j = pl.program_id(0)
    i = pl.program_id(1)

    @pl.when(i == 0)
    def _():
        wbf_ref[:, :PROJ_UNIT] = _bf16(wa_ref[...])
        wbf_ref[:, PROJ_UNIT:] = _bf16(wb_ref[...])

    def project():
        bias = jnp.concatenate([ba_ref[...], bb_ref[...]], axis=1)
        return _dot(x_ref[...], wbf_ref[...]) + bias

    @pl.when(j < _Q_TILES)
    def _():
        acc = project()
        c, sn, sp = c_ref[...], sn_ref[...], sp_ref[...]
        scale = HEAD_DIM ** -0.5 * LOG2_E
        for t in range(PROJ_TN // LANES):
            sl = slice(t * LANES, (t + 1) * LANES)
            o_ref[:, sl] = _bf16(_rope_slab(acc[:, sl], c, sn, sp) * scale)

    @pl.when((j >= _Q_TILES) & (j < _GATE_END))
    def _():
        o_ref[...] = _bf16(jax.nn.sigmoid(project()))

    @pl.when(j >= _GATE_END)
    def _():
        o_ref[...] = _bf16(jax.nn.gelu(project()))


def _kv_proj_kernel(x_ref, w_ref, b_ref, c_ref, sn_ref, sp_ref, o_ref, xb_ref, wbf_ref):
    @pl.when(pl.program_id(0) == 0)
    def _():
        wbf_ref[...] = _bf16(w_ref[...])

    xb = _bf16(x_ref[...])
    xb_ref[...] = xb
    acc = _dot(xb, wbf_ref[...]) + b_ref[...]
    c, sn, sp = c_ref[...], sn_ref[...], sp_ref[...]
    for t in range(2 * KV_W // LANES):
        sl = slice(t * LANES, (t + 1) * LANES)
        if t < KV_W // LANES:
            o_ref[:, sl] = _bf16(_rope_slab(acc[:, sl], c, sn, sp))
        else:
            o_ref[:, sl] = _bf16(acc[:, sl])


def _rope_tables():
    half = ROT_DIM // 2
    inv_freq = ROPE_THETA ** (-np.arange(0, ROT_DIM, 2, dtype=np.float32) / ROT_DIM)
    pos = np.arange(SEQ, dtype=np.float32)
    ang = jnp.asarray(pos[:, None] * inv_freq[None, :].astype(np.float32), jnp.float32)
    cos, sin = jnp.cos(ang), jnp.sin(ang)
    ones = jnp.ones((SEQ, HEAD_DIM - ROT_DIM), jnp.float32)
    zeros = jnp.zeros((SEQ, HEAD_DIM - ROT_DIM), jnp.float32)
    zh = jnp.zeros((SEQ, half), jnp.float32)
    c = jnp.concatenate([cos, cos, ones], axis=1)
    s_next = jnp.concatenate([-sin, zh, zeros], axis=1)
    s_prev = jnp.concatenate([zh, sin, zeros], axis=1)
    rep = LANES // HEAD_DIM
    return tuple(jnp.tile(t, (1, rep)) for t in (c, s_next, s_prev))


def _in_proj(x_bf, w_in, b_in):
    c, sn, sp = _rope_tables()
    n_i = N_TOK // PROJ_TM
    pos_tiles = SEQ // PROJ_TM
    tbl = pl.BlockSpec((PROJ_TM, LANES), lambda j, i, src: (i % pos_tiles, 0))
    grid_spec = pltpu.PrefetchScalarGridSpec(
        num_scalar_prefetch=1,
        grid=(_N_PROJ_TILES, n_i),
        in_specs=[
            pl.BlockSpec((PROJ_TM, D_MODEL), lambda j, i, src: (i, 0)),
            pl.BlockSpec((D_MODEL, PROJ_UNIT), lambda j, i, src: (0, src[2 * j])),
            pl.BlockSpec((D_MODEL, PROJ_UNIT), lambda j, i, src: (0, src[2 * j + 1])),
            pl.BlockSpec((1, PROJ_UNIT), lambda j, i, src: (0, src[2 * j])),
            pl.BlockSpec((1, PROJ_UNIT), lambda j, i, src: (0, src[2 * j + 1])),
            tbl, tbl, tbl,
        ],
        out_specs=pl.BlockSpec((PROJ_TM, PROJ_TN), lambda j, i, src: (i, j)),
        scratch_shapes=[pltpu.VMEM((D_MODEL, PROJ_TN), jnp.bfloat16)],
    )
    return pl.pallas_call(
        _in_proj_kernel,
        grid_spec=grid_spec,
        out_shape=jax.ShapeDtypeStruct((N_TOK, H_W), jnp.bfloat16),
        compiler_params=pltpu.CompilerParams(
            dimension_semantics=("arbitrary", "arbitrary"),
            vmem_limit_bytes=_vmem_limit(
                ((PROJ_TM, D_MODEL), jnp.bfloat16, 2), ((D_MODEL, PROJ_TN), jnp.float32, 2),
                ((PROJ_TM, LANES), jnp.float32, 6), ((PROJ_TM, PROJ_TN), jnp.bfloat16, 2),
                ((D_MODEL, PROJ_TN), jnp.bfloat16, 1), ((PROJ_TM, PROJ_TN), jnp.float32, 3))),
        name="in_proj",
    )(jnp.asarray(_SRC_UNIT), x_bf, w_in, w_in, b_in, b_in, c, sn, sp)


def _kv_proj(x2d, w_in, b_in):
    c, sn, sp = _rope_tables()
    tm = PROJ_TM // 2
    pos_tiles = SEQ // tm
    tbl = pl.BlockSpec((tm, LANES), lambda i: (i % pos_tiles, 0))
    return pl.pallas_call(
        _kv_proj_kernel,
        grid=(N_TOK // tm,),
        in_specs=[
            pl.BlockSpec((tm, D_MODEL), lambda i: (i, 0)),
            pl.BlockSpec((D_MODEL, 2 * KV_W), lambda i: (0, _SRC_UNIT_KV)),
            pl.BlockSpec((1, 2 * KV_W), lambda i: (0, _SRC_UNIT_KV)),
            tbl, tbl, tbl,
        ],
        out_specs=[pl.BlockSpec((tm, 2 * KV_W), lambda i: (i, 0)),
                   pl.BlockSpec((tm, D_MODEL), lambda i: (i, 0))],
        out_shape=[jax.ShapeDtypeStruct((N_TOK, 2 * KV_W), jnp.bfloat16),
                   jax.ShapeDtypeStruct((N_TOK, D_MODEL), jnp.bfloat16)],
        scratch_shapes=[pltpu.VMEM((D_MODEL, 2 * KV_W), jnp.bfloat16)],
        compiler_params=pltpu.CompilerParams(
            dimension_semantics=("arbitrary",),
            vmem_limit_bytes=_vmem_limit(
                ((tm, D_MODEL), jnp.float32, 2), ((tm, D_MODEL), jnp.bfloat16, 3),
                ((D_MODEL, 2 * KV_W), jnp.float32, 2), ((D_MODEL, 2 * KV_W), jnp.bfloat16, 1),
                ((tm, LANES), jnp.float32, 6), ((tm, 2 * KV_W), jnp.float32, 3))),
        name="kv_proj",
    )(x2d, w_in, b_in, c, sn, sp)


def _mixers_kernel(sink_ref, q_ref, kvc_ref, kvp_ref, u_ref, vg_ref, lng_ref, lnb_ref,
                   ws_ref, bs_ref, attn_ref, sgu_ref):
    n = pl.program_id(0)
    w = WINDOW
    first_key = jnp.where((n % (SEQ // w)) == 0, w, 0)
    pairs = GQA // 2

    qi = lax.broadcasted_iota(jnp.int32, (w, 2 * w), 0)
    kj = lax.broadcasted_iota(jnp.int32, (w, 2 * w), 1)
    valid = (kj > qi) & (kj <= qi + w) & (kj >= first_key)
    lane = lax.broadcasted_iota(jnp.int32, (2 * w, LANES), 1)
    low = lane < HEAD_DIM
    low_r = lax.broadcasted_iota(jnp.int32, (w, LANES), 1) < HEAD_DIM
    ones_low = jnp.where(low, 1.0, 0.0).astype(jnp.bfloat16)
    ones_high = jnp.where(low, 0.0, 1.0).astype(jnp.bfloat16)
    neg_inf = jnp.float32(-jnp.inf)

    kv = jnp.concatenate([kvp_ref[...], kvc_ref[...]], axis=0).astype(jnp.float32)

    def padded(group, head_is_high):
        rolled = pltpu.roll(group, HEAD_DIM, axis=1)
        if head_is_high:
            lo_half, hi_half = rolled, group
        else:
            lo_half, hi_half = group, rolled
        return (_bf16(jnp.where(low, lo_half, 0.0)), _bf16(jnp.where(low, 0.0, hi_half)))

    for h in range(N_KV_HEADS):
        g0 = (h // 2) * LANES
        k_lo, k_hi = padded(kv[:, g0:g0 + LANES], h % 2 == 1)
        v_lo, v_hi = padded(kv[:, KV_W + g0:KV_W + g0 + LANES], h % 2 == 1)
        r_even = jnp.concatenate([v_lo, ones_low], axis=1)
        r_odd = jnp.concatenate([v_hi, ones_high], axis=1)
        q4 = jnp.concatenate(
            [q_ref[:, (h * pairs + p) * LANES:(h * pairs + p + 1) * LANES] for p in range(pairs)],
            axis=0)
        s_e = _dot_nt(q4, k_lo)
        s_o = _dot_nt(q4, k_hi)
        p_e, p_o, sink_terms = [], [], []
        for p in range(pairs):
            rs = slice(p * w, (p + 1) * w)
            sink_e = sink_ref[h * GQA + 2 * p] * LOG2_E
            sink_o = sink_ref[h * GQA + 2 * p + 1] * LOG2_E
            se = jnp.where(valid, s_e[rs], neg_inf)
            so = jnp.where(valid, s_o[rs], neg_inf)
            m_e = jnp.maximum(jnp.max(se, axis=1, keepdims=True), sink_e)
            m_o = jnp.maximum(jnp.max(so, axis=1, keepdims=True), sink_o)
            p_e.append(_bf16(jnp.exp2(se - m_e)))
            p_o.append(_bf16(jnp.exp2(so - m_o)))
            sink_terms.append(jnp.exp2(jnp.where(low_r, sink_e - m_e, sink_o - m_o)))
        acc = (_dot(jnp.concatenate(p_e, axis=0), r_even)
               + _dot(jnp.concatenate(p_o, axis=0), r_odd))
        out = acc[:, :LANES] / (acc[:, LANES:] + jnp.concatenate(sink_terms, axis=0))
        for p in range(pairs):
            c0 = (h * pairs + p) * LANES
            attn_ref[:, c0:c0 + LANES] = _bf16(out[p * w:(p + 1) * w])

    ti = lax.broadcasted_iota(jnp.int32, (w, w), 0)
    si = lax.broadcasted_iota(jnp.int32, (w, w), 1)
    causal = si <= ti
    for g in range(SGU_GROUPS):
        sl = slice(g * SGU_CH, (g + 1) * SGU_CH)
        x = vg_ref[:, sl].astype(jnp.float32)
        mu = jnp.mean(x, axis=-1, keepdims=True)
        xc = x - mu
        var = jnp.mean(xc * xc, axis=-1, keepdims=True)
        vn = xc * lax.rsqrt(var + LN_EPS) * lng_ref[:, sl] + lnb_ref[:, sl]
        wsg = _bf16(jnp.where(causal, ws_ref[g], 0.0))
        sv = _dot(wsg, _bf16(vn)) + bs_ref[g]
        sgu_ref[:, sl] = _bf16(u_ref[:, sl].astype(jnp.float32) * sv)


def _mixers(h, hkv, sinks, ln_g, ln_b, w_s, b_s):
    w = WINDOW
    nb = N_TOK // w
    grid_spec = pltpu.PrefetchScalarGridSpec(
        num_scalar_prefetch=0,
        grid=(nb,),
        in_specs=[
            pl.BlockSpec(memory_space=pltpu.SMEM),
            pl.BlockSpec((w, ATTN_W), lambda n: (n, H_Q // ATTN_W)),
            pl.BlockSpec((w, 2 * KV_W), lambda n: (n, 0)),
            pl.BlockSpec((w, 2 * KV_W), lambda n: (jnp.maximum(n - 1, 0), 0)),
            pl.BlockSpec((w, SGU_W), lambda n: (n, H_U // SGU_W)),
            pl.BlockSpec((w, SGU_W), lambda n: (n, H_VG // SGU_W)),
            pl.BlockSpec((1, SGU_W), lambda n: (0, 0)),
            pl.BlockSpec((1, SGU_W), lambda n: (0, 0)),
            pl.BlockSpec((SGU_GROUPS, w, w), lambda n: (0, 0, 0)),
            pl.BlockSpec((SGU_GROUPS, w, 1), lambda n: (0, 0, 0)),
        ],
        out_specs=[
            pl.BlockSpec((w, ATTN_W), lambda n: (n, 0)),
            pl.BlockSpec((w, SGU_W), lambda n: (n, 0)),
        ],
    )
    return pl.pallas_call(
        _mixers_kernel,
        grid_spec=grid_spec,
        out_shape=[jax.ShapeDtypeStruct((N_TOK, ATTN_W), jnp.bfloat16),
                   jax.ShapeDtypeStruct((N_TOK, SGU_W), jnp.bfloat16)],
        compiler_params=pltpu.CompilerParams(
            dimension_semantics=("arbitrary",),
            vmem_limit_bytes=_vmem_limit(
                ((w, ATTN_W), jnp.bfloat16, 4), ((w, 2 * KV_W), jnp.bfloat16, 4),
                ((w, SGU_W), jnp.bfloat16, 6), ((SGU_GROUPS, w, w), jnp.float32, 2),
                ((SGU_GROUPS, w, LANES), jnp.float32, 2),
                ((GQA // 2 * w, 2 * w), jnp.float32, 16))),
        name="mixers",
    )(sinks, h, hkv, hkv, h, h, ln_g.reshape(1, SGU_W), ln_b.reshape(1, SGU_W), w_s,
      b_s.reshape(SGU_GROUPS, w, 1))


def _layer_norm(z, g, b):
    mu = jnp.mean(z, axis=-1, keepdims=True)
    zc = z - mu
    var = jnp.mean(zc * zc, axis=-1, keepdims=True)
    return zc * lax.rsqrt(var + LN_EPS) * g + b


def _merge_kernel(attn_ref, sgu_ref, ga_ref, gb_ref, x_ref, wa_ref, wb_ref, wo_ref, g_ref, b_ref,
                  wr_ref, x1_ref, lg_ref, z_ref):
    i = pl.program_id(0)
    nt = pl.num_programs(0) - 1

    def project(slot):
        mix = (ga_ref[...].astype(jnp.float32) * _dot(attn_ref[...], wa_ref[...])
               + gb_ref[...].astype(jnp.float32) * _dot(sgu_ref[...], wb_ref[...]))
        z_ref[slot] = ALPHA * x_ref[...] + _dot(_bf16(mix), wo_ref[...])

    def norm_and_route(slot):
        x1 = _layer_norm(z_ref[slot], g_ref[...], b_ref[...])
        x1_ref[...] = x1
        x_hi = _bf16(x1)
        x_lo = _bf16(x1 - x_hi.astype(jnp.float32))
        parts = _dot(x_hi, wr_ref[...]) + _dot(x_lo, wr_ref[...])
        lg_ref[...] = parts + pltpu.roll(parts, N_EXPERTS, axis=1)

    @pl.when(i == 0)
    def _():
        project(0)

    @pl.when((i >= 1) & (i < nt))
    def _():
        norm_and_route((i - 1) % 2)
        project(i % 2)

    @pl.when(i == nt)
    def _():
        norm_and_route((i - 1) % 2)


def _router_parts(w_router):
    hi = _bf16(w_router)
    lo = _bf16(w_router - hi.astype(jnp.float32))
    return jnp.concatenate([hi, lo], axis=1)


def _merge(attn, sgu, h, x2d, wa, wb, wo, g, b, wr):
    tm = MERGE_TM
    nt = N_TOK // tm
    resident = pl.Buffered(1)
    cur = lambda i: jnp.minimum(i, nt - 1)
    prev = lambda i: jnp.maximum(i - 1, 0)
    grid_spec = pltpu.PrefetchScalarGridSpec(
        num_scalar_prefetch=0,
        grid=(nt + 1,),
        in_specs=[
            pl.BlockSpec((tm, ATTN_W), lambda i: (cur(i), 0)),
            pl.BlockSpec((tm, SGU_W), lambda i: (cur(i), 0)),
            pl.BlockSpec((tm, D_MODEL), lambda i: (cur(i), H_GA // D_MODEL)),
            pl.BlockSpec((tm, D_MODEL), lambda i: (cur(i), H_GB // D_MODEL)),
            pl.BlockSpec((tm, D_MODEL), lambda i: (cur(i), 0)),
            pl.BlockSpec((ATTN_W, D_MODEL), lambda i: (0, 0), pipeline_mode=resident),
            pl.BlockSpec((SGU_W, D_MODEL), lambda i: (0, 0), pipeline_mode=resident),
            pl.BlockSpec((D_MODEL, D_MODEL), lambda i: (0, 0), pipeline_mode=resident),
            pl.BlockSpec((1, D_MODEL), lambda i: (0, 0)),
            pl.BlockSpec((1, D_MODEL), lambda i: (0, 0)),
            pl.BlockSpec((D_MODEL, 2 * N_EXPERTS), lambda i: (0, 0)),
        ],
        out_specs=[
            pl.BlockSpec((tm, D_MODEL), lambda i: (prev(i), 0)),
            pl.BlockSpec((tm, 2 * N_EXPERTS), lambda i: (prev(i), 0)),
        ],
        scratch_shapes=[pltpu.VMEM((2, tm, D_MODEL), jnp.float32)],
    )
    x1, lg = pl.pallas_call(
        _merge_kernel,
        grid_spec=grid_spec,
        out_shape=[jax.ShapeDtypeStruct((N_TOK, D_MODEL), jnp.float32),
                   jax.ShapeDtypeStruct((N_TOK, 2 * N_EXPERTS), jnp.float32)],
        compiler_params=pltpu.CompilerParams(
            dimension_semantics=("arbitrary",),
            vmem_limit_bytes=_vmem_limit(
                ((tm, ATTN_W), jnp.bfloat16, 2), ((tm, SGU_W), jnp.bfloat16, 2),
                ((tm, D_MODEL), jnp.bfloat16, 4), ((tm, D_MODEL), jnp.float32, 4),
                ((ATTN_W + SGU_W + D_MODEL, D_MODEL), jnp.bfloat16, 1),
                ((D_MODEL, 2 * N_EXPERTS), jnp.bfloat16, 2), ((2, tm, D_MODEL), jnp.float32, 1),
                ((tm, D_MODEL), jnp.float32, 8))),
        name="merge",
    )(attn, sgu, h, h, x2d, wa, wb, wo, g, b, wr)
    return x1, lg[:, :N_EXPERTS].T


def _first_argmax(v, rows):
    m = jnp.max(v, axis=0, keepdims=True)
    i = jnp.min(jnp.where(v == m, rows, float(v.shape[0])), axis=0, keepdims=True)
    return m, i


def _row_index(shape):
    return lax.broadcasted_iota(jnp.int32, shape, 0).astype(jnp.float32)


def _route_kernel(lg_ref, bias_ref, idx_ref, w_ref):
    tt = lg_ref.shape[1]
    neg_inf = jnp.float32(-jnp.inf)
    scores = jax.nn.sigmoid(lg_ref[...])
    biased = scores + bias_ref[...]
    row_g = _row_index((GROUP_SIZE, tt))
    gs = []
    for g in range(N_EXPERT_GROUPS):
        blk = biased[g * GROUP_SIZE:(g + 1) * GROUP_SIZE]
        m1, i1 = _first_argmax(blk, row_g)
        m2 = jnp.max(jnp.where(row_g == i1, neg_inf, blk), axis=0, keepdims=True)
        gs.append(m1 + m2)
    cur = jnp.concatenate(gs, axis=0)
    row_n = _row_index((N_EXPERT_GROUPS, tt))
    sel = jnp.zeros((N_EXPERT_GROUPS, tt), jnp.float32)
    for _ in range(TOPK_GROUPS):
        _, i = _first_argmax(cur, row_n)
        hit = row_n == i
        sel = jnp.where(hit, 1.0, sel)
        cur = jnp.where(hit, neg_inf, cur)
    emask = jnp.concatenate(
        [jnp.broadcast_to(sel[g:g + 1], (GROUP_SIZE, tt)) for g in range(N_EXPERT_GROUPS)], axis=0)
    masked = jnp.where(emask > 0.5, biased, neg_inf)
    row_e = _row_index((N_EXPERTS, tt))
    idx_rows, w_rows = [], []
    for _ in range(TOP_K):
        _, i = _first_argmax(masked, row_e)
        hit = row_e == i
        w_rows.append(jnp.sum(jnp.where(hit, scores, 0.0), axis=0, keepdims=True))
        idx_rows.append(i)
        masked = jnp.where(hit, neg_inf, masked)
    wsel = jnp.concatenate(w_rows, axis=0)
    idx_ref[...] = jnp.concatenate(idx_rows, axis=0).astype(jnp.int32)
    w_ref[...] = wsel / (jnp.sum(wsel, axis=0, keepdims=True) + 1e-20) * ROUTED_SCALE


def _route(logits_t, bias):
    tt = ROUTE_TT
    return pl.pallas_call(
        _route_kernel,
        grid=(N_TOK // tt,),
        in_specs=[pl.BlockSpec((N_EXPERTS, tt), lambda i: (0, i)),
                  pl.BlockSpec((N_EXPERTS, 1), lambda i: (0, 0))],
        out_specs=[pl.BlockSpec((TOP_K, tt), lambda i: (0, i)),
                   pl.BlockSpec((TOP_K, tt), lambda i: (0, i))],
        out_shape=[jax.ShapeDtypeStruct((TOP_K, N_TOK), jnp.int32),
                   jax.ShapeDtypeStruct((TOP_K, N_TOK), jnp.float32)],
        compiler_params=pltpu.CompilerParams(dimension_semantics=("arbitrary",)),
        name="route",
    )(logits_t, bias.reshape(N_EXPERTS, 1))


def _dispatch_plan(idx_t):
    bm = EXPERT_BM
    nb = N_EXPERT_BLOCKS
    flat_e = idx_t.reshape(-1)
    counts = jnp.sum((flat_e[:, None] == jnp.arange(N_EXPERTS, dtype=jnp.int32)[None, :])
                     .astype(jnp.int32), axis=0)
    e_ids = jnp.arange(N_EXPERTS, dtype=jnp.int32)
    upto = (e_ids[:, None] <= e_ids[None, :]).astype(jnp.int32)
    padded = (counts + bm - 1) // bm * bm
    pad_end = jnp.sum(padded[:, None] * upto, axis=0)
    fill_end = jnp.sum((padded - counts)[:, None] * upto, axis=0)
    n_fill = N_SLOTS - N_ASSIGN
    fill_key = jnp.sum((jnp.arange(n_fill, dtype=jnp.int32)[:, None] >= fill_end[None, :])
                       .astype(jnp.int32), axis=1)
    keys = jnp.concatenate([flat_e, fill_key])
    pos_bits = (N_SLOTS - 1).bit_length()
    slot = jnp.arange(N_SLOTS, dtype=jnp.int32)
    src = jnp.sort((keys << pos_bits) | slot) & ((1 << pos_bits) - 1)
    real = src < N_ASSIGN
    dst = jnp.where(real, src, N_ASSIGN + (slot & (2 * bm - 1)))
    dump_block = N_ASSIGN + jnp.arange(bm, dtype=jnp.int32)
    dst = jnp.concatenate([dump_block, dst, dst[-bm:]]).reshape(nb + 2, 1, bm)
    n_used = jnp.sum(jnp.any(real.reshape(nb, bm), axis=1).astype(jnp.int32))
    block_start = jnp.arange(nb + 1, dtype=jnp.int32) * bm
    block_e = jnp.sum((block_start[:, None] >= pad_end[None, :]).astype(jnp.int32), axis=1)
    block_e = jnp.minimum(block_e, N_EXPERTS - 1).astype(jnp.int32)
    prev_e = jnp.concatenate([jnp.full((1,), -1, jnp.int32), block_e[:-1]])
    fresh = (block_e != prev_e).astype(jnp.int32)
    later = (counts > 0)[None, :] & (e_ids[None, :] > block_e[:, None])
    next_e = jnp.min(jnp.where(later, e_ids[None, :], N_EXPERTS), axis=1).astype(jnp.int32)
    return dst, n_used.reshape(1), block_e, fresh, next_e


def _experts_kernel(nused_ref, be_ref, fresh_ref, nxt_ref, dstp_ref, dstc_ref, dstn_ref, dstnn_ref,
                    x_hbm, w1_hbm, w3_hbm, w2_hbm, y_hbm, xbuf, obuf, w1s, w3s, w2s, w1b, w3b, w2b,
                    gsem, ssem, wsem):
    b = pl.program_id(0)
    bm = EXPERT_BM
    slot = b % 2
    gslot = lax.rem(b, N_GATHER_BUFS)
    n_used = nused_ref[0]

    def start_gather(idx_ref, s):
        for r in range(bm):
            tok = idx_ref[0, 0, r] & (N_TOK - 1)
            pltpu.make_async_copy(x_hbm.at[pl.ds(tok, 1)], xbuf.at[s, pl.ds(r, 1)],
                                  gsem.at[s]).start()

    def wait_gather(s):
        pltpu.make_async_copy(x_hbm.at[pl.ds(0, bm)], xbuf.at[s], gsem.at[s]).wait()

    def start_scatter_prev(s):
        for r in range(bm):
            row = dstp_ref[0, 0, r]
            pltpu.make_async_copy(obuf.at[s, pl.ds(r, 1)], y_hbm.at[pl.ds(row, 1)],
                                  ssem.at[s]).start()

    def wait_scatter(s):
        pltpu.make_async_copy(obuf.at[s], y_hbm.at[pl.ds(0, bm)], ssem.at[s]).wait()

    def weight_copies(e):
        return (pltpu.make_async_copy(w1_hbm.at[e], w1s, wsem.at[0]),
                pltpu.make_async_copy(w3_hbm.at[e], w3s, wsem.at[1]),
                pltpu.make_async_copy(w2_hbm.at[e], w2s, wsem.at[2]))

    def block_step(w1v, w3v, w2v):
        start_scatter_prev(1 - slot)
        xb = _bf16(xbuf[gslot])
        a = _bf16(jax.nn.silu(_dot(xb, w1v)) * _dot(xb, w3v))
        start_gather(dstnn_ref, lax.rem(b + 2, N_GATHER_BUFS))
        obuf[slot] = _dot(a, w2v)

    @pl.when(b == 0)
    def _():
        for c in weight_copies(be_ref[0]):
            c.start(priority=1)
        obuf[1] = jnp.zeros(obuf.shape[1:], obuf.dtype)
        fill = pltpu.make_async_copy(obuf.at[1], y_hbm.at[pl.ds(N_ASSIGN + bm, bm)], ssem.at[0])
        fill.start()
        fill.wait()
        start_gather(dstc_ref, 0)
        start_gather(dstn_ref, 1)

    @pl.when((b >= 1) & (b <= n_used))
    def _():
        wait_scatter(slot)

    @pl.when(b < n_used)
    def _():
        wait_gather(gslot)

        @pl.when(fresh_ref[b] == 1)
        def _():
            for c in weight_copies(be_ref[b]):
                c.wait()
            w1v, w3v, w2v = _bf16(w1s[...]), _bf16(w3s[...]), _bf16(w2s[...])
            w1b[...] = w1v
            w3b[...] = w3v
            w2b[...] = w2v
            block_step(w1v, w3v, w2v)

            @pl.when(nxt_ref[b] < N_EXPERTS)
            def _():
                for c in weight_copies(nxt_ref[b]):
                    c.start(priority=1)

        @pl.when(fresh_ref[b] == 0)
        def _():
            block_step(w1b[...], w3b[...], w2b[...])

    @pl.when(b == n_used)
    def _():
        wait_gather(gslot)
        wait_gather(lax.rem(b + 1, N_GATHER_BUFS))
        start_scatter_prev(1 - slot)
        wait_scatter(1 - slot)


def _experts(x1, w1, w3, w2, dst, n_used, block_e, fresh, next_e):
    bm = EXPERT_BM
    nb = N_EXPERT_BLOCKS
    smem_blk = lambda f: pl.BlockSpec((1, 1, bm), f, memory_space=pltpu.SMEM)
    hbm = pl.BlockSpec(memory_space=pl.ANY)
    grid_spec = pltpu.PrefetchScalarGridSpec(
        num_scalar_prefetch=4,
        grid=(nb + 1,),
        in_specs=[
            smem_blk(lambda b, *_: (b, 0, 0)),
            smem_blk(lambda b, *_: (b + 1, 0, 0)),
            smem_blk(lambda b, *_: (jnp.minimum(b + 2, nb + 1), 0, 0)),
            smem_blk(lambda b, *_: (jnp.minimum(b + 3, nb + 1), 0, 0)),
            hbm, hbm, hbm, hbm,
        ],
        out_specs=hbm,
        scratch_shapes=[
            pltpu.VMEM((N_GATHER_BUFS, bm, D_MODEL), jnp.float32),
            pltpu.VMEM((2, bm, D_MODEL), jnp.float32),
            pltpu.VMEM((D_MODEL, D_EXPERT), jnp.float32),
            pltpu.VMEM((D_MODEL, D_EXPERT), jnp.float32),
            pltpu.VMEM((D_EXPERT, D_MODEL), jnp.float32),
            pltpu.VMEM((D_MODEL, D_EXPERT), jnp.bfloat16),
            pltpu.VMEM((D_MODEL, D_EXPERT), jnp.bfloat16),
            pltpu.VMEM((D_EXPERT, D_MODEL), jnp.bfloat16),
            pltpu.SemaphoreType.DMA((N_GATHER_BUFS,)),
            pltpu.SemaphoreType.DMA((2,)),
            pltpu.SemaphoreType.DMA((3,)),
        ],
    )
    return pl.pallas_call(
        _experts_kernel,
        grid_spec=grid_spec,
        out_shape=jax.ShapeDtypeStruct((N_ASSIGN + 2 * bm, D_MODEL), jnp.float32),
        compiler_params=pltpu.CompilerParams(
            dimension_semantics=("arbitrary",),
            vmem_limit_bytes=_vmem_limit(
                ((N_GATHER_BUFS + 2, bm, D_MODEL), jnp.float32, 1),
                ((3, D_MODEL, D_EXPERT), jnp.float32, 1),
                ((3, D_MODEL, D_EXPERT), jnp.bfloat16, 2),
                ((bm, D_MODEL), jnp.float32, 2), ((bm, D_EXPERT), jnp.float32, 4))),
        name="experts",
    )(n_used, block_e, fresh, next_e, dst, dst, dst, dst, x1, w1, w3, w2)


def _combine_kernel(*refs):
    y_refs = refs[:TOP_K]
    w_ref, x_ref, s1_ref, s3_ref, s2_ref, g_ref, b_ref, o_ref = refs[TOP_K:]
    x1 = x_ref[...]
    xb = _bf16(x1)
    a = _bf16(jax.nn.silu(_dot(xb, s1_ref[...])) * _dot(xb, s3_ref[...]))
    ffn = _dot(a, s2_ref[...])
    wts = w_ref[...]
    routed = wts[:, 0:1] * y_refs[0][...]
    for k in range(1, TOP_K):
        routed = routed + wts[:, k:k + 1] * y_refs[k][...]
    o_ref[...] = _layer_norm(ALPHA * x1 + (routed + ffn), g_ref[...], b_ref[...])


def _combine(y, wts, x1, s1, s3, s2, g, b):
    tm = COMBINE_TM
    nt = N_TOK // tm
    y_specs = [pl.BlockSpec((tm, D_MODEL), functools.partial(lambda i, k: (k * nt + i, 0), k=k))
               for k in range(TOP_K)]
    return pl.pallas_call(
        _combine_kernel,
        grid=(nt,),
        in_specs=y_specs + [
            pl.BlockSpec((tm, TOP_K), lambda i: (i, 0)),
            pl.BlockSpec((tm, D_MODEL), lambda i: (i, 0)),
            pl.BlockSpec((D_MODEL, D_SHARED), lambda i: (0, 0)),
            pl.BlockSpec((D_MODEL, D_SHARED), lambda i: (0, 0)),
            pl.BlockSpec((D_SHARED, D_MODEL), lambda i: (0, 0)),
            pl.BlockSpec((1, D_MODEL), lambda i: (0, 0)),
            pl.BlockSpec((1, D_MODEL), lambda i: (0, 0)),
        ],
        out_specs=pl.BlockSpec((tm, D_MODEL), lambda i: (i, 0)),
        out_shape=jax.ShapeDtypeStruct((N_TOK, D_MODEL), jnp.float32),
        compiler_params=pltpu.CompilerParams(
            dimension_semantics=("arbitrary",),
            vmem_limit_bytes=_vmem_limit(
                ((tm, D_MODEL), jnp.float32, 2 * (TOP_K + 2)), ((tm, LANES), jnp.float32, 2),
                ((3, D_MODEL, D_SHARED), jnp.bfloat16, 2), ((tm, D_MODEL), jnp.float32, 6))),
        name="combine",
    )(*([y] * TOP_K), wts, x1, s1, s3, s2, g, b)


def kernel(x, w_in, b_in, sinks, sgu_ln_g, sgu_ln_b, w_spatial, b_spatial, w_branch_attn,
           w_branch_sgu, w_out, ln1_g, ln1_b, w_router, router_bias, w1, w3, w2, ws1, ws3, ws2,
           ln2_g, ln2_b):
    assert x.shape == (BATCH, SEQ, D_MODEL) and w_in.shape == (1, D_MODEL, IN_W)
    x2d = x.reshape(N_TOK, D_MODEL)
    hkv, x_bf = _kv_proj(x2d, w_in[0], b_in)
    h = _in_proj(x_bf, w_in[0], b_in)
    attn, sgu = _mixers(h, hkv, sinks[0], sgu_ln_g[0], sgu_ln_b[0], w_spatial[0], b_spatial[0])
    x1, logits_t = _merge(attn, sgu, h, x2d, _bf16(w_branch_attn[0]), _bf16(w_branch_sgu[0]),
                          _bf16(w_out[0]), ln1_g, ln1_b, _router_parts(w_router[0]))
    idx_t, w_t = _route(logits_t, router_bias[0])
    y = _experts(x1, w1[0], w3[0], w2[0], *_dispatch_plan(idx_t))
    out = _combine(y, w_t.T, x1, _bf16(ws1[0]), _bf16(ws3[0]), _bf16(ws2[0]), ln2_g, ln2_b)
    return out.reshape(BATCH, SEQ, D_MODEL)
```
